```python
import math
import jax, jax.numpy as jnp
from jax import lax
import numpy as np

D_MODEL = 1024
BATCH = 8
SEQ = 2048
DEPTH = 4
DEC_BATCH = 128
DEC_SEQ = 4
PAST_LEN = 8192
PAGE_SIZE = 128

D_CONV_A = D_MODEL
CONV_A_WIDTH = 3
D_SSM = D_MODEL
SSM_HEAD_DIM = 64
SSM_HEADS = D_SSM // SSM_HEAD_DIM
SSM_GROUPS = 2
SSM_STATE = 64
SSM_CONV_WIDTH = 4
SSM_CONV_DIM = D_SSM + 2 * SSM_GROUPS * SSM_STATE
SSD_CHUNK = 128
ATTN_HEADS = 16
KV_HEADS = 4
HEAD_DIM = 64
Q_PER_KV = ATTN_HEADS // KV_HEADS
D_ATTN = ATTN_HEADS * HEAD_DIM
D_KV = KV_HEADS * HEAD_DIM
WINDOW = 128
ROPE_THETA = 500000.0
ROPE_DIMS = HEAD_DIM // 4
N_EXPERTS = 32
TOP_K = 4
D_FF = D_MODEL
SWIGLU_LIMIT = 7.0
SWIGLU_ALPHA = 1.702
MOE_BLOCK = 128

N_BRANCHES = 3
RMS_EPS = 1e-5

IN_A = 3 * D_CONV_A
IN_SSM = D_SSM + SSM_CONV_DIM + SSM_HEADS
IN_ATTN = D_ATTN + 2 * D_KV
IN_DIM = IN_A + IN_SSM + IN_ATTN

kernel_name = 'gated_hybrid_conv_ssd_swa_moe_step'

F32 = jnp.float32


def rmsnorm(x, g):
    x32 = x.astype(F32)
    y = x32 * lax.rsqrt(jnp.mean(x32 * x32, axis=-1, keepdims=True) + RMS_EPS)
    return (y * g.astype(F32)).astype(x.dtype)


def gated_rmsnorm(y, z, g):
    b, L, d = y.shape
    u = (y * jax.nn.silu(z)).astype(F32).reshape(b, L, SSM_GROUPS, d // SSM_GROUPS)
    u = u * lax.rsqrt(jnp.mean(u * u, axis=-1, keepdims=True) + RMS_EPS)
    return (u.reshape(b, L, d) * g.astype(F32)).astype(y.dtype)


def causal_depthwise_conv(x, buf, w):
    xp = jnp.concatenate([buf.astype(x.dtype), x], axis=1)
    y = lax.conv_general_dilated(xp, w[:, None, :].astype(x.dtype), window_strides=(1,), padding='VALID',
                                 dimension_numbers=('NWC', 'WIO', 'NWC'), feature_group_count=x.shape[-1])
    return y, xp[:, xp.shape[1] - (w.shape[0] - 1):]


def partial_rope(x, pos):
    half = ROPE_DIMS // 2
    inv_freq = jnp.exp(-math.log(ROPE_THETA) * jnp.arange(half, dtype=F32) / half)
    ang = pos.astype(F32)[:, None] * inv_freq[None, :]
    cos = jnp.cos(ang)[:, None, :].astype(x.dtype)
    sin = jnp.sin(ang)[:, None, :].astype(x.dtype)
    x1, x2, rest = x[..., :half], x[..., half:ROPE_DIMS], x[..., ROPE_DIMS:]
    return jnp.concatenate([x1 * cos - x2 * sin, x2 * cos + x1 * sin, rest], axis=-1)


def sinks_attention(q, k, v, mask, sinks):
    s = jnp.einsum('...qkgd,...skd->...kgqs', q, k).astype(F32) * (HEAD_DIM ** -0.5)
    s = jnp.where(mask, s, -jnp.inf)
    sink = jnp.broadcast_to(sinks.astype(F32).reshape(KV_HEADS, Q_PER_KV, 1, 1), s.shape[:-1] + (1,))
    p = jax.nn.softmax(jnp.concatenate([s, sink], axis=-1), axis=-1)[..., :-1]
    return jnp.einsum('...kgqs,...skd->...qkgd', p.astype(v.dtype), v)


def banded_window_attention(q, k, v, sinks):
    b, L = q.shape[:2]
    nb = L // WINDOW
    qb = q.reshape(b, nb, WINDOW, KV_HEADS, Q_PER_KV, HEAD_DIM)

    def with_prev(t):
        t = t.reshape(b, nb, WINDOW, KV_HEADS, HEAD_DIM)
        prev = jnp.concatenate([jnp.zeros_like(t[:, :1]), t[:, :-1]], axis=1)
        return jnp.concatenate([prev, t], axis=2)

    kk, vv = with_prev(k), with_prev(v)
    blk = jnp.arange(nb)[:, None] * WINDOW
    qpos = blk + jnp.arange(WINDOW)[None, :]
    kpos = blk - WINDOW + jnp.arange(2 * WINDOW)[None, :]
    diff = qpos[:, :, None] - kpos[:, None, :]
    mask = (diff >= 0) & (diff < WINDOW) & (kpos[:, None, :] >= 0)
    o = sinks_attention(qb, kk, vv, mask[:, None, None], sinks)
    return o.reshape(b, L, D_ATTN)


def cached_window_attention(q, k, v, k_buf, v_buf, sinks):
    b, L = q.shape[:2]
    kk = jnp.concatenate([k_buf.astype(k.dtype), k], axis=1)
    vv = jnp.concatenate([v_buf.astype(v.dtype), v], axis=1)
    qpos = PAST_LEN + jnp.arange(L)
    kpos = jnp.concatenate([PAST_LEN - WINDOW + jnp.arange(WINDOW), PAST_LEN + jnp.arange(L)])
    diff = qpos[:, None] - kpos[None, :]
    mask = (diff >= 0) & (diff < WINDOW)
    o = sinks_attention(q.reshape(b, L, KV_HEADS, Q_PER_KV, HEAD_DIM), kk, vv, mask, sinks)
    n = kk.shape[1]
    return o.reshape(b, L, D_ATTN), kk[:, n - WINDOW:], vv[:, n - WINDOW:]


def ssd_chunked_scan(x, dt, a, bm, cm, s0, chunk):
    b, L, H, P = x.shape
    N = bm.shape[-1]
    nc = L // chunk
    xdt = (x.astype(F32) * dt[..., None]).reshape(b, nc, chunk, H, P)
    bc = bm.astype(F32).reshape(b, nc, chunk, H, N)
    cc = cm.astype(F32).reshape(b, nc, chunk, H, N)
    acs = jnp.cumsum((dt * a).reshape(b, nc, chunk, H), axis=2)
    causal = jnp.tril(jnp.ones((chunk, chunk), dtype=bool))[None, None, :, :, None]
    seg = acs[:, :, :, None, :] - acs[:, :, None, :, :]
    decay = jnp.exp(jnp.where(causal, seg, -jnp.inf))
    y_diag = jnp.einsum('bclsh,bcshp->bclhp', jnp.einsum('bclhn,bcshn->bclsh', cc, bc) * decay, xdt)
    chunk_states = jnp.einsum('bcshn,bcsh,bcshp->bchpn', bc, jnp.exp(acs[:, :, -1:, :] - acs), xdt)
    chunk_decay = jnp.exp(acs[:, :, -1, :])

    def carry_step(s, inp):
        st, dec = inp
        return s * dec[:, :, None, None] + st, s

    s_final, s_prev = lax.scan(carry_step, s0.astype(F32),
                               (jnp.moveaxis(chunk_states, 1, 0), jnp.moveaxis(chunk_decay, 1, 0)))
    y_off = jnp.einsum('bclhn,bchpn,bclh->bclhp', cc, jnp.moveaxis(s_prev, 0, 1), jnp.exp(acs))
    return (y_diag + y_off).reshape(b, L, H, P), s_final


def mixing_sublayer(h, pos, conv_a_buf, conv_ssm_buf, ssm_state, k_buf, v_buf, lw, prompt):
    b, L, _ = h.shape
    proj = h @ lw['w_in']
    pa, ps, pc = proj[..., :IN_A], proj[..., IN_A:IN_A + IN_SSM], proj[..., IN_A + IN_SSM:]
    a_b, a_c, a_x = jnp.split(pa, 3, axis=-1)
    conv_a, new_conv_a = causal_depthwise_conv(a_c * a_x, conv_a_buf, lw['conv_a_w'])
    y_a = a_b * conv_a
    z = ps[..., :D_SSM]
    xbc = ps[..., D_SSM:D_SSM + SSM_CONV_DIM]
    dt_raw = ps[..., D_SSM + SSM_CONV_DIM:]
    xbc, new_conv_ssm = causal_depthwise_conv(xbc, conv_ssm_buf, lw['ssm_conv_w'])
    xbc = jax.nn.silu(xbc + lw['ssm_conv_b'])
    gn = SSM_GROUPS * SSM_STATE
    xs = xbc[..., :D_SSM].reshape(b, L, SSM_HEADS, SSM_HEAD_DIM)
    bm = jnp.repeat(xbc[..., D_SSM:D_SSM + gn].reshape(b, L, SSM_GROUPS, SSM_STATE), SSM_HEADS // SSM_GROUPS, axis=2)
    cm = jnp.repeat(xbc[..., D_SSM + gn:].reshape(b, L, SSM_GROUPS, SSM_STATE), SSM_HEADS // SSM_GROUPS, axis=2)
    dt = jax.nn.softplus(dt_raw.astype(F32) + lw['ssm_dt_bias'].astype(F32))
    a = -jnp.exp(lw['ssm_a_log'].astype(F32))
    ys, new_ssm = ssd_chunked_scan(xs, dt, a, bm, cm, ssm_state, min(SSD_CHUNK, L))
    ys = ys.astype(xs.dtype) + lw['ssm_d'][:, None].astype(xs.dtype) * xs
    y_b = gated_rmsnorm(ys.reshape(b, L, D_SSM), z, lw['ssm_norm'])
    q = partial_rope(pc[..., :D_ATTN].reshape(b, L, ATTN_HEADS, HEAD_DIM), pos)
    k = partial_rope(pc[..., D_ATTN:D_ATTN + D_KV].reshape(b, L, KV_HEADS, HEAD_DIM), pos)
    v = pc[..., D_ATTN + D_KV:].reshape(b, L, KV_HEADS, HEAD_DIM)
    if prompt:
        y_c = banded_window_attention(q, k, v, lw['attn_sinks'])
        new_k, new_v = k[:, L - WINDOW:], v[:, L - WINDOW:]
    else:
        y_c, new_k, new_v = cached_window_attention(q, k, v, k_buf, v_buf, lw['attn_sinks'])
    g = jax.nn.sigmoid(h @ lw['w_branch_gate'] + lw['b_branch_gate'])
    g_a, g_b, g_c = jnp.split(g, N_BRANCHES, axis=-1)
    merged = g_a * (y_a @ lw['w_proj_a']) + g_b * (y_b @ lw['w_proj_b']) + g_c * (y_c @ lw['w_proj_c'])
    return merged @ lw['w_out'], (new_conv_a, new_conv_ssm, new_ssm.astype(ssm_state.dtype), new_k, new_v)


def moe_ffn(h, lw):
    t, d = h.shape
    logits = (h @ lw['router_w'] + lw['router_b']).astype(F32)
    top_val, top_idx = lax.top_k(logits, TOP_K)
    gate = jax.nn.softmax(top_val, axis=-1)
    tk = t * TOP_K
    flat_e = top_idx.reshape(tk)
    flat_tok = jnp.repeat(jnp.arange(t, dtype=jnp.int32), TOP_K)
    flat_gate = gate.reshape(tk)
    order = jnp.argsort(flat_e)
    se = flat_e[order]
    counts = jnp.bincount(flat_e, length=N_EXPERTS)
    padded = (counts + MOE_BLOCK - 1) // MOE_BLOCK * MOE_BLOCK
    starts = jnp.cumsum(counts) - counts
    pad_ends = jnp.cumsum(padded)
    pad_starts = pad_ends - padded
    dest = pad_starts[se] + jnp.arange(tk, dtype=jnp.int32) - starts[se]
    n_blocks = -(-tk // MOE_BLOCK) + N_EXPERTS
    n_slots = n_blocks * MOE_BLOCK
    slot_tok = jnp.full((n_slots,), t, jnp.int32).at[dest].set(flat_tok[order])
    slot_gate = jnp.zeros((n_slots,), F32).at[dest].set(flat_gate[order])
    block_expert = jnp.minimum(
        jnp.searchsorted(pad_ends, jnp.arange(n_blocks, dtype=jnp.int32) * MOE_BLOCK, side='right'), N_EXPERTS - 1)
    h_pad = jnp.concatenate([h, jnp.zeros((1, d), h.dtype)], axis=0)
    xb = h_pad[slot_tok].reshape(n_blocks, MOE_BLOCK, d)

    def expert_block(args):
        xe, e = args
        gu = xe @ lw['w_gate_up'][e] + lw['b_gate_up'][e]
        g, u = gu[:, :D_FF], gu[:, D_FF:]
        g = jnp.minimum(g, SWIGLU_LIMIT)
        u = jnp.clip(u, -SWIGLU_LIMIT, SWIGLU_LIMIT)
        act = g * jax.nn.sigmoid(SWIGLU_ALPHA * g) * (u + 1.0)
        return act @ lw['w_down'][e] + lw['b_down'][e]

    out = lax.map(expert_block, (xb, block_expert))
    contrib = out.reshape(n_slots, d).astype(F32) * slot_gate[:, None]
    y = jax.ops.segment_sum(contrib, slot_tok, num_segments=t + 1)[:t]
    return y.astype(h.dtype)


def run_trunk(x, c, pos, states, wts, norm_final):
    prompt = states is None
    b, L, _ = x.shape
    outs = ([], [], [], [], [])
    for l in range(DEPTH):
        lw = {name: arr[l] for name, arr in wts.items()}
        ada = (jax.nn.silu(c) @ lw['w_ada'] + lw['b_ada'])[:, None, :]
        sh1, sc1, g1, sh2, sc2, g2 = jnp.split(ada, 6, axis=-1)
        if prompt:
            bufs = (jnp.zeros((b, CONV_A_WIDTH - 1, D_CONV_A), x.dtype),
                    jnp.zeros((b, SSM_CONV_WIDTH - 1, SSM_CONV_DIM), x.dtype),
                    jnp.zeros((b, SSM_HEADS, SSM_HEAD_DIM, SSM_STATE), x.dtype),
                    None, None)
        else:
            bufs = tuple(s[l] for s in states)
        h = rmsnorm(x, lw['norm_mix']) * (1.0 + sc1) + sh1
        mix, new_bufs = mixing_sublayer(h, pos, *bufs, lw, prompt)
        x = x + g1 * mix
        h = rmsnorm(x, lw['norm_ffn']) * (1.0 + sc2) + sh2
        x = x + g2 * moe_ffn(h.reshape(b * L, D_MODEL), lw).reshape(b, L, D_MODEL)
        for o, nbuf in zip(outs, new_bufs):
            o.append(nbuf)
    y = rmsnorm(x, norm_final)
    return y, tuple(jnp.stack(o) for o in outs)


def setup_inputs(seed: int = 0) -> dict:
    key = jax.random.key(seed)
    ks = iter(jax.random.split(key, 40))

    def nrm(shape, scale):
        return scale * jax.random.normal(next(ks), shape, F32)

    dt0 = jnp.exp(jax.random.uniform(next(ks), (DEPTH, SSM_HEADS), F32, math.log(1e-3), math.log(1e-1)))
    return {
        'x_prompt': nrm((BATCH, SEQ, D_MODEL), 1.0),
        'x_sample': nrm((DEC_BATCH, DEC_SEQ, D_MODEL), 1.0),
        'state_conv_a': nrm((DEPTH, DEC_BATCH, CONV_A_WIDTH - 1, D_CONV_A), 1.0),
        'state_conv_ssm': nrm((DEPTH, DEC_BATCH, SSM_CONV_WIDTH - 1, SSM_CONV_DIM), 1.0),
        'state_ssm': nrm((DEPTH, DEC_BATCH, SSM_HEADS, SSM_HEAD_DIM, SSM_STATE), 0.5),
        'cache_k': nrm((DEPTH, DEC_BATCH, WINDOW, KV_HEADS, HEAD_DIM), 1.0),
        'cache_v': nrm((DEPTH, DEC_BATCH, WINDOW, KV_HEADS, HEAD_DIM), 1.0),
        'c_prompt': nrm((BATCH, D_MODEL), 1.0),
        'c_sample': nrm((DEC_BATCH, D_MODEL), 1.0),
        'w_ada': nrm((DEPTH, D_MODEL, 6 * D_MODEL), 0.5 * D_MODEL ** -0.5),
        'b_ada': nrm((DEPTH, 6 * D_MODEL), 0.02),
        'norm_mix': 1.0 + nrm((DEPTH, D_MODEL), 0.05),
        'norm_ffn': 1.0 + nrm((DEPTH, D_MODEL), 0.05),
        'w_in': nrm((DEPTH, D_MODEL, IN_DIM), D_MODEL ** -0.5),
        'conv_a_w': nrm((DEPTH, CONV_A_WIDTH, D_CONV_A), CONV_A_WIDTH ** -0.5),
        'ssm_conv_w': nrm((DEPTH, SSM_CONV_WIDTH, SSM_CONV_DIM), SSM_CONV_WIDTH ** -0.5),
        'ssm_conv_b': nrm((DEPTH, SSM_CONV_DIM), 0.02),
        'ssm_dt_bias': dt0 + jnp.log(-jnp.expm1(-dt0)),
        'ssm_a_log': jnp.log(jax.random.uniform(next(ks), (DEPTH, SSM_HEADS), F32, 1.0, 16.0)),
        'ssm_d': 1.0 + nrm((DEPTH, SSM_HEADS), 0.1),
        'ssm_norm': 1.0 + nrm((DEPTH, D_SSM), 0.05),
        'attn_sinks': nrm((DEPTH, ATTN_HEADS), 0.5),
        'w_branch_gate': nrm((DEPTH, D_MODEL, N_BRANCHES * D_MODEL), D_MODEL ** -0.5),
        'b_branch_gate': nrm((DEPTH, N_BRANCHES * D_MODEL), 0.02),
        'w_proj_a': nrm((DEPTH, D_CONV_A, D_MODEL), D_CONV_A ** -0.5),
        'w_proj_b': nrm((DEPTH, D_SSM, D_MODEL), D_SSM ** -0.5),
        'w_proj_c': nrm((DEPTH, D_ATTN, D_MODEL), D_ATTN ** -0.5),
        'w_out': nrm((DEPTH, D_MODEL, D_MODEL), D_MODEL ** -0.5),
        'router_w': nrm((DEPTH, D_MODEL, N_EXPERTS), D_MODEL ** -0.5),
        'router_b': nrm((DEPTH, N_EXPERTS), 0.01),
        'w_gate_up': nrm((DEPTH, N_EXPERTS, D_MODEL, 2 * D_FF), D_MODEL ** -0.5),
        'b_gate_up': nrm((DEPTH, N_EXPERTS, 2 * D_FF), 0.01),
        'w_down': nrm((DEPTH, N_EXPERTS, D_FF, D_MODEL), D_FF ** -0.5),
        'b_down': nrm((DEPTH, N_EXPERTS, D_MODEL), 0.01),
        'norm_final': 1.0 + nrm((D_MODEL,), 0.05),
    }


def reference(x_prompt, x_sample, state_conv_a, state_conv_ssm, state_ssm, cache_k, cache_v,
              c_prompt, c_sample, w_ada, b_ada, norm_mix, norm_ffn, w_in, conv_a_w, ssm_conv_w,
              ssm_conv_b, ssm_dt_bias, ssm_a_log, ssm_d, ssm_norm, attn_sinks, w_branch_gate,
              b_branch_gate, w_proj_a, w_proj_b, w_proj_c, w_out, router_w, router_b,
              w_gate_up, b_gate_up, w_down, b_down, norm_final):
    wts = {
        'w_ada': w_ada, 'b_ada': b_ada, 'norm_mix': norm_mix, 'norm_ffn': norm_ffn, 'w_in': w_in,
        'conv_a_w': conv_a_w, 'ssm_conv_w': ssm_conv_w, 'ssm_conv_b': ssm_conv_b,
        'ssm_dt_bias': ssm_dt_bias, 'ssm_a_log': ssm_a_log, 'ssm_d': ssm_d, 'ssm_norm': ssm_norm,
        'attn_sinks': attn_sinks, 'w_branch_gate': w_branch_gate, 'b_branch_gate': b_branch_gate,
        'w_proj_a': w_proj_a, 'w_proj_b': w_proj_b, 'w_proj_c': w_proj_c, 'w_out': w_out,
        'router_w': router_w, 'router_b': router_b, 'w_gate_up': w_gate_up, 'b_gate_up': b_gate_up,
        'w_down': w_down, 'b_down': b_down,
    }
    pos_prompt = jnp.arange(x_prompt.shape[1], dtype=jnp.int32)
    pos_sample = PAST_LEN + jnp.arange(x_sample.shape[1], dtype=jnp.int32)
    y_prompt, (ca_p, cs_p, ss_p, k_p, v_p) = run_trunk(x_prompt, c_prompt, pos_prompt, None, wts, norm_final)
    y_sample, (ca_s, cs_s, ss_s, k_s, v_s) = run_trunk(
        x_sample, c_sample, pos_sample, (state_conv_a, state_conv_ssm, state_ssm, cache_k, cache_v), wts, norm_final)
    return (y_prompt, y_sample, ca_p, ca_s, cs_p, cs_s, ss_p, ss_s, k_p, k_s, v_p, v_s)
```

```python
import functools
import math

import jax
import jax.numpy as jnp
from jax import lax
from jax.experimental import pallas as pl
from jax.experimental.pallas import tpu as pltpu

F32 = jnp.float32
BF16 = jnp.bfloat16
I32 = jnp.int32

PAST_LEN = 8192
SSM_HEADS = 16
SSM_HEAD_DIM = 64
SSM_GROUPS = 2
SSM_STATE = 64
SSM_CONV_WIDTH = 4
CONV_A_WIDTH = 3
SSD_CHUNK = 128
ATTN_HEADS = 16
KV_HEADS = 4
HEAD_DIM = 64
Q_PER_KV = ATTN_HEADS // KV_HEADS
WINDOW = 128
ROPE_THETA = 500000.0
ROPE_DIMS = HEAD_DIM // 4
N_EXPERTS = 32
TOP_K = 4
SWIGLU_LIMIT = 7.0
SWIGLU_ALPHA = 1.702
RMS_EPS = 1e-5
NEG_BIG = -1e30

LANES = 128
SUBLANES = 8
VMEM_LIMIT = 56 * 1024 * 1024

MXU_DTYPE = BF16
C_B, C_C, C_X, C_Z, C_XS, C_Q, C_BC, C_K, C_V, C_DT, N_COL = (
    0, 1024, 2048, 3072, 4096, 5120, 6144, 6400, 6656, 6912, 7168)


def _cp(*sem):
    return pltpu.CompilerParams(dimension_semantics=sem, vmem_limit_bytes=VMEM_LIMIT)


def _mm(a, b):
    return jnp.dot(a.astype(MXU_DTYPE), b.astype(MXU_DTYPE), preferred_element_type=F32)


def _mm_nt(a, b):
    return lax.dot_general(a.astype(MXU_DTYPE), b.astype(MXU_DTYPE), (((1,), (1,)), ((), ())),
                           preferred_element_type=F32)


def _split(v, n):
    if MXU_DTYPE == F32:
        return [v]
    parts, r = [], v
    for _ in range(n):
        p = r.astype(MXU_DTYPE)
        parts.append(p)
        r = r - p.astype(F32)
    return parts


def _mm_sel(v, sel, n=3):
    acc = None
    for p in _split(v, n):
        t = jnp.dot(p, sel, preferred_element_type=F32)
        acc = t if acc is None else acc + t
    return acc


def _sel_mm(sel, v, n=3):
    acc = None
    for p in _split(v, n):
        t = jnp.dot(sel, p, preferred_element_type=F32)
        acc = t if acc is None else acc + t
    return acc


def _rms(x, w):
    return x * lax.rsqrt(jnp.mean(x * x, axis=-1, keepdims=True) + RMS_EPS) * w


def _silu(x):
    return x * (1.0 / (1.0 + jnp.exp(-x)))


def _sigmoid(x):
    return 1.0 / (1.0 + jnp.exp(-x))


def _softplus(x):
    return jnp.maximum(x, 0.0) + jnp.log(1.0 + jnp.exp(-jnp.abs(x)))


def _tile_rows(x, n):
    return x if n == 1 else jnp.concatenate([x] * n, axis=0)


def _tile_lanes(x, n):
    return x if n == 1 else jnp.concatenate([x] * n, axis=1)


def _iota(shape, dim):
    return lax.broadcasted_iota(I32, shape, dim)


def _half_mask(lane, upper):
    return lane >= HEAD_DIM if upper else lane < HEAD_DIM


def _ada_kernel(c_ref, w_ref, b_ref, o_ref):
    a = _silu(c_ref[...])
    o_ref[...] = _mm(a, w_ref[...]) + b_ref[...]


def _ada_all(c_all, w_ada, b_ada):
    depth, d, n6 = w_ada.shape
    nb = c_all.shape[0]
    tn = d
    return pl.pallas_call(
        _ada_kernel,
        grid=(depth, n6 // tn),
        in_specs=[pl.BlockSpec((nb, d), lambda l, j: (0, 0)),
                  pl.BlockSpec((None, d, tn), lambda l, j: (l, 0, j)),
                  pl.BlockSpec((None, 1, tn), lambda l, j: (l, 0, j))],
        out_specs=pl.BlockSpec((None, nb, tn), lambda l, j: (l, 0, j)),
        out_shape=jax.ShapeDtypeStruct((depth, nb, n6), F32),
        compiler_params=_cp("arbitrary", "arbitrary"),
        name="ada",
    )(c_all, w_ada, b_ada.reshape(depth, 1, n6))


def _inproj_kernel(x_ref, nw_ref, shp_ref, scp_ref, shs_ref, scs_ref, w_ref, p_ref, h_ref, *, n_p, reps):
    i = pl.program_id(0)
    j = pl.program_id(1)

    def make_h(sh, sc):
        y = _rms(x_ref[...], nw_ref[...])
        h_ref[...] = (y * (1.0 + sc) + sh).astype(h_ref.dtype)

    @pl.when((j == 0) & (i < n_p))
    def _():
        make_h(shp_ref[...], scp_ref[...])

    @pl.when((j == 0) & (i >= n_p))
    def _():
        make_h(_tile_rows(shs_ref[...], reps), _tile_rows(scs_ref[...], reps))

    p_ref[...] = jnp.dot(h_ref[...], w_ref[...], preferred_element_type=F32).astype(p_ref.dtype)


def _mod_specs(dims, col, grid_rank):
    d, tm, seq, b_p, b_s = dims

    def pidx(i, *_):
        return (jnp.minimum(i * tm // seq, b_p - 1), 0, col)

    def sidx(i, *_):
        return (0, col)

    return pl.BlockSpec((None, 1, d), pidx), pl.BlockSpec((b_s, d), sidx)


def _inproj(x_all, norm_w, ada_p, ada_s, w_pack, dims, tn=1024):
    d, tm, seq, b_p, b_s = dims
    t_all = x_all.shape[0]
    n_t = t_all // tm
    n_p = b_p * seq // tm
    shp, shs = _mod_specs(dims, 0, 2)
    scp, scs = _mod_specs(dims, 1, 2)
    kern = functools.partial(_inproj_kernel, n_p=n_p, reps=tm // b_s)
    return pl.pallas_call(
        kern,
        grid=(n_t, N_COL // tn),
        in_specs=[pl.BlockSpec((tm, d), lambda i, j: (i, 0)),
                  pl.BlockSpec((1, d), lambda i, j: (0, 0)),
                  shp, scp, shs, scs,
                  pl.BlockSpec((d, tn), lambda i, j: (0, j))],
        out_specs=[pl.BlockSpec((tm, tn), lambda i, j: (i, j)),
                   pl.BlockSpec((tm, d), lambda i, j: (i, 0))],
        out_shape=[jax.ShapeDtypeStruct((t_all, N_COL), F32),
                   jax.ShapeDtypeStruct((t_all, d), MXU_DTYPE)],
        compiler_params=_cp("arbitrary", "arbitrary"),
        name="inproj",
    )(x_all, norm_w.reshape(1, d), ada_p, ada_p, ada_s, ada_s, w_pack)


def _conva_p_kernel(b_ref, c_ref, x_ref, w_ref, y_ref, st_ref, buf):
    seq = b_ref.shape[0]
    u = c_ref[...] * x_ref[...]
    buf[0:SUBLANES, :] = jnp.zeros((SUBLANES, buf.shape[1]), F32)
    buf[SUBLANES:, :] = u
    w = w_ref[...]
    acc = w[2:3, :] * u
    acc = acc + w[1:2, :] * buf[pl.ds(SUBLANES - 1, seq), :]
    acc = acc + w[0:1, :] * buf[pl.ds(SUBLANES - 2, seq), :]
    y_ref[...] = b_ref[...] * acc
    st_ref[...] = buf[pl.ds(seq, SUBLANES), :]


def _conva_prompt(p, conv_w, dims, t_all, tc=256):
    d, tm, seq, b_p, b_s = dims
    nc = d // tc
    return pl.pallas_call(
        _conva_p_kernel,
        grid=(b_p, nc),
        in_specs=[pl.BlockSpec((seq, tc), lambda b, c: (b, C_B // tc + c)),
                  pl.BlockSpec((seq, tc), lambda b, c: (b, C_C // tc + c)),
                  pl.BlockSpec((seq, tc), lambda b, c: (b, C_X // tc + c)),
                  pl.BlockSpec((CONV_A_WIDTH, tc), lambda b, c: (0, c))],
        out_specs=[pl.BlockSpec((seq, tc), lambda b, c: (b, c)),
                   pl.BlockSpec((None, SUBLANES, tc), lambda b, c: (b, 0, c))],
        out_shape=[jax.ShapeDtypeStruct((t_all, d), F32),
                   jax.ShapeDtypeStruct((b_p, SUBLANES, d), F32)],
        scratch_shapes=[pltpu.VMEM((seq + SUBLANES, tc), F32)],
        compiler_params=_cp("arbitrary", "arbitrary"),
        name="conva_prompt",
    )(p, p, p, conv_w)


def _conva_s_kernel(b_ref, c_ref, x_ref, s_ref, w_ref, y_ref, st_ref, *, b_s, steps):
    ext = jnp.concatenate([s_ref[...], c_ref[...] * x_ref[...]], axis=0)
    w = w_ref[...]
    acc = None
    for k in range(CONV_A_WIDTH):
        t = w[k:k + 1, :] * ext[k * b_s:(k + steps) * b_s]
        acc = t if acc is None else acc + t
    y_ref[...] = b_ref[...] * acc
    st_ref[...] = ext[steps * b_s:]


def _conva_sample(p, state_tm, conv_w, dims, steps, tc=256):
    d, tm, seq, b_p, b_s = dims
    t_s = b_s * steps
    rb = (b_p * seq) // t_s
    ns = (CONV_A_WIDTH - 1) * b_s
    kern = functools.partial(_conva_s_kernel, b_s=b_s, steps=steps)
    return pl.pallas_call(
        kern,
        grid=(d // tc,),
        in_specs=[pl.BlockSpec((t_s, tc), lambda c: (rb, C_B // tc + c)),
                  pl.BlockSpec((t_s, tc), lambda c: (rb, C_C // tc + c)),
                  pl.BlockSpec((t_s, tc), lambda c: (rb, C_X // tc + c)),
                  pl.BlockSpec((ns, tc), lambda c: (0, c)),
                  pl.BlockSpec((CONV_A_WIDTH, tc), lambda c: (0, c))],
        out_specs=[pl.BlockSpec((t_s, tc), lambda c: (0, c)),
                   pl.BlockSpec((ns, tc), lambda c: (0, c))],
        out_shape=[jax.ShapeDtypeStruct((t_s, d), F32),
                   jax.ShapeDtypeStruct((ns, d), F32)],
        compiler_params=_cp("arbitrary"),
        name="conva_sample",
    )(p, p, p, state_tm, conv_w)


def _ssd_p_kernel(z_ref, xs_ref, bc_ref, dt_ref, wx_ref, wbc_ref, bx_ref, bbc_ref, dtb_ref, alog_ref,
                  de_ref, gn_ref, rexp_ref,
                  y_ref, csx_ref, csbc_ref, stout_ref,
                  xbuf, bcbuf, st, ybuf):
    c = pl.program_id(1)
    q = xs_ref.shape[0]
    d_ssm = xs_ref.shape[1]
    half = d_ssm // SSM_GROUPS

    @pl.when(c == 0)
    def _():
        xbuf[0:SUBLANES, :] = jnp.zeros((SUBLANES, xbuf.shape[1]), F32)
        bcbuf[0:SUBLANES, :] = jnp.zeros((SUBLANES, bcbuf.shape[1]), F32)
        st[...] = jnp.zeros(st.shape, F32)

    xbuf[SUBLANES:, :] = xs_ref[...]
    bcbuf[SUBLANES:, :] = bc_ref[...]

    def conv(buf, w, b):
        acc = b
        for k in range(SSM_CONV_WIDTH):
            acc = acc + w[k:k + 1, :] * buf[pl.ds(SUBLANES - (SSM_CONV_WIDTH - 1) + k, q), :]
        return _silu(acc)

    xa = conv(xbuf, wx_ref[...], bx_ref[...])
    bca = conv(bcbuf, wbc_ref[...], bbc_ref[...])
    csx_ref[...] = xbuf[pl.ds(q, SUBLANES), :]
    csbc_ref[...] = bcbuf[pl.ds(q, SUBLANES), :]
    xbuf[0:SUBLANES, :] = xbuf[pl.ds(q, SUBLANES), :]
    bcbuf[0:SUBLANES, :] = bcbuf[pl.ds(q, SUBLANES), :]

    dtv = _softplus(dt_ref[:, 0:LANES] + dtb_ref[...])
    a = -jnp.exp(alog_ref[...])
    da = dtv * a
    row = _iota((q, q), 0)
    col = _iota((q, q), 1)
    causal = row >= col
    tri = jnp.where(causal, 1.0, 0.0).astype(MXU_DTYPE)
    acs = _sel_mm(tri, da)
    acs_t = acs.T
    rexp = rexp_ref[...]
    acs_e = _mm_sel(acs, rexp)
    dt_e = _mm_sel(dtv, rexp)
    last_e = acs_e[q - 1:q, :]
    exp_e = jnp.exp(acs_e)
    w_e = jnp.exp(last_e - acs_e)
    dec_last = jnp.exp(last_e)

    xdt = xa * dt_e
    xdt_w = xdt * w_e
    bslab = bca[:, 0:LANES]
    cslab = bca[:, LANES:2 * LANES]
    bt = bslab.T
    st_old = st[...]
    y_off = _mm(cslab, st_old) * exp_e
    upd = _mm(bt, xdt_w)
    srow = _iota(st.shape, 0)
    scol = _iota(st.shape, 1)
    diag = (srow >= SSM_STATE) == (scol >= half)
    st[...] = dec_last * st_old + jnp.where(diag, upd, 0.0)

    lane = _iota((q, LANES), 1)
    cbs = []
    for g in range(SSM_GROUPS):
        cm = jnp.where(_half_mask(lane, g == 1), cslab, 0.0)
        cbs.append(_mm(cm, bt))
    heads_per_group = SSM_HEADS // SSM_GROUPS
    for jp in range(SSM_HEADS // 2):
        ms = []
        for hh in (2 * jp, 2 * jp + 1):
            seg = acs[:, hh:hh + 1] - acs_t[hh:hh + 1, :]
            ldec = jnp.exp(jnp.where(causal, seg, NEG_BIG))
            ms.append(cbs[hh // heads_per_group] * ldec)
        lhs = jnp.concatenate(ms, axis=1)
        slab = xdt[:, jp * LANES:(jp + 1) * LANES]
        rhs = jnp.concatenate([jnp.where(lane < SSM_HEAD_DIM, slab, 0.0),
                               jnp.where(lane >= SSM_HEAD_DIM, slab, 0.0)], axis=0)
        ybuf[:, jp * LANES:(jp + 1) * LANES] = _mm(lhs, rhs)

    y = ybuf[...] + y_off + de_ref[...] * xa
    u = y * _silu(z_ref[...])
    outs = []
    for g in range(SSM_GROUPS):
        ug = u[:, g * half:(g + 1) * half]
        outs.append(ug * lax.rsqrt(jnp.mean(ug * ug, axis=-1, keepdims=True) + RMS_EPS))
    y_ref[...] = jnp.concatenate(outs, axis=1) * gn_ref[...]

    @pl.when(c == pl.num_programs(1) - 1)
    def _():
        stout_ref[...] = st[...]


def _ssd_prompt(p, prm, dims, t_all):
    d, tm, seq, b_p, b_s = dims
    q = min(SSD_CHUNK, seq)
    nc = seq // q
    n_bc = 2 * SSM_GROUPS * SSM_STATE
    full = lambda shape: pl.BlockSpec(shape, lambda b, c: tuple(0 for _ in shape))
    return pl.pallas_call(
        _ssd_p_kernel,
        grid=(b_p, nc),
        in_specs=[pl.BlockSpec((q, d), lambda b, c: (b * nc + c, C_Z // d)),
                  pl.BlockSpec((q, d), lambda b, c: (b * nc + c, C_XS // d)),
                  pl.BlockSpec((q, n_bc), lambda b, c: (b * nc + c, C_BC // n_bc)),
                  pl.BlockSpec((q, n_bc), lambda b, c: (b * nc + c, C_DT // n_bc)),
                  full((SSM_CONV_WIDTH, d)), full((SSM_CONV_WIDTH, n_bc)),
                  full((1, d)), full((1, n_bc)), full((1, LANES)), full((1, LANES)),
                  full((1, d)), full((1, d)), full((LANES, d))],
        out_specs=[pl.BlockSpec((q, d), lambda b, c: (b * nc + c, 0)),
                   pl.BlockSpec((None, SUBLANES, d), lambda b, c: (b, 0, 0)),
                   pl.BlockSpec((None, SUBLANES, n_bc), lambda b, c: (b, 0, 0)),
                   pl.BlockSpec((None, 2 * SSM_STATE, d), lambda b, c: (b, 0, 0))],
        out_shape=[jax.ShapeDtypeStruct((t_all, d), F32),
                   jax.ShapeDtypeStruct((b_p, SUBLANES, d), F32),
                   jax.ShapeDtypeStruct((b_p, SUBLANES, n_bc), F32),
                   jax.ShapeDtypeStruct((b_p, 2 * SSM_STATE, d), F32)],
        scratch_shapes=[pltpu.VMEM((q + SUBLANES, d), F32),
                        pltpu.VMEM((q + SUBLANES, n_bc), F32),
                        pltpu.VMEM((2 * SSM_STATE, d), F32),
                        pltpu.VMEM((q, d), F32)],
        compiler_params=_cp("arbitrary", "arbitrary"),
        name="ssd_prompt",
    )(p, p, p, p, prm["wx"], prm["wbc"], prm["bx"], prm["bbc"], prm["dtb"], prm["alog"],
      prm["de"], prm["gn"], prm["rexp"])


def _ssd_s_kernel(xs_ref, bc_ref, dt_ref, cx_ref, cbc_ref, wx_ref, wbc_ref, bx_ref, bbc_ref, dtb_ref,
                  alog_ref, de_ref, e2_ref, g2_ref, s_ref,
                  y_ref, snew_ref, csx_ref, csbc_ref, *, b_s, steps):
    j = pl.program_id(0)
    pn = SSM_HEAD_DIM * SSM_STATE
    extx = jnp.concatenate([cx_ref[...], xs_ref[...]], axis=0)
    extbc = jnp.concatenate([cbc_ref[...], bc_ref[...]], axis=0)

    def conv(ext, w, b):
        acc = b
        for k in range(SSM_CONV_WIDTH):
            acc = acc + w[k:k + 1, :] * ext[k * b_s:(k + steps) * b_s]
        return _silu(acc)

    xa = conv(extx, wx_ref[...], bx_ref[...])
    bca = conv(extbc, wbc_ref[...], bbc_ref[...])
    csx_ref[...] = extx[steps * b_s:]
    csbc_ref[...] = extbc[steps * b_s:]
    dtv = _softplus(dt_ref[:, 0:LANES] + dtb_ref[...])
    dec = jnp.exp(dtv * (-jnp.exp(alog_ref[...])))
    lane = _iota(dtv.shape, 1)
    grp = (2 * j) // (SSM_HEADS // SSM_GROUPS)
    own = (lane // SSM_STATE) == grp
    bslab = bca[:, 0:LANES]
    cslab = bca[:, LANES:2 * LANES]
    b2 = jnp.where(own, bslab, pltpu.roll(bslab, SSM_STATE, 1))
    c2 = jnp.where(own, cslab, pltpu.roll(cslab, SSM_STATE, 1))
    ys = [None] * steps
    for lh in range(2):
        hh = 2 * j + lh
        dt_col = jnp.sum(jnp.where(lane == hh, dtv, 0.0), axis=1, keepdims=True)
        dec_col = jnp.sum(jnp.where(lane == hh, dec, 0.0), axis=1, keepdims=True)
        s = s_ref[:, lh * pn:(lh + 1) * pn]
        for t in range(steps):
            rows = slice(t * b_s, (t + 1) * b_s)
            xe = _mm_sel(xa[rows] * dt_col[rows], e2_ref[lh])
            s = dec_col[rows] * s + xe * _tile_lanes(b2[rows], pn // LANES)
            yt = _mm(s * _tile_lanes(c2[rows], pn // LANES), g2_ref[lh])
            ys[t] = yt if ys[t] is None else ys[t] + yt
        snew_ref[:, lh * pn:(lh + 1) * pn] = s
    y_ref[...] = jnp.concatenate(ys, axis=0) + de_ref[...] * xa


def _ssd_sample(p, cs_x, cs_bc, state2d, prm, dims, steps):
    d, tm, seq, b_p, b_s = dims
    t_s = b_s * steps
    rb = (b_p * seq) // t_s
    n_bc = 2 * SSM_GROUPS * SSM_STATE
    pn = SSM_HEAD_DIM * SSM_STATE
    ns = (SSM_CONV_WIDTH - 1) * b_s
    kern = functools.partial(_ssd_s_kernel, b_s=b_s, steps=steps)
    full = lambda shape: pl.BlockSpec(shape, lambda j: tuple(0 for _ in shape))
    return pl.pallas_call(
        kern,
        grid=(SSM_HEADS // 2,),
        in_specs=[pl.BlockSpec((t_s, LANES), lambda j: (rb, C_XS // LANES + j)),
                  pl.BlockSpec((t_s, n_bc), lambda j: (rb, C_BC // n_bc)),
                  pl.BlockSpec((t_s, n_bc), lambda j: (rb, C_DT // n_bc)),
                  pl.BlockSpec((ns, LANES), lambda j: (0, j)),
                  full((ns, n_bc)),
                  pl.BlockSpec((SSM_CONV_WIDTH, LANES), lambda j: (0, j)),
                  full((SSM_CONV_WIDTH, n_bc)),
                  pl.BlockSpec((1, LANES), lambda j: (0, j)),
                  full((1, n_bc)), full((1, LANES)), full((1, LANES)),
                  pl.BlockSpec((1, LANES), lambda j: (0, j)),
                  full((2, LANES, pn)), full((2, pn, LANES)),
                  pl.BlockSpec((b_s, 2 * pn), lambda j: (0, j))],
        out_specs=[pl.BlockSpec((t_s, LANES), lambda j: (0, j)),
                   pl.BlockSpec((b_s, 2 * pn), lambda j: (0, j)),
                   pl.BlockSpec((ns, LANES), lambda j: (0, j)),
                   full((ns, n_bc))],
        out_shape=[jax.ShapeDtypeStruct((t_s, d), F32),
                   jax.ShapeDtypeStruct(state2d.shape, F32),
                   jax.ShapeDtypeStruct((ns, d), F32),
                   jax.ShapeDtypeStruct((ns, n_bc), F32)],
        compiler_params=_cp("arbitrary"),
        name="ssd_sample",
    )(p, p, p, cs_x, cs_bc, prm["wx"], prm["wbc"], prm["bx"], prm["bbc"], prm["dtb"], prm["alog"],
      prm["de"], prm["e2"], prm["g2"], state2d)


def _gnorm_s_kernel(y_ref, z_ref, gn_ref, o_ref):
    u = y_ref[...] * _silu(z_ref[...])
    half = u.shape[1] // SSM_GROUPS
    outs = []
    for g in range(SSM_GROUPS):
        ug = u[:, g * half:(g + 1) * half]
        outs.append(ug * lax.rsqrt(jnp.mean(ug * ug, axis=-1, keepdims=True) + RMS_EPS))
    o_ref[...] = jnp.concatenate(outs, axis=1) * gn_ref[...]


def _gnorm_sample(y_raw, p, gn, dims, steps):
    d, tm, seq, b_p, b_s = dims
    t_s = b_s * steps
    rb = (b_p * seq) // t_s
    return pl.pallas_call(
        _gnorm_s_kernel,
        grid=(1,),
        in_specs=[pl.BlockSpec((t_s, d), lambda i: (0, 0)),
                  pl.BlockSpec((t_s, d), lambda i: (rb, C_Z // d)),
                  pl.BlockSpec((1, d), lambda i: (0, 0))],
        out_specs=pl.BlockSpec((t_s, d), lambda i: (0, 0)),
        out_shape=jax.ShapeDtypeStruct((t_s, d), F32),
        compiler_params=_cp("arbitrary"),
        name="gnorm_sample",
    )(y_raw, p, gn)


def _rope(x, cos_f, sin_a, sin_b):
    n = x.shape[1] // cos_f.shape[1]
    half = ROPE_DIMS // 2
    return (x * _tile_lanes(cos_f, n) + pltpu.roll(x, half, 1) * _tile_lanes(sin_a, n)
            + pltpu.roll(x, x.shape[1] - half, 1) * _tile_lanes(sin_b, n))


def _attn_p_kernel(sink_ref, q_ref, k_ref, v_ref, cos_ref, sa_ref, sb_ref,
                   o_ref, kn_ref, vn_ref, kprev, vprev):
    c = pl.program_id(1)
    w = q_ref.shape[0]
    cos_f, sin_a, sin_b = cos_ref[...], sa_ref[...], sb_ref[...]
    q = _rope(q_ref[...], cos_f, sin_a, sin_b)
    k = _rope(k_ref[...], cos_f, sin_a, sin_b)
    v = v_ref[...]

    @pl.when(c == 0)
    def _():
        kprev[...] = jnp.zeros(kprev.shape, F32)
        vprev[...] = jnp.zeros(vprev.shape, F32)

    kk = jnp.concatenate([kprev[...], k], axis=0)
    vv = jnp.concatenate([vprev[...], v], axis=0)
    row = _iota((w, 2 * w), 0)
    col = _iota((w, 2 * w), 1)
    valid = (col > row) & (col <= row + w) & ((c > 0) | (col >= w))
    valid = _tile_rows(valid.astype(F32), Q_PER_KV) > 0.5
    lane_k = _iota((2 * w, LANES), 1)
    lane_q = _iota((w, LANES), 1)
    scale = HEAD_DIM ** -0.5
    for kh in range(KV_HEADS):
        ks = kk[:, LANES * (kh // 2):LANES * (kh // 2 + 1)]
        vs = vv[:, LANES * (kh // 2):LANES * (kh // 2 + 1)]
        mine = _half_mask(lane_k, kh % 2 == 1)
        ks = jnp.where(mine, ks, 0.0)
        vs = jnp.where(mine, vs, 0.0)
        k2 = ks + pltpu.roll(ks, HEAD_DIM, 1)
        v2 = vs + pltpu.roll(vs, HEAD_DIM, 1)
        qs, sinks = [], []
        for g in range(Q_PER_KV):
            hq = Q_PER_KV * kh + g
            sl = q[:, LANES * (hq // 2):LANES * (hq // 2 + 1)]
            qs.append(jnp.where(_half_mask(lane_q, hq % 2 == 1), sl, 0.0))
            sinks.append(jnp.full((w, 1), sink_ref[hq], F32))
        qst = jnp.concatenate(qs, axis=0)
        sink = jnp.concatenate(sinks, axis=0)
        s = _mm_nt(qst, k2) * scale
        s = jnp.where(valid, s, NEG_BIG)
        m = jnp.maximum(jnp.max(s, axis=-1, keepdims=True), sink)
        pr = jnp.exp(s - m)
        den = jnp.sum(pr, axis=-1, keepdims=True) + jnp.exp(sink - m)
        o = _mm(pr, v2) / den
        for pr_i in range(Q_PER_KV // 2):
            left = o[(2 * pr_i) * w:(2 * pr_i + 1) * w]
            right = o[(2 * pr_i + 1) * w:(2 * pr_i + 2) * w]
            slab = (Q_PER_KV * kh) // 2 + pr_i
            o_ref[:, slab * LANES:(slab + 1) * LANES] = jnp.where(lane_q < HEAD_DIM, left, right)
    kprev[...] = k
    vprev[...] = v
    kn_ref[...] = k
    vn_ref[...] = v


def _attn_prompt(p, sinks, tabs, dims, t_all):
    d, tm, seq, b_p, b_s = dims
    w = WINDOW
    nb = seq // w
    d_kv = KV_HEADS * HEAD_DIM
    tab_spec = pl.BlockSpec((w, LANES), lambda b, c: (c, 0))
    return pl.pallas_call(
        _attn_p_kernel,
        grid=(b_p, nb),
        in_specs=[pl.BlockSpec(memory_space=pltpu.SMEM),
                  pl.BlockSpec((w, d), lambda b, c: (b * nb + c, C_Q // d)),
                  pl.BlockSpec((w, d_kv), lambda b, c: (b * nb + c, C_K // d_kv)),
                  pl.BlockSpec((w, d_kv), lambda b, c: (b * nb + c, C_V // d_kv)),
                  tab_spec, tab_spec, tab_spec],
        out_specs=[pl.BlockSpec((w, d), lambda b, c: (b * nb + c, 0)),
                   pl.BlockSpec((None, w, d_kv), lambda b, c: (b, 0, 0)),
                   pl.BlockSpec((None, w, d_kv), lambda b, c: (b, 0, 0))],
        out_shape=[jax.ShapeDtypeStruct((t_all, d), F32),
                   jax.ShapeDtypeStruct((b_p, w, d_kv), F32),
                   jax.ShapeDtypeStruct((b_p, w, d_kv), F32)],
        scratch_shapes=[pltpu.VMEM((w, d_kv), F32), pltpu.VMEM((w, d_kv), F32)],
        compiler_params=_cp("arbitrary", "arbitrary"),
        name="attn_prompt",
    )(sinks, p, p, p, *tabs)


def _rope_s_kernel(q_ref, k_ref, cos_ref, sa_ref, sb_ref, qo_ref, ko_ref):
    cos_f, sin_a, sin_b = cos_ref[...], sa_ref[...], sb_ref[...]
    qo_ref[...] = _rope(q_ref[...], cos_f, sin_a, sin_b)
    ko_ref[...] = _rope(k_ref[...], cos_f, sin_a, sin_b)


def _rope_sample(p, tabs, dims, steps):
    d, tm, seq, b_p, b_s = dims
    t_s = b_s * steps
    rb = (b_p * seq) // t_s
    d_kv = KV_HEADS * HEAD_DIM
    tab_spec = pl.BlockSpec((t_s, LANES), lambda i: (0, 0))
    return pl.pallas_call(
        _rope_s_kernel,
        grid=(1,),
        in_specs=[pl.BlockSpec((t_s, d), lambda i: (rb, C_Q // d)),
                  pl.BlockSpec((t_s, d_kv), lambda i: (rb, C_K // d_kv)),
                  tab_spec, tab_spec, tab_spec],
        out_specs=[pl.BlockSpec((t_s, d), lambda i: (0, 0)),
                   pl.BlockSpec((t_s, d_kv), lambda i: (0, 0))],
        out_shape=[jax.ShapeDtypeStruct((t_s, d), F32),
                   jax.ShapeDtypeStruct((t_s, d_kv), F32)],
        compiler_params=_cp("arbitrary"),
        name="rope_sample",
    )(p, p, *tabs)


def _attn_s_kernel(q_ref, kn_ref, vn_ref, ko_ref, vo_ref, sink_ref, valid_ref, o_ref):
    tb, nq, d_kv = q_ref.shape
    w = kn_ref.shape[1]
    n_old = ko_ref.shape[1]
    pad = jnp.zeros((w - n_old, d_kv), F32)
    valid = valid_ref[...] > 0.5
    sink = sink_ref[:, 0:1]
    scale = HEAD_DIM ** -0.5
    orow = _iota((nq, HEAD_DIM), 0)
    rows_per_kv = nq // KV_HEADS
    for b in range(tb):
        kk = jnp.concatenate([kn_ref[b], ko_ref[b], pad], axis=0)
        vv = jnp.concatenate([vn_ref[b], vo_ref[b], pad], axis=0)
        s = _mm_nt(q_ref[b], kk) * scale
        s = jnp.where(valid, s, NEG_BIG)
        m = jnp.maximum(jnp.max(s, axis=-1, keepdims=True), sink)
        pr = jnp.exp(s - m)
        den = jnp.sum(pr, axis=-1, keepdims=True) + jnp.exp(sink - m)
        o = _mm(pr, vv) / den
        acc = jnp.zeros((nq, HEAD_DIM), F32)
        for kh in range(KV_HEADS):
            mine = (orow >= kh * rows_per_kv) & (orow < (kh + 1) * rows_per_kv)
            acc = acc + jnp.where(mine, o[:, kh * HEAD_DIM:(kh + 1) * HEAD_DIM], 0.0)
        o_ref[b] = acc


def _attn_sample(qbd, k_new, v_new, k_old, v_old, sink_col, steps, tb=8):
    b_s, nq, d_kv = qbd.shape
    w = k_new.shape[1]
    t = (jnp.arange(nq) % steps)[:, None]
    col = jnp.arange(2 * w)[None, :]
    valid = jnp.where(col < w, col <= (w - steps) + t, (col - w > t) & (col - w < steps)).astype(F32)
    return pl.pallas_call(
        _attn_s_kernel,
        grid=(b_s // tb,),
        in_specs=[pl.BlockSpec((tb, nq, d_kv), lambda i: (i, 0, 0)),
                  pl.BlockSpec((tb, w, d_kv), lambda i: (i, 0, 0)),
                  pl.BlockSpec((tb, w, d_kv), lambda i: (i, 0, 0)),
                  pl.BlockSpec((tb, SUBLANES, d_kv), lambda i: (i, 0, 0)),
                  pl.BlockSpec((tb, SUBLANES, d_kv), lambda i: (i, 0, 0)),
                  pl.BlockSpec((nq, LANES), lambda i: (0, 0)),
                  pl.BlockSpec((nq, 2 * w), lambda i: (0, 0))],
        out_specs=pl.BlockSpec((tb, nq, HEAD_DIM), lambda i: (i, 0, 0)),
        out_shape=jax.ShapeDtypeStruct((b_s, nq, HEAD_DIM), F32),
        compiler_params=_cp("arbitrary"),
        name="attn_sample",
    )(qbd, k_new, v_new, k_old, v_old, sink_col, valid)


def _merge_kernel(h_ref, yap_ref, ybp_ref, ycp_ref, yas_ref, ybs_ref, ycs_ref, x_ref,
                  wg_ref, bg_ref, pa_ref, pb_ref, pc_ref, wo_ref,
                  g1p_ref, g1s_ref, nw_ref, shp_ref, scp_ref, shs_ref, scs_ref, rw_ref, rb_ref,
                  x1_ref, h2_ref, ti_ref, tg_ref, *, n_p, reps):
    i = pl.program_id(0)
    d = x_ref.shape[1]
    is_p = i < n_p
    ya = jnp.where(is_p, yap_ref[...], yas_ref[...])
    yb = jnp.where(is_p, ybp_ref[...], ybs_ref[...])
    yc = jnp.where(is_p, ycp_ref[...], ycs_ref[...])
    g = _sigmoid(jnp.dot(h_ref[...], wg_ref[...], preferred_element_type=F32) + bg_ref[...])
    merged = (g[:, 0:d] * _mm(ya, pa_ref[...]) + g[:, d:2 * d] * _mm(yb, pb_ref[...])
              + g[:, 2 * d:3 * d] * _mm(yc, pc_ref[...]))
    mix = _mm(merged, wo_ref[...])

    def finish(g1, sh, sc):
        x1 = x_ref[...] + g1 * mix
        x1_ref[...] = x1
        h2_ref[...] = _rms(x1, nw_ref[...]) * (1.0 + sc) + sh

    @pl.when(i < n_p)
    def _():
        finish(g1p_ref[...], shp_ref[...], scp_ref[...])

    @pl.when(i >= n_p)
    def _():
        finish(_tile_rows(g1s_ref[...], reps), _tile_rows(shs_ref[...], reps), _tile_rows(scs_ref[...], reps))

    h2 = h2_ref[...]
    rw = rw_ref[...]
    hs = _split(h2, 2)
    ws = _split(rw, 2)
    if len(hs) == 1:
        logits = jnp.dot(h2, rw, preferred_element_type=F32)
    else:
        logits = (jnp.dot(hs[0], ws[0], preferred_element_type=F32)
                  + jnp.dot(hs[0], ws[1], preferred_element_type=F32)
                  + jnp.dot(hs[1], ws[0], preferred_element_type=F32))
    logits = logits + rb_ref[...]
    lane = _iota(logits.shape, 1).astype(F32)
    vals, idxs = [], []
    cur = logits
    for _ in range(TOP_K):
        mx = jnp.max(cur, axis=-1, keepdims=True)
        ix = jnp.min(jnp.where(cur == mx, lane, float(LANES)), axis=-1, keepdims=True)
        vals.append(mx)
        idxs.append(ix)
        cur = jnp.where(lane == ix, -jnp.inf, cur)
    es = [jnp.exp(v - vals[0]) for v in vals]
    den = es[0]
    for e in es[1:]:
        den = den + e
    ti = jnp.zeros(logits.shape, F32)
    tg = jnp.zeros(logits.shape, F32)
    for k in range(TOP_K):
        ti = jnp.where(lane == float(k), idxs[k], ti)
        tg = jnp.where(lane == float(k), es[k] / den, tg)
    ti_ref[...] = ti.astype(I32)
    tg_ref[...] = tg


def _merge(h, ys_p, ys_s, x_all, wts, ada_p, ada_s, dims, tm_m=256):
    d, tm, seq, b_p, b_s = dims
    t_all = x_all.shape[0]
    tm_m = min(tm_m, tm)
    mdims = (d, tm_m, seq, b_p, b_s)
    n_p = b_p * seq // tm_m
    g1p, g1s = _mod_specs(mdims, 2, 1)
    shp, shs = _mod_specs(mdims, 3, 1)
    scp, scs = _mod_specs(mdims, 4, 1)
    row = lambda n: pl.BlockSpec((tm_m, n), lambda i: (i, 0))
    row_p = pl.BlockSpec((tm_m, d), lambda i: (jnp.minimum(i, n_p - 1), 0))
    row_s = pl.BlockSpec((tm_m, d), lambda i: (jnp.maximum(i - n_p, 0), 0))
    const = lambda shape: pl.BlockSpec(shape, lambda i: tuple(0 for _ in shape))
    kern = functools.partial(_merge_kernel, n_p=n_p, reps=max(tm_m // b_s, 1))
    return pl.pallas_call(
        kern,
        grid=(t_all // tm_m,),
        in_specs=[row(d), row_p, row_p, row_p, row_s, row_s, row_s, row(d),
                  const((d, 3 * d)), const((1, 3 * d)), const((d, d)), const((d, d)), const((d, d)),
                  const((d, d)),
                  g1p, g1s, const((1, d)), shp, scp, shs, scs,
                  const((d, LANES)), const((1, LANES))],
        out_specs=[row(d), row(d), row(LANES), row(LANES)],
        out_shape=[jax.ShapeDtypeStruct((t_all, d), F32),
                   jax.ShapeDtypeStruct((t_all, d), F32),
                   jax.ShapeDtypeStruct((t_all, LANES), I32),
                   jax.ShapeDtypeStruct((t_all, LANES), F32)],
        compiler_params=_cp("arbitrary"),
        name="merge",
    )(h, *ys_p, *ys_s, x_all, wts["wg"], wts["bg"], wts["pa"], wts["pb"], wts["pc"], wts["wo"],
      ada_p, ada_s, wts["nffn"], ada_p, ada_p, ada_s, ada_s, wts["rw"], wts["rb"])


def _rank_kernel(ti_ref, dest_ref, cnt_ref, run, tot):
    ph = pl.program_id(0)
    i = pl.program_id(1)
    tm = ti_ref.shape[0]
    ti = ti_ref[...]
    lane = _iota((tm, LANES), 1)
    oh = jnp.zeros((tm, LANES), F32)
    for k in range(TOP_K):
        oh = oh + jnp.where(lane == ti[:, k:k + 1], 1.0, 0.0)
    ones = jnp.ones((SUBLANES, tm), MXU_DTYPE)
    colsum = jnp.dot(ones, oh.astype(MXU_DTYPE), preferred_element_type=F32)[0:1, :]

    @pl.when((ph == 0) & (i == 0))
    def _():
        tot[...] = jnp.zeros(tot.shape, F32)

    @pl.when(ph == 0)
    def _():
        tot[...] = tot[...] + colsum

    @pl.when((ph == 1) & (i == 0))
    def _():
        run[...] = jnp.zeros(run.shape, F32)

    @pl.when(ph == 1)
    def _():
        r = _iota((LANES, LANES), 0)
        c = _iota((LANES, LANES), 1)
        upper = jnp.where(r < c, 1.0, 0.0).astype(MXU_DTYPE)
        starts = _mm_sel(jnp.broadcast_to(tot[...], (SUBLANES, LANES)), upper)[0:1, :]
        rr = _iota((tm, tm), 0)
        cc = _iota((tm, tm), 1)
        lower = jnp.where(rr > cc, 1.0, 0.0).astype(MXU_DTYPE)
        pre = jnp.dot(lower, oh.astype(MXU_DTYPE), preferred_element_type=F32)
        pos = pre + run[...] + starts
        dest = jnp.zeros((tm, LANES), F32)
        for k in range(TOP_K):
            dk = jnp.sum(jnp.where(lane == ti[:, k:k + 1], pos, 0.0), axis=1, keepdims=True)
            dest = jnp.where(lane == k, dk, dest)
        dest_ref[...] = dest.astype(I32)
        run[...] = run[...] + colsum

    cnt_ref[...] = jnp.broadcast_to(tot[...], cnt_ref.shape)


def _rank(topi, tm):
    t_all = topi.shape[0]
    return pl.pallas_call(
        _rank_kernel,
        grid=(2, t_all // tm),
        in_specs=[pl.BlockSpec((tm, LANES), lambda ph, i: (i, 0))],
        out_specs=[pl.BlockSpec((tm, LANES), lambda ph, i: (i * ph, 0)),
                   pl.BlockSpec((SUBLANES, LANES), lambda ph, i: (0, 0))],
        out_shape=[jax.ShapeDtypeStruct((t_all, LANES), I32),
                   jax.ShapeDtypeStruct((SUBLANES, LANES), F32)],
        scratch_shapes=[pltpu.VMEM((1, LANES), F32), pltpu.VMEM((1, LANES), F32)],
        compiler_params=_cp("arbitrary", "arbitrary"),
        name="rank",
    )(topi)


def _dispatch_kernel(dest_ref, h2_ref, xs_ref, sem):
    tm = h2_ref.shape[0]

    def issue(r, carry):
        for k in range(TOP_K):
            pltpu.make_async_copy(h2_ref.at[pl.ds(r, 1)], xs_ref.at[pl.ds(dest_ref[r * TOP_K + k], 1)],
                                  sem).start()
        return carry

    lax.fori_loop(0, tm, issue, 0)
    for _ in range(TOP_K):
        pltpu.make_async_copy(h2_ref, xs_ref.at[pl.ds(0, tm)], sem).wait()


def _dispatch(dest_flat, h2, tm=256):
    t_all, d = h2.shape
    return pl.pallas_call(
        _dispatch_kernel,
        grid=(t_all // tm,),
        in_specs=[pl.BlockSpec((tm * TOP_K,), lambda i: (i,), memory_space=pltpu.SMEM),
                  pl.BlockSpec((tm, d), lambda i: (i, 0))],
        out_specs=pl.BlockSpec(memory_space=pl.ANY),
        out_shape=jax.ShapeDtypeStruct((t_all * TOP_K, d), F32),
        scratch_shapes=[pltpu.SemaphoreType.DMA(())],
        compiler_params=_cp("arbitrary"),
        name="dispatch",
    )(dest_flat, h2)


def _expert_kernel(tile_ref, exp_ref, lo_ref, hi_ref, firste_ref, firstt_ref, valid_ref,
                   x_ref, wgu_ref, bgu_ref, wd_ref, bd_ref, o_ref, wgu_s, wd_s):
    w = pl.program_id(0)
    tmx = x_ref.shape[0]
    f = wd_ref.shape[0]

    @pl.when(firste_ref[w] == 1)
    def _():
        wgu_s[...] = wgu_ref[...].astype(wgu_s.dtype)
        wd_s[...] = wd_ref[...].astype(wd_s.dtype)

    @pl.when(valid_ref[w] == 1)
    def _():
        gu = jnp.dot(x_ref[...].astype(MXU_DTYPE), wgu_s[...], preferred_element_type=F32) + bgu_ref[...]
        g = jnp.minimum(gu[:, 0:f], SWIGLU_LIMIT)
        u = jnp.clip(gu[:, f:2 * f], -SWIGLU_LIMIT, SWIGLU_LIMIT)
        act = g * _sigmoid(SWIGLU_ALPHA * g) * (u + 1.0)
        res = jnp.dot(act.astype(MXU_DTYPE), wd_s[...], preferred_element_type=F32) + bd_ref[...]
        rows = tile_ref[w] * tmx + _iota((tmx, 1), 0)
        mine = (rows >= lo_ref[w]) & (rows < hi_ref[w])

        @pl.when(firstt_ref[w] == 1)
        def _():
            o_ref[...] = jnp.where(mine, res, 0.0)

        @pl.when(firstt_ref[w] == 0)
        def _():
            o_ref[...] = jnp.where(mine, res, o_ref[...])


def _experts(meta, xs, w_gu, b_gu, w_d, b_d, tmx):
    tk, d = xs.shape
    n_e, _, f2 = w_gu.shape
    f = f2 // 2
    n_w = meta[0].shape[0]
    grid_spec = pltpu.PrefetchScalarGridSpec(
        num_scalar_prefetch=7,
        grid=(n_w,),
        in_specs=[pl.BlockSpec((tmx, d), lambda w, tl, ex, *_: (tl[w], 0)),
                  pl.BlockSpec((None, d, f2), lambda w, tl, ex, *_: (ex[w], 0, 0)),
                  pl.BlockSpec((None, 1, f2), lambda w, tl, ex, *_: (ex[w], 0, 0)),
                  pl.BlockSpec((None, f, d), lambda w, tl, ex, *_: (ex[w], 0, 0)),
                  pl.BlockSpec((None, 1, d), lambda w, tl, ex, *_: (ex[w], 0, 0))],
        out_specs=pl.BlockSpec((tmx, d), lambda w, tl, ex, *_: (tl[w], 0)),
        scratch_shapes=[pltpu.VMEM((d, f2), MXU_DTYPE), pltpu.VMEM((f, d), MXU_DTYPE)],
    )
    return pl.pallas_call(
        _expert_kernel,
        grid_spec=grid_spec,
        out_shape=jax.ShapeDtypeStruct((tk, d), F32),
        compiler_params=_cp("arbitrary"),
        name="experts",
    )(*meta, xs, w_gu, b_gu.reshape(n_e, 1, f2), w_d, b_d.reshape(n_e, 1, d))


def _expert_schedule(counts, tk, tmx):
    n_e = counts.shape[0]
    n_tiles = tk // tmx
    n_w = n_tiles + n_e - 1
    ends = jnp.cumsum(counts)
    starts = ends - counts
    first_tile = starts // tmx
    last_tile = jnp.maximum(ends - 1, 0) // tmx
    n_t = jnp.where(counts > 0, last_tile - first_tile + 1, 0)
    cum = jnp.cumsum(n_t)
    total = cum[-1]
    w = jnp.arange(n_w, dtype=I32)
    wc = jnp.minimum(w, total - 1)
    ex = jnp.searchsorted(cum, wc, side="right").astype(I32)
    tile = (first_tile[ex] + wc - (cum[ex] - n_t[ex])).astype(I32)
    valid = (w < total).astype(I32)
    prev_ex = jnp.concatenate([jnp.full((1,), -1, I32), ex[:-1]])
    prev_tile = jnp.concatenate([jnp.full((1,), -1, I32), tile[:-1]])
    first_e = ((ex != prev_ex) & (valid == 1)).astype(I32)
    first_t = ((tile != prev_tile) & (valid == 1)).astype(I32)
    return (tile, ex, starts[ex].astype(I32), ends[ex].astype(I32), first_e, first_t, valid)


def _combine_kernel(dest_ref, eo_ref, x1_ref, tg_ref, g2p_ref, g2s_ref, nf_ref, o_ref, buf, sem, *,
                    n_p, reps, final):
    i = pl.program_id(0)
    tm = x1_ref.shape[0]

    def issue(r, carry):
        for k in range(TOP_K):
            pltpu.make_async_copy(eo_ref.at[pl.ds(dest_ref[r * TOP_K + k], 1)], buf.at[k, pl.ds(r, 1)],
                                  sem).start()
        return carry

    lax.fori_loop(0, tm, issue, 0)
    for k in range(TOP_K):
        pltpu.make_async_copy(eo_ref.at[pl.ds(0, tm)], buf.at[k], sem).wait()
    tg = tg_ref[...]
    y = tg[:, 0:1] * buf[0]
    for k in range(1, TOP_K):
        y = y + tg[:, k:k + 1] * buf[k]

    def finish(g2):
        x2 = x1_ref[...] + g2 * y
        o_ref[...] = _rms(x2, nf_ref[...]) if final else x2

    @pl.when(i < n_p)
    def _():
        finish(g2p_ref[...])

    @pl.when(i >= n_p)
    def _():
        finish(_tile_rows(g2s_ref[...], reps))


def _combine(dest_flat, eo, x1, topg, ada_p, ada_s, norm_final, dims, final, tm_c=256):
    d, tm, seq, b_p, b_s = dims
    t_all = x1.shape[0]
    tm_c = min(tm_c, tm)
    cdims = (d, tm_c, seq, b_p, b_s)
    n_p = b_p * seq // tm_c
    g2p, g2s = _mod_specs(cdims, 5, 1)
    kern = functools.partial(_combine_kernel, n_p=n_p, reps=max(tm_c // b_s, 1), final=final)
    return pl.pallas_call(
        kern,
        grid=(t_all // tm_c,),
        in_specs=[pl.BlockSpec((tm_c * TOP_K,), lambda i: (i,), memory_space=pltpu.SMEM),
                  pl.BlockSpec(memory_space=pl.ANY),
                  pl.BlockSpec((tm_c, d), lambda i: (i, 0)),
                  pl.BlockSpec((tm_c, LANES), lambda i: (i, 0)),
                  g2p, g2s,
                  pl.BlockSpec((1, d), lambda i: (0, 0))],
        out_specs=pl.BlockSpec((tm_c, d), lambda i: (i, 0)),
        out_shape=jax.ShapeDtypeStruct((t_all, d), F32),
        scratch_shapes=[pltpu.VMEM((TOP_K, tm_c, d), F32), pltpu.SemaphoreType.DMA(())],
        compiler_params=_cp("arbitrary"),
        name="combine",
    )(dest_flat, eo, x1, topg, ada_p, ada_s, norm_final.reshape(1, d))


def _rope_tables(pos):
    half = ROPE_DIMS // 2
    inv_freq = jnp.exp(-math.log(ROPE_THETA) * jnp.arange(half, dtype=F32) / half)
    ang = pos.astype(F32)[:, None] * inv_freq[None, :]
    cos, sin = jnp.cos(ang), jnp.sin(ang)
    n = pos.shape[0]
    rest = HEAD_DIM - ROPE_DIMS
    cos_h = jnp.concatenate([cos, cos, jnp.ones((n, rest), F32)], axis=1)
    sa_h = jnp.concatenate([jnp.zeros((n, half), F32), sin, jnp.zeros((n, rest), F32)], axis=1)
    sb_h = jnp.concatenate([-sin, jnp.zeros((n, half + rest), F32)], axis=1)
    rep = LANES // HEAD_DIM
    return tuple(jnp.tile(t, (1, rep)) for t in (cos_h, sa_h, sb_h))


def _pad_lanes(v, n, value=0.0):
    return jnp.pad(v, ((0, 0), (0, n - v.shape[1])), constant_values=value)


def kernel(x_prompt, x_sample, state_conv_a, state_conv_ssm, state_ssm, cache_k, cache_v, c_prompt, c_sample, w_ada, b_ada, norm_mix, norm_ffn, w_in, conv_a_w, ssm_conv_w, ssm_conv_b, ssm_dt_bias, ssm_a_log, ssm_d, ssm_norm, attn_sinks, w_branch_gate, b_branch_gate, w_proj_a, w_proj_b, w_proj_c, w_out, router_w, router_b, w_gate_up, b_gate_up, w_down, b_down, norm_final):
    b_p, seq, d = x_prompt.shape
    b_s, steps, _ = x_sample.shape
    depth = w_ada.shape[0]
    t_p, t_s = b_p * seq, b_s * steps
    t_all = t_p + t_s
    tm = t_s
    dims = (d, tm, seq, b_p, b_s)
    n_e = router_w.shape[2]
    d_ssm = SSM_HEADS * SSM_HEAD_DIM
    n_bc = 2 * SSM_GROUPS * SSM_STATE
    d_kv = KV_HEADS * HEAD_DIM
    pn = SSM_HEAD_DIM * SSM_STATE
    assert d == d_ssm == ATTN_HEADS * HEAD_DIM and t_p % t_s == 0 and seq % t_s == 0
    assert t_s % 256 == 0 or t_s <= 256
    assert steps <= SUBLANES and PAST_LEN >= WINDOW and seq >= WINDOW and b_s % SUBLANES == 0

    x_all = jnp.concatenate([x_prompt.reshape(t_p, d), x_sample.transpose(1, 0, 2).reshape(t_s, d)], axis=0)
    ada = _ada_all(jnp.concatenate([c_prompt, c_sample], axis=0), w_ada, b_ada)

    o_z = 3 * d
    o_bc = o_z + 2 * d_ssm
    o_dt = o_bc + n_bc
    o_q = o_dt + SSM_HEADS
    o_k = o_q + d
    w_pack = jnp.concatenate(
        [w_in[:, :, :o_bc], w_in[:, :, o_q:o_k], w_in[:, :, o_bc:o_dt], w_in[:, :, o_k:],
         w_in[:, :, o_dt:o_q], jnp.zeros((depth, d, N_COL - C_DT - SSM_HEADS), F32)], axis=2).astype(MXU_DTYPE)

    hp = jnp.arange(d_ssm) // SSM_HEAD_DIM
    rexp = (jnp.arange(LANES)[:, None] == hp[None, :]).astype(MXU_DTYPE)
    lane_pn = jnp.arange(pn)
    e2 = jnp.stack([(jnp.arange(LANES)[:, None] == (lane_pn // SSM_STATE + lh * SSM_HEAD_DIM)[None, :])
                    for lh in range(2)]).astype(MXU_DTYPE)
    g2 = jnp.transpose(e2, (0, 2, 1))

    tabs_p = _rope_tables(jnp.arange(seq, dtype=I32))
    tabs_s = _rope_tables(jnp.repeat(PAST_LEN + jnp.arange(steps, dtype=I32), b_s))
    eye_kv = jnp.eye(KV_HEADS, dtype=F32)

    outs = [[] for _ in range(10)]
    y_final = None
    for l in range(depth):
        ada_p = ada[l, :b_p].reshape(b_p, 1, 6 * d)
        ada_s = ada[l, b_p:]
        ssm_prm = dict(
            wx=ssm_conv_w[l, :, :d_ssm], wbc=ssm_conv_w[l, :, d_ssm:],
            bx=ssm_conv_b[l, :d_ssm].reshape(1, d_ssm), bbc=ssm_conv_b[l, d_ssm:].reshape(1, n_bc),
            dtb=_pad_lanes(ssm_dt_bias[l].reshape(1, SSM_HEADS), LANES),
            alog=_pad_lanes(ssm_a_log[l].reshape(1, SSM_HEADS), LANES),
            de=jnp.repeat(ssm_d[l], SSM_HEAD_DIM).reshape(1, d_ssm),
            gn=ssm_norm[l].reshape(1, d_ssm), rexp=rexp, e2=e2, g2=g2)
        wts = dict(
            wg=w_branch_gate[l].astype(MXU_DTYPE), bg=b_branch_gate[l].reshape(1, 3 * d),
            pa=w_proj_a[l].astype(MXU_DTYPE), pb=w_proj_b[l].astype(MXU_DTYPE),
            pc=w_proj_c[l].astype(MXU_DTYPE), wo=w_out[l].astype(MXU_DTYPE),
            nffn=norm_ffn[l].reshape(1, d),
            rw=_pad_lanes(router_w[l], LANES), rb=_pad_lanes(router_b[l].reshape(1, n_e), LANES, NEG_BIG))

        p, h = _inproj(x_all, norm_mix[l], ada_p, ada_s, w_pack[l], dims)

        ya_p, ca_p = _conva_prompt(p, conv_a_w[l], dims, t_p)
        ca_tm = state_conv_a[l].transpose(1, 0, 2).reshape((CONV_A_WIDTH - 1) * b_s, d)
        ya_s, ca_s = _conva_sample(p, ca_tm, conv_a_w[l], dims, steps)
        outs[0].append(ca_p[:, SUBLANES - (CONV_A_WIDTH - 1):])
        outs[1].append(ca_s.reshape(CONV_A_WIDTH - 1, b_s, d).transpose(1, 0, 2))

        yb_p, csx_p, csbc_p, st_p = _ssd_prompt(p, ssm_prm, dims, t_p)
        nsc = SSM_CONV_WIDTH - 1
        outs[2].append(jnp.concatenate([csx_p[:, SUBLANES - nsc:], csbc_p[:, SUBLANES - nsc:]], axis=2))
        hg = SSM_HEADS // SSM_GROUPS
        st_g = jnp.stack([st_p[:, g * SSM_STATE:(g + 1) * SSM_STATE, g * (d_ssm // 2):(g + 1) * (d_ssm // 2)]
                          for g in range(SSM_GROUPS)], axis=1)
        outs[4].append(st_g.reshape(b_p, SSM_GROUPS, SSM_STATE, hg, SSM_HEAD_DIM)
                       .transpose(0, 1, 3, 4, 2).reshape(b_p, SSM_HEADS, SSM_HEAD_DIM, SSM_STATE))
        cs_tm = state_conv_ssm[l].transpose(1, 0, 2).reshape(nsc * b_s, d_ssm + n_bc)
        y_raw, st_s, csx_s, csbc_s = _ssd_sample(p, cs_tm[:, :d_ssm], cs_tm[:, d_ssm:],
                                                 state_ssm[l].reshape(b_s, SSM_HEADS * pn), ssm_prm, dims, steps)
        yb_s = _gnorm_sample(y_raw, p, ssm_prm["gn"], dims, steps)
        outs[3].append(jnp.concatenate([csx_s, csbc_s], axis=1).reshape(nsc, b_s, d_ssm + n_bc).transpose(1, 0, 2))
        outs[5].append(st_s.reshape(b_s, SSM_HEADS, SSM_HEAD_DIM, SSM_STATE))

        yc_p, k_p, v_p = _attn_prompt(p, attn_sinks[l], tabs_p, dims, t_p)
        outs[6].append(k_p.reshape(b_p, WINDOW, KV_HEADS, HEAD_DIM))
        outs[8].append(v_p.reshape(b_p, WINDOW, KV_HEADS, HEAD_DIM))
        q_s, k_s = _rope_sample(p, tabs_s, dims, steps)
        v_s = p[t_p:, C_V:C_V + d_kv]
        k_new = jnp.concatenate([cache_k[l][:, steps:].reshape(b_s, WINDOW - steps, d_kv),
                                 k_s.reshape(steps, b_s, d_kv).transpose(1, 0, 2)], axis=1)
        v_new = jnp.concatenate([cache_v[l][:, steps:].reshape(b_s, WINDOW - steps, d_kv),
                                 v_s.reshape(steps, b_s, d_kv).transpose(1, 0, 2)], axis=1)
        outs[7].append(k_new.reshape(b_s, WINDOW, KV_HEADS, HEAD_DIM))
        outs[9].append(v_new.reshape(b_s, WINDOW, KV_HEADS, HEAD_DIM))
        q4 = q_s.reshape(steps, b_s, KV_HEADS, Q_PER_KV, HEAD_DIM).transpose(1, 2, 3, 0, 4)
        qbd = (q4.reshape(b_s, KV_HEADS, Q_PER_KV * steps, 1, HEAD_DIM)
               * eye_kv[None, :, None, :, None]).reshape(b_s, KV_HEADS * Q_PER_KV * steps, d_kv)
        sink_col = jnp.broadcast_to(jnp.repeat(attn_sinks[l], steps)[:, None], (ATTN_HEADS * steps, LANES))
        o_s = _attn_sample(qbd, k_new, v_new, cache_k[l].reshape(b_s, WINDOW, d_kv),
                           cache_v[l].reshape(b_s, WINDOW, d_kv), sink_col, steps)
        yc_s = o_s.reshape(b_s, KV_HEADS, Q_PER_KV, steps, HEAD_DIM).transpose(3, 0, 1, 2, 4).reshape(t_s, d)

        x1, h2, topi, topg = _merge(h, (ya_p, yb_p, yc_p), (ya_s, yb_s, yc_s), x_all, wts, ada_p, ada_s, dims)
        dest, cnt = _rank(topi, tm)
        dest_flat = dest[:, :TOP_K].reshape(t_all * TOP_K)
        xs = _dispatch(dest_flat, h2, min(256, tm))
        tmx = min(256, tm)
        meta = _expert_schedule(cnt[0, :n_e].astype(I32), t_all * TOP_K, tmx)
        eo = _experts(meta, xs, w_gate_up[l], b_gate_up[l], w_down[l], b_down[l], tmx)
        x_all = _combine(dest_flat, eo, x1, topg, ada_p, ada_s, norm_final, dims, final=(l == depth - 1))

    y_prompt = x_all[:t_p].reshape(b_p, seq, d)
    y_sample = x_all[t_p:].reshape(steps, b_s, d).transpose(1, 0, 2)
    st = [jnp.stack(o) for o in outs]
    return (y_prompt, y_sample, st[0], st[1], st[2], st[3], st[4], st[5], st[6], st[7], st[8], st[9])
```

```python
import functools
import math

import jax
import jax.numpy as jnp
from jax import lax
from jax.experimental import pallas as pl
from jax.experimental.pallas import tpu as pltpu

F32 = jnp.float32
BF16 = jnp.bfloat16
I32 = jnp.int32

PAST_LEN = 8192
SSM_HEADS = 16
SSM_HEAD_DIM = 64
SSM_GROUPS = 2
SSM_STATE = 64
SSM_CONV_WIDTH = 4
CONV_A_WIDTH = 3
SSD_CHUNK = 128
ATTN_HEADS = 16
KV_HEADS = 4
HEAD_DIM = 64
Q_PER_KV = ATTN_HEADS // KV_HEADS
WINDOW = 128
ROPE_THETA = 500000.0
ROPE_DIMS = HEAD_DIM // 4
N_EXPERTS = 32
TOP_K = 4
SWIGLU_LIMIT = 7.0
SWIGLU_ALPHA = 1.702
RMS_EPS = 1e-5
NEG_BIG = -1e30

LANES = 128
SUBLANES = 8
VMEM_LIMIT = 56 * 1024 * 1024

MXU_DTYPE = BF16
C_B, C_C, C_X, C_Z, C_XS, C_Q, N_MAIN = 0, 1024, 2048, 3072, 4096, 5120, 6144
T_BC, T_K, T_V, T_DT, N_TAIL = 0, 256, 512, 768, 1024
N_COL = N_MAIN + N_TAIL
HEAD_ORDER = tuple(h for j in range(ATTN_HEADS // 2)
                   for h in ((j // Q_PER_KV) * 2 * Q_PER_KV + j % Q_PER_KV,
                             (j // Q_PER_KV) * 2 * Q_PER_KV + Q_PER_KV + j % Q_PER_KV))


def _cp(*sem):
    return pltpu.CompilerParams(dimension_semantics=sem, vmem_limit_bytes=VMEM_LIMIT)


def _mm(a, b):
    return jnp.dot(a.astype(MXU_DTYPE), b.astype(MXU_DTYPE), preferred_element_type=F32)


def _mm_nt(a, b):
    return lax.dot_general(a.astype(MXU_DTYPE), b.astype(MXU_DTYPE), (((1,), (1,)), ((), ())),
                           preferred_element_type=F32)


def _split(v, n):
    if MXU_DTYPE == F32:
        return [v]
    parts, r = [], v
    for _ in range(n):
        p = r.astype(MXU_DTYPE)
        parts.append(p)
        r = r - p.astype(F32)
    return parts


def _mm_sel(v, sel, n=3):
    acc = None
    for p in _split(v, n):
        t = jnp.dot(p, sel, preferred_element_type=F32)
        acc = t if acc is None else acc + t
    return acc


def _sel_mm(sel, v, n=3):
    acc = None
    for p in _split(v, n):
        t = jnp.dot(sel, p, preferred_element_type=F32)
        acc = t if acc is None else acc + t
    return acc


def _rms(x, w):
    return x * lax.rsqrt(jnp.mean(x * x, axis=-1, keepdims=True) + RMS_EPS) * w


def _silu(x):
    return x * (1.0 / (1.0 + jnp.exp(-x)))


def _sigmoid(x):
    return 1.0 / (1.0 + jnp.exp(-x))


def _softplus(x):
    return jnp.maximum(x, 0.0) + jnp.log(1.0 + jnp.exp(-jnp.abs(x)))


def _tile_rows(x, n):
    return x if n == 1 else jnp.concatenate([x] * n, axis=0)


def _tile_lanes(x, n):
    return x if n == 1 else jnp.concatenate([x] * n, axis=1)


def _iota(shape, dim):
    return lax.broadcasted_iota(I32, shape, dim)


def _half_mask(lane, upper):
    return lane >= HEAD_DIM if upper else lane < HEAD_DIM


def _ada_kernel(c_ref, w_ref, b_ref, o_ref):
    a = _silu(c_ref[...])
    o_ref[...] = _mm(a, w_ref[...]) + b_ref[...]


def _ada_all(c_all, w_ada, b_ada):
    depth, d, n6 = w_ada.shape
    nb = c_all.shape[0]
    tn = d
    return pl.pallas_call(
        _ada_kernel,
        grid=(depth, n6 // tn),
        in_specs=[pl.BlockSpec((nb, d), lambda l, j: (0, 0)),
                  pl.BlockSpec((None, d, tn), lambda l, j: (l, 0, j)),
                  pl.BlockSpec((None, 1, tn), lambda l, j: (l, 0, j))],
        out_specs=pl.BlockSpec((None, nb, tn), lambda l, j: (l, 0, j)),
        out_shape=jax.ShapeDtypeStruct((depth, nb, n6), F32),
        compiler_params=_cp("arbitrary", "arbitrary"),
        name="ada",
    )(c_all, w_ada, b_ada.reshape(depth, 1, n6))


def _inproj_kernel(x_ref, nw_ref, shp_ref, scp_ref, shs_ref, scs_ref, w_ref, pm_ref, pt_ref, h_ref, *,
                   n_p, reps, n_main):
    i = pl.program_id(0)
    j = pl.program_id(1)

    def make_h(sh, sc):
        y = _rms(x_ref[...], nw_ref[...])
        h_ref[...] = (y * (1.0 + sc) + sh).astype(h_ref.dtype)

    @pl.when((j == 0) & (i < n_p))
    def _():
        make_h(shp_ref[...], scp_ref[...])

    @pl.when((j == 0) & (i >= n_p))
    def _():
        make_h(_tile_rows(shs_ref[...], reps), _tile_rows(scs_ref[...], reps))

    res = jnp.dot(h_ref[...], w_ref[...], preferred_element_type=F32)

    @pl.when(j < n_main)
    def _():
        pm_ref[...] = res.astype(pm_ref.dtype)

    @pl.when(j == n_main)
    def _():
        pt_ref[...] = res


def _mod_specs(dims, col, grid_rank):
    d, tm, seq, b_p, b_s = dims

    def pidx(i, *_):
        return (jnp.minimum(i * tm // seq, b_p - 1), 0, col)

    def sidx(i, *_):
        return (0, col)

    return pl.BlockSpec((None, 1, d), pidx), pl.BlockSpec((b_s, d), sidx)


def _inproj(x_all, norm_w, ada_p, ada_s, w_pack, layer, dims):
    d, tm, seq, b_p, b_s = dims
    tn = N_TAIL
    t_all = x_all.shape[0]
    n_t = t_all // tm
    n_p = b_p * seq // tm
    n_main = N_MAIN // tn
    shp, shs = _mod_specs(dims, 0, 2)
    scp, scs = _mod_specs(dims, 1, 2)
    kern = functools.partial(_inproj_kernel, n_p=n_p, reps=tm // b_s, n_main=n_main)
    return pl.pallas_call(
        kern,
        grid=(n_t, n_main + 1),
        in_specs=[pl.BlockSpec((tm, d), lambda i, j: (i, 0)),
                  pl.BlockSpec((1, d), lambda i, j: (0, 0)),
                  shp, scp, shs, scs,
                  pl.BlockSpec((None, d, tn), lambda i, j: (layer, 0, j))],
        out_specs=[pl.BlockSpec((tm, tn), lambda i, j: (i, jnp.minimum(j, n_main - 1))),
                   pl.BlockSpec((tm, tn), lambda i, j: (i, 0)),
                   pl.BlockSpec((tm, d), lambda i, j: (i, 0))],
        out_shape=[jax.ShapeDtypeStruct((t_all, N_MAIN), MXU_DTYPE),
                   jax.ShapeDtypeStruct((t_all, N_TAIL), F32),
                   jax.ShapeDtypeStruct((t_all, d), MXU_DTYPE)],
        compiler_params=_cp("arbitrary", "arbitrary"),
        name="inproj",
    )(x_all, norm_w.reshape(1, d), ada_p, ada_p, ada_s, ada_s, w_pack)


def _conva_p_kernel(b_ref, c_ref, x_ref, w_ref, y_ref, st_ref, buf):
    seq = b_ref.shape[0]
    u = c_ref[...].astype(F32) * x_ref[...].astype(F32)
    buf[0:SUBLANES, :] = jnp.zeros((SUBLANES, buf.shape[1]), F32)
    buf[SUBLANES:, :] = u
    w = w_ref[...]
    acc = w[2:3, :] * u
    acc = acc + w[1:2, :] * buf[pl.ds(SUBLANES - 1, seq), :]
    acc = acc + w[0:1, :] * buf[pl.ds(SUBLANES - 2, seq), :]
    y_ref[...] = (b_ref[...].astype(F32) * acc).astype(y_ref.dtype)
    st_ref[...] = buf[pl.ds(seq, SUBLANES), :]


def _conva_prompt(p, conv_w, dims, t_all, tc=256):
    d, tm, seq, b_p, b_s = dims
    nc = d // tc
    return pl.pallas_call(
        _conva_p_kernel,
        grid=(b_p, nc),
        in_specs=[pl.BlockSpec((seq, tc), lambda b, c: (b, C_B // tc + c)),
                  pl.BlockSpec((seq, tc), lambda b, c: (b, C_C // tc + c)),
                  pl.BlockSpec((seq, tc), lambda b, c: (b, C_X // tc + c)),
                  pl.BlockSpec((CONV_A_WIDTH, tc), lambda b, c: (0, c))],
        out_specs=[pl.BlockSpec((seq, tc), lambda b, c: (b, c)),
                   pl.BlockSpec((None, SUBLANES, tc), lambda b, c: (b, 0, c))],
        out_shape=[jax.ShapeDtypeStruct((t_all, d), MXU_DTYPE),
                   jax.ShapeDtypeStruct((b_p, SUBLANES, d), F32)],
        scratch_shapes=[pltpu.VMEM((seq + SUBLANES, tc), F32)],
        compiler_params=_cp("arbitrary", "arbitrary"),
        name="conva_prompt",
    )(p, p, p, conv_w)


def _conva_s_kernel(b_ref, c_ref, x_ref, s_ref, w_ref, y_ref, st_ref, *, b_s, steps):
    ext = jnp.concatenate([s_ref[...], c_ref[...].astype(F32) * x_ref[...].astype(F32)], axis=0)
    w = w_ref[...]
    acc = None
    for k in range(CONV_A_WIDTH):
        t = w[k:k + 1, :] * ext[k * b_s:(k + steps) * b_s]
        acc = t if acc is None else acc + t
    y_ref[...] = (b_ref[...].astype(F32) * acc).astype(y_ref.dtype)
    st_ref[...] = ext[steps * b_s:]


def _conva_sample(p, state_tm, conv_w, dims, steps, tc=256):
    d, tm, seq, b_p, b_s = dims
    t_s = b_s * steps
    rb = (b_p * seq) // t_s
    ns = (CONV_A_WIDTH - 1) * b_s
    kern = functools.partial(_conva_s_kernel, b_s=b_s, steps=steps)
    return pl.pallas_call(
        kern,
        grid=(d // tc,),
        in_specs=[pl.BlockSpec((t_s, tc), lambda c: (rb, C_B // tc + c)),
                  pl.BlockSpec((t_s, tc), lambda c: (rb, C_C // tc + c)),
                  pl.BlockSpec((t_s, tc), lambda c: (rb, C_X // tc + c)),
                  pl.BlockSpec((ns, tc), lambda c: (0, c)),
                  pl.BlockSpec((CONV_A_WIDTH, tc), lambda c: (0, c))],
        out_specs=[pl.BlockSpec((t_s, tc), lambda c: (0, c)),
                   pl.BlockSpec((ns, tc), lambda c: (0, c))],
        out_shape=[jax.ShapeDtypeStruct((t_s, d), MXU_DTYPE),
                   jax.ShapeDtypeStruct((ns, d), F32)],
        compiler_params=_cp("arbitrary"),
        name="conva_sample",
    )(p, p, p, state_tm, conv_w)


def _ssd_p_kernel(z_ref, xs_ref, bc_ref, dt_ref, wx_ref, wbc_ref, bx_ref, bbc_ref, dtb_ref, alog_ref,
                  de_ref, gn_ref, rexp_ref,
                  y_ref, csx_ref, csbc_ref, stout_ref,
                  xbuf, bcbuf, st, ybuf):
    c = pl.program_id(1)
    q = xs_ref.shape[0]
    d_ssm = xs_ref.shape[1]
    half = d_ssm // SSM_GROUPS

    @pl.when(c == 0)
    def _():
        xbuf[0:SUBLANES, :] = jnp.zeros((SUBLANES, xbuf.shape[1]), F32)
        bcbuf[0:SUBLANES, :] = jnp.zeros((SUBLANES, bcbuf.shape[1]), F32)
        st[...] = jnp.zeros(st.shape, F32)

    xbuf[SUBLANES:, :] = xs_ref[...].astype(F32)
    bcbuf[SUBLANES:, :] = bc_ref[...]

    def conv(buf, w, b):
        acc = b
        for k in range(SSM_CONV_WIDTH):
            acc = acc + w[k:k + 1, :] * buf[pl.ds(SUBLANES - (SSM_CONV_WIDTH - 1) + k, q), :]
        return _silu(acc)

    xa = conv(xbuf, wx_ref[...], bx_ref[...])
    bca = conv(bcbuf, wbc_ref[...], bbc_ref[...])
    csx_ref[...] = xbuf[pl.ds(q, SUBLANES), :]
    csbc_ref[...] = bcbuf[pl.ds(q, SUBLANES), :]
    xbuf[0:SUBLANES, :] = xbuf[pl.ds(q, SUBLANES), :]
    bcbuf[0:SUBLANES, :] = bcbuf[pl.ds(q, SUBLANES), :]

    dtv = _softplus(dt_ref[:, 0:LANES] + dtb_ref[...])
    a = -jnp.exp(alog_ref[...])
    da = dtv * a
    row = _iota((q, q), 0)
    col = _iota((q, q), 1)
    causal = row >= col
    tri = jnp.where(causal, 1.0, 0.0).astype(MXU_DTYPE)
    acs = _sel_mm(tri, da)
    acs_t = acs.T
    rexp = rexp_ref[...]
    acs_e = _mm_sel(acs, rexp)
    dt_e = _mm_sel(dtv, rexp)
    last_e = acs_e[q - 1:q, :]
    exp_e = jnp.exp(acs_e)
    w_e = jnp.exp(last_e - acs_e)
    dec_last = jnp.exp(last_e)

    xdt = xa * dt_e
    xdt_w = xdt * w_e
    bslab = bca[:, 0:LANES]
    cslab = bca[:, LANES:2 * LANES]
    bt = bslab.T
    st_old = st[...]
    y_off = _mm(cslab, st_old) * exp_e
    upd = _mm(bt, xdt_w)
    srow = _iota(st.shape, 0)
    scol = _iota(st.shape, 1)
    diag = (srow >= SSM_STATE) == (scol >= half)
    st[...] = dec_last * st_old + jnp.where(diag, upd, 0.0)

    lane = _iota((q, LANES), 1)
    cbs = []
    for g in range(SSM_GROUPS):
        cm = jnp.where(_half_mask(lane, g == 1), cslab, 0.0)
        cbs.append(_mm(cm, bt))
    heads_per_group = SSM_HEADS // SSM_GROUPS
    for jp in range(SSM_HEADS // 2):
        ms = []
        for hh in (2 * jp, 2 * jp + 1):
            seg = acs[:, hh:hh + 1] - acs_t[hh:hh + 1, :]
            ldec = jnp.exp(jnp.where(causal, seg, NEG_BIG))
            ms.append(cbs[hh // heads_per_group] * ldec)
        lhs = jnp.concatenate(ms, axis=1)
        slab = xdt[:, jp * LANES:(jp + 1) * LANES]
        rhs = jnp.concatenate([jnp.where(lane < SSM_HEAD_DIM, slab, 0.0),
                               jnp.where(lane >= SSM_HEAD_DIM, slab, 0.0)], axis=0)
        ybuf[:, jp * LANES:(jp + 1) * LANES] = _mm(lhs, rhs)

    y = ybuf[...] + y_off + de_ref[...] * xa
    u = y * _silu(z_ref[...].astype(F32))
    outs = []
    for g in range(SSM_GROUPS):
        ug = u[:, g * half:(g + 1) * half]
        outs.append(ug * lax.rsqrt(jnp.mean(ug * ug, axis=-1, keepdims=True) + RMS_EPS))
    y_ref[...] = (jnp.concatenate(outs, axis=1) * gn_ref[...]).astype(y_ref.dtype)

    @pl.when(c == pl.num_programs(1) - 1)
    def _():
        stout_ref[...] = st[...]


def _ssd_prompt(pm, pt, prm, dims, t_all):
    d, tm, seq, b_p, b_s = dims
    q = min(SSD_CHUNK, seq)
    nc = seq // q
    n_bc = 2 * SSM_GROUPS * SSM_STATE
    full = lambda shape: pl.BlockSpec(shape, lambda b, c: tuple(0 for _ in shape))
    return pl.pallas_call(
        _ssd_p_kernel,
        grid=(b_p, nc),
        in_specs=[pl.BlockSpec((q, d), lambda b, c: (b * nc + c, C_Z // d)),
                  pl.BlockSpec((q, d), lambda b, c: (b * nc + c, C_XS // d)),
                  pl.BlockSpec((q, n_bc), lambda b, c: (b * nc + c, T_BC // n_bc)),
                  pl.BlockSpec((q, n_bc), lambda b, c: (b * nc + c, T_DT // n_bc)),
                  full((SSM_CONV_WIDTH, d)), full((SSM_CONV_WIDTH, n_bc)),
                  full((1, d)), full((1, n_bc)), full((1, LANES)), full((1, LANES)),
                  full((1, d)), full((1, d)), full((LANES, d))],
        out_specs=[pl.BlockSpec((q, d), lambda b, c: (b * nc + c, 0)),
                   pl.BlockSpec((None, SUBLANES, d), lambda b, c: (b, 0, 0)),
                   pl.BlockSpec((None, SUBLANES, n_bc), lambda b, c: (b, 0, 0)),
                   pl.BlockSpec((None, 2 * SSM_STATE, d), lambda b, c: (b, 0, 0))],
        out_shape=[jax.ShapeDtypeStruct((t_all, d), MXU_DTYPE),
                   jax.ShapeDtypeStruct((b_p, SUBLANES, d), F32),
                   jax.ShapeDtypeStruct((b_p, SUBLANES, n_bc), F32),
                   jax.ShapeDtypeStruct((b_p, 2 * SSM_STATE, d), F32)],
        scratch_shapes=[pltpu.VMEM((q + SUBLANES, d), F32),
                        pltpu.VMEM((q + SUBLANES, n_bc), F32),
                        pltpu.VMEM((2 * SSM_STATE, d), F32),
                        pltpu.VMEM((q, d), F32)],
        compiler_params=_cp("arbitrary", "arbitrary"),
        name="ssd_prompt",
    )(pm, pm, pt, pt, prm["wx"], prm["wbc"], prm["bx"], prm["bbc"], prm["dtb"], prm["alog"],
      prm["de"], prm["gn"], prm["rexp"])


def _ssd_s_kernel(xs_ref, bc_ref, dt_ref, cx_ref, cbc_ref, wx_ref, wbc_ref, bx_ref, bbc_ref, dtb_ref,
                  alog_ref, de_ref, e2_ref, g2_ref, s_ref,
                  y_ref, snew_ref, csx_ref, csbc_ref, *, b_s, steps):
    j = pl.program_id(0)
    pn = SSM_HEAD_DIM * SSM_STATE
    extx = jnp.concatenate([cx_ref[...], xs_ref[...].astype(F32)], axis=0)
    extbc = jnp.concatenate([cbc_ref[...], bc_ref[...]], axis=0)

    def conv(ext, w, b):
        acc = b
        for k in range(SSM_CONV_WIDTH):
            acc = acc + w[k:k + 1, :] * ext[k * b_s:(k + steps) * b_s]
        return _silu(acc)

    xa = conv(extx, wx_ref[...], bx_ref[...])
    bca = conv(extbc, wbc_ref[...], bbc_ref[...])
    csx_ref[...] = extx[steps * b_s:]
    csbc_ref[...] = extbc[steps * b_s:]
    dtv = _softplus(dt_ref[:, 0:LANES] + dtb_ref[...])
    dec = jnp.exp(dtv * (-jnp.exp(alog_ref[...])))
    lane = _iota(dtv.shape, 1)
    grp = (2 * j) // (SSM_HEADS // SSM_GROUPS)
    own = (lane // SSM_STATE) == grp
    bslab = bca[:, 0:LANES]
    cslab = bca[:, LANES:2 * LANES]
    b2 = jnp.where(own, bslab, pltpu.roll(bslab, SSM_STATE, 1))
    c2 = jnp.where(own, cslab, pltpu.roll(cslab, SSM_STATE, 1))
    ys = [None] * steps
    for lh in range(2):
        hh = 2 * j + lh
        dt_col = jnp.sum(jnp.where(lane == hh, dtv, 0.0), axis=1, keepdims=True)
        dec_col = jnp.sum(jnp.where(lane == hh, dec, 0.0), axis=1, keepdims=True)
        s = s_ref[:, lh * pn:(lh + 1) * pn]
        for t in range(steps):
            rows = slice(t * b_s, (t + 1) * b_s)
            xe = _mm_sel(xa[rows] * dt_col[rows], e2_ref[lh])
            s = dec_col[rows] * s + xe * _tile_lanes(b2[rows], pn // LANES)
            yt = _mm(s * _tile_lanes(c2[rows], pn // LANES), g2_ref[lh])
            ys[t] = yt if ys[t] is None else ys[t] + yt
        snew_ref[:, lh * pn:(lh + 1) * pn] = s
    y_ref[...] = jnp.concatenate(ys, axis=0) + de_ref[...] * xa


def _ssd_sample(pm, pt, cs_x, cs_bc, state3d, layer, prm, dims, steps):
    d, tm, seq, b_p, b_s = dims
    t_s = b_s * steps
    rb = (b_p * seq) // t_s
    n_bc = 2 * SSM_GROUPS * SSM_STATE
    pn = SSM_HEAD_DIM * SSM_STATE
    ns = (SSM_CONV_WIDTH - 1) * b_s
    kern = functools.partial(_ssd_s_kernel, b_s=b_s, steps=steps)
    full = lambda shape: pl.BlockSpec(shape, lambda j: tuple(0 for _ in shape))
    return pl.pallas_call(
        kern,
        grid=(SSM_HEADS // 2,),
        in_specs=[pl.BlockSpec((t_s, LANES), lambda j: (rb, C_XS // LANES + j)),
                  pl.BlockSpec((t_s, n_bc), lambda j: (rb, T_BC // n_bc)),
                  pl.BlockSpec((t_s, n_bc), lambda j: (rb, T_DT // n_bc)),
                  pl.BlockSpec((ns, LANES), lambda j: (0, j)),
                  full((ns, n_bc)),
                  pl.BlockSpec((SSM_CONV_WIDTH, LANES), lambda j: (0, j)),
                  full((SSM_CONV_WIDTH, n_bc)),
                  pl.BlockSpec((1, LANES), lambda j: (0, j)),
                  full((1, n_bc)), full((1, LANES)), full((1, LANES)),
                  pl.BlockSpec((1, LANES), lambda j: (0, j)),
                  full((2, LANES, pn)), full((2, pn, LANES)),
                  pl.BlockSpec((None, b_s, 2 * pn), lambda j: (layer, 0, j))],
        out_specs=[pl.BlockSpec((t_s, LANES), lambda j: (0, j)),
                   pl.BlockSpec((b_s, 2 * pn), lambda j: (0, j)),
                   pl.BlockSpec((ns, LANES), lambda j: (0, j)),
                   full((ns, n_bc))],
        out_shape=[jax.ShapeDtypeStruct((t_s, d), F32),
                   jax.ShapeDtypeStruct(state3d.shape[1:], F32),
                   jax.ShapeDtypeStruct((ns, d), F32),
                   jax.ShapeDtypeStruct((ns, n_bc), F32)],
        compiler_params=_cp("arbitrary"),
        name="ssd_sample",
    )(pm, pt, pt, cs_x, cs_bc, prm["wx"], prm["wbc"], prm["bx"], prm["bbc"], prm["dtb"], prm["alog"],
      prm["de"], prm["e2"], prm["g2"], state3d)


def _gnorm_s_kernel(y_ref, z_ref, gn_ref, o_ref):
    u = y_ref[...] * _silu(z_ref[...].astype(F32))
    half = u.shape[1] // SSM_GROUPS
    outs = []
    for g in range(SSM_GROUPS):
        ug = u[:, g * half:(g + 1) * half]
        outs.append(ug * lax.rsqrt(jnp.mean(ug * ug, axis=-1, keepdims=True) + RMS_EPS))
    o_ref[...] = (jnp.concatenate(outs, axis=1) * gn_ref[...]).astype(o_ref.dtype)


def _gnorm_sample(y_raw, p, gn, dims, steps):
    d, tm, seq, b_p, b_s = dims
    t_s = b_s * steps
    rb = (b_p * seq) // t_s
    return pl.pallas_call(
        _gnorm_s_kernel,
        grid=(1,),
        in_specs=[pl.BlockSpec((t_s, d), lambda i: (0, 0)),
                  pl.BlockSpec((t_s, d), lambda i: (rb, C_Z // d)),
                  pl.BlockSpec((1, d), lambda i: (0, 0))],
        out_specs=pl.BlockSpec((t_s, d), lambda i: (0, 0)),
        out_shape=jax.ShapeDtypeStruct((t_s, d), MXU_DTYPE),
        compiler_params=_cp("arbitrary"),
        name="gnorm_sample",
    )(y_raw, p, gn)


def _rope(x, cos_f, sin_a, sin_b):
    n = x.shape[1] // cos_f.shape[1]
    half = ROPE_DIMS // 2
    return (x * _tile_lanes(cos_f, n) + pltpu.roll(x, half, 1) * _tile_lanes(sin_a, n)
            + pltpu.roll(x, x.shape[1] - half, 1) * _tile_lanes(sin_b, n))


def _attn_p_kernel(sink_ref, q_ref, k_ref, v_ref, cos_ref, sa_ref, sb_ref,
                   o_ref, kn_ref, vn_ref, kprev, vprev):
    c = pl.program_id(1)
    w = q_ref.shape[0]
    cos_f, sin_a, sin_b = cos_ref[...], sa_ref[...], sb_ref[...]
    scale = HEAD_DIM ** -0.5
    k = _rope(k_ref[...], cos_f, sin_a, sin_b)
    v = v_ref[...]

    @pl.when(c == 0)
    def _():
        kprev[...] = jnp.zeros(kprev.shape, F32)
        vprev[...] = jnp.zeros(vprev.shape, F32)

    kk = jnp.concatenate([kprev[...], k], axis=0)
    vv = jnp.concatenate([vprev[...], v], axis=0)
    key = _iota((2 * w, w), 0)
    qry = _iota((2 * w, w), 1)
    valid = (key > qry) & (key <= qry + w) & ((c > 0) | (key >= w))
    valid = _tile_lanes(valid.astype(F32), Q_PER_KV) > 0.5
    lane_k = _iota((2 * w, LANES), 1)
    sub_v = _iota((LANES, 2 * w), 0)
    for a in range(KV_HEADS // 2):
        slabs = range(a * Q_PER_KV, (a + 1) * Q_PER_KV)
        qst = jnp.concatenate(
            [(_rope(q_ref[:, LANES * j:LANES * (j + 1)].astype(F32), cos_f, sin_a, sin_b) * scale).astype(MXU_DTYPE)
             for j in slabs], axis=0)
        ksl = kk[:, LANES * a:LANES * (a + 1)]
        vt = vv[:, LANES * a:LANES * (a + 1)].T
        o_t = None
        for hk in range(2):
            ks = jnp.where(_half_mask(lane_k, hk == 1), ksl, 0.0).astype(MXU_DTYPE)
            vs = jnp.where(_half_mask(sub_v, hk == 1), vt, 0.0).astype(MXU_DTYPE)
            sink = jnp.concatenate([jnp.full((1, w), sink_ref[HEAD_ORDER[2 * j + hk]], F32) for j in slabs], axis=1)
            s = lax.dot_general(ks, qst, (((1,), (1,)), ((), ())), preferred_element_type=F32)
            s = jnp.where(valid, s, NEG_BIG)
            m = jnp.maximum(jnp.max(s, axis=0, keepdims=True), sink)
            pr = jnp.exp(s - m)
            den = jnp.sum(pr, axis=0, keepdims=True) + jnp.exp(sink - m)
            pn = (pr * (1.0 / den)).astype(MXU_DTYPE)
            o = jnp.dot(vs, pn, preferred_element_type=F32)
            o_t = o if o_t is None else o_t + o
        o_all = o_t.T
        for g, j in enumerate(slabs):
            o_ref[:, LANES * j:LANES * (j + 1)] = o_all[g * w:(g + 1) * w].astype(o_ref.dtype)
    kprev[...] = k
    vprev[...] = v
    kn_ref[...] = k
    vn_ref[...] = v


def _attn_prompt(pm, pt, sinks, tabs, dims, t_all):
    d, tm, seq, b_p, b_s = dims
    w = WINDOW
    nb = seq // w
    d_kv = KV_HEADS * HEAD_DIM
    tab_spec = pl.BlockSpec((w, LANES), lambda b, c: (c, 0))
    const = lambda shape: pl.BlockSpec(shape, lambda b, c: tuple(0 for _ in shape))
    return pl.pallas_call(
        _attn_p_kernel,
        grid=(b_p, nb),
        in_specs=[pl.BlockSpec(memory_space=pltpu.SMEM),
                  pl.BlockSpec((w, d), lambda b, c: (b * nb + c, C_Q // d)),
                  pl.BlockSpec((w, d_kv), lambda b, c: (b * nb + c, T_K // d_kv)),
                  pl.BlockSpec((w, d_kv), lambda b, c: (b * nb + c, T_V // d_kv)),
                  tab_spec, tab_spec, tab_spec],
        out_specs=[pl.BlockSpec((w, d), lambda b, c: (b * nb + c, 0)),
                   pl.BlockSpec((None, w, d_kv), lambda b, c: (b, 0, 0)),
                   pl.BlockSpec((None, w, d_kv), lambda b, c: (b, 0, 0))],
        out_shape=[jax.ShapeDtypeStruct((t_all, d), MXU_DTYPE),
                   jax.ShapeDtypeStruct((b_p, w, d_kv), F32),
                   jax.ShapeDtypeStruct((b_p, w, d_kv), F32)],
        scratch_shapes=[pltpu.VMEM((w, d_kv), F32), pltpu.VMEM((w, d_kv), F32)],
        compiler_params=_cp("arbitrary", "arbitrary"),
        name="attn_prompt",
    )(sinks, pm, pt, pt, *tabs)


def _rope_s_kernel(q_ref, k_ref, cos_ref, sa_ref, sb_ref, qo_ref, ko_ref):
    cos_f, sin_a, sin_b = cos_ref[...], sa_ref[...], sb_ref[...]
    qo_ref[...] = _rope(q_ref[...].astype(F32), cos_f, sin_a, sin_b)
    ko_ref[...] = _rope(k_ref[...], cos_f, sin_a, sin_b)


def _rope_sample(pm, pt, tabs, dims, steps):
    d, tm, seq, b_p, b_s = dims
    t_s = b_s * steps
    rb = (b_p * seq) // t_s
    d_kv = KV_HEADS * HEAD_DIM
    tab_spec = pl.BlockSpec((t_s, LANES), lambda i: (0, 0))
    return pl.pallas_call(
        _rope_s_kernel,
        grid=(1,),
        in_specs=[pl.BlockSpec((t_s, d), lambda i: (rb, C_Q // d)),
                  pl.BlockSpec((t_s, d_kv), lambda i: (rb, T_K // d_kv)),
                  tab_spec, tab_spec, tab_spec],
        out_specs=[pl.BlockSpec((t_s, d), lambda i: (0, 0)),
                   pl.BlockSpec((t_s, d_kv), lambda i: (0, 0))],
        out_shape=[jax.ShapeDtypeStruct((t_s, d), F32),
                   jax.ShapeDtypeStruct((t_s, d_kv), F32)],
        compiler_params=_cp("arbitrary"),
        name="rope_sample",
    )(pm, pt, *tabs)


def _attn_s_kernel(q_ref, kn_ref, vn_ref, ko_ref, vo_ref, sink_ref, valid_ref, o_ref):
    tb, nq, d_kv = q_ref.shape
    w = kn_ref.shape[1]
    n_old = ko_ref.shape[1]
    pad = jnp.zeros((w - n_old, d_kv), F32)
    valid = valid_ref[...] > 0.5
    sink = sink_ref[:, 0:1]
    scale = HEAD_DIM ** -0.5
    orow = _iota((nq, HEAD_DIM), 0)
    rows_per_kv = nq // KV_HEADS
    for b in range(tb):
        kk = jnp.concatenate([kn_ref[b], ko_ref[b], pad], axis=0)
        vv = jnp.concatenate([vn_ref[b], vo_ref[b], pad], axis=0)
        s = _mm_nt(q_ref[b], kk) * scale
        s = jnp.where(valid, s, NEG_BIG)
        m = jnp.maximum(jnp.max(s, axis=-1, keepdims=True), sink)
        pr = jnp.exp(s - m)
        den = jnp.sum(pr, axis=-1, keepdims=True) + jnp.exp(sink - m)
        o = _mm(pr, vv) / den
        acc = jnp.zeros((nq, HEAD_DIM), F32)
        for kh in range(KV_HEADS):
            mine = (orow >= kh * rows_per_kv) & (orow < (kh + 1) * rows_per_kv)
            acc = acc + jnp.where(mine, o[:, kh * HEAD_DIM:(kh + 1) * HEAD_DIM], 0.0)
        o_ref[b] = acc


def _attn_sample(qbd, k_new, v_new, k_old, v_old, sink_col, steps, tb=8):
    b_s, nq, d_kv = qbd.shape
    w = k_new.shape[1]
    t = (jnp.arange(nq) % steps)[:, None]
    col = jnp.arange(2 * w)[None, :]
    valid = jnp.where(col < w, col <= (w - steps) + t, (col - w > t) & (col - w < steps)).astype(F32)
    return pl.pallas_call(
        _attn_s_kernel,
        grid=(b_s // tb,),
        in_specs=[pl.BlockSpec((tb, nq, d_kv), lambda i: (i, 0, 0)),
                  pl.BlockSpec((tb, w, d_kv), lambda i: (i, 0, 0)),
                  pl.BlockSpec((tb, w, d_kv), lambda i: (i, 0, 0)),
                  pl.BlockSpec((tb, SUBLANES, d_kv), lambda i: (i, 0, 0)),
                  pl.BlockSpec((tb, SUBLANES, d_kv), lambda i: (i, 0, 0)),
                  pl.BlockSpec((nq, LANES), lambda i: (0, 0)),
                  pl.BlockSpec((nq, 2 * w), lambda i: (0, 0))],
        out_specs=pl.BlockSpec((tb, nq, HEAD_DIM), lambda i: (i, 0, 0)),
        out_shape=jax.ShapeDtypeStruct((b_s, nq, HEAD_DIM), F32),
        compiler_params=_cp("arbitrary"),
        name="attn_sample",
    )(qbd, k_new, v_new, k_old, v_old, sink_col, valid)


def _merge_kernel(h_ref, yap_ref, ybp_ref, ycp_ref, yas_ref, ybs_ref, ycs_ref, x_ref,
                  wg_ref, bg_ref, pa_ref, pb_ref, pc_ref, wo_ref,
                  g1p_ref, g1s_ref, nw_ref, shp_ref, scp_ref, shs_ref, scs_ref, rw_ref, rb_ref,
                  x1_ref, h2_ref, ti_ref, tg_ref, *, n_p, reps):
    i = pl.program_id(0)
    d = x_ref.shape[1]
    is_p = i < n_p
    ya = jnp.where(is_p, yap_ref[...], yas_ref[...])
    yb = jnp.where(is_p, ybp_ref[...], ybs_ref[...])
    yc = jnp.where(is_p, ycp_ref[...], ycs_ref[...])
    g = _sigmoid(jnp.dot(h_ref[...], wg_ref[...], preferred_element_type=F32) + bg_ref[...])
    merged = (g[:, 0:d] * _mm(ya, pa_ref[...]) + g[:, d:2 * d] * _mm(yb, pb_ref[...])
              + g[:, 2 * d:3 * d] * _mm(yc, pc_ref[...]))
    mix = _mm(merged, wo_ref[...])

    def finish(g1, sh, sc):
        x1 = x_ref[...] + g1 * mix
        x1_ref[...] = x1
        h2_ref[...] = _rms(x1, nw_ref[...]) * (1.0 + sc) + sh

    @pl.when(i < n_p)
    def _():
        finish(g1p_ref[...], shp_ref[...], scp_ref[...])

    @pl.when(i >= n_p)
    def _():
        finish(_tile_rows(g1s_ref[...], reps), _tile_rows(shs_ref[...], reps), _tile_rows(scs_ref[...], reps))

    h2 = h2_ref[...]
    rw = rw_ref[...]
    hs = _split(h2, 2)
    ws = _split(rw, 2)
    if len(hs) == 1:
        logits = jnp.dot(h2, rw, preferred_element_type=F32)
    else:
        logits = (jnp.dot(hs[0], ws[0], preferred_element_type=F32)
                  + jnp.dot(hs[0], ws[1], preferred_element_type=F32)
                  + jnp.dot(hs[1], ws[0], preferred_element_type=F32))
    logits = logits + rb_ref[...]
    lane = _iota(logits.shape, 1).astype(F32)
    vals, idxs = [], []
    cur = logits
    for _ in range(TOP_K):
        mx = jnp.max(cur, axis=-1, keepdims=True)
        ix = jnp.min(jnp.where(cur == mx, lane, float(LANES)), axis=-1, keepdims=True)
        vals.append(mx)
        idxs.append(ix)
        cur = jnp.where(lane == ix, -jnp.inf, cur)
    es = [jnp.exp(v - vals[0]) for v in vals]
    den = es[0]
    for e in es[1:]:
        den = den + e
    ti = jnp.zeros(logits.shape, F32)
    tg = jnp.zeros(logits.shape, F32)
    for k in range(TOP_K):
        ti = jnp.where(lane == float(k), idxs[k], ti)
        tg = jnp.where(lane == float(k), es[k] / den, tg)
    ti_ref[...] = ti.astype(I32)
    tg_ref[...] = tg


def _merge(h, ys_p, ys_s, x_all, wts, ada_p, ada_s, dims, tm_m=256):
    d, tm, seq, b_p, b_s = dims
    t_all = x_all.shape[0]
    tm_m = min(tm_m, tm)
    mdims = (d, tm_m, seq, b_p, b_s)
    n_p = b_p * seq // tm_m
    g1p, g1s = _mod_specs(mdims, 2, 1)
    shp, shs = _mod_specs(mdims, 3, 1)
    scp, scs = _mod_specs(mdims, 4, 1)
    row = lambda n: pl.BlockSpec((tm_m, n), lambda i: (i, 0))
    row_p = pl.BlockSpec((tm_m, d), lambda i: (jnp.minimum(i, n_p - 1), 0))
    row_s = pl.BlockSpec((tm_m, d), lambda i: (jnp.maximum(i - n_p, 0), 0))
    const = lambda shape: pl.BlockSpec(shape, lambda i: tuple(0 for _ in shape))
    kern = functools.partial(_merge_kernel, n_p=n_p, reps=max(tm_m // b_s, 1))
    return pl.pallas_call(
        kern,
        grid=(t_all // tm_m,),
        in_specs=[row(d), row_p, row_p, row_p, row_s, row_s, row_s, row(d),
                  const((d, 3 * d)), const((1, 3 * d)), const((d, d)), const((d, d)), const((d, d)),
                  const((d, d)),
                  g1p, g1s, const((1, d)), shp, scp, shs, scs,
                  const((d, LANES)), const((1, LANES))],
        out_specs=[row(d), row(d), row(LANES), row(LANES)],
        out_shape=[jax.ShapeDtypeStruct((t_all, d), F32),
                   jax.ShapeDtypeStruct((t_all, d), F32),
                   jax.ShapeDtypeStruct((t_all, LANES), I32),
                   jax.ShapeDtypeStruct((t_all, LANES), F32)],
        compiler_params=_cp("arbitrary"),
        name="merge",
    )(h, *ys_p, *ys_s, x_all, wts["wg"], wts["bg"], wts["pa"], wts["pb"], wts["pc"], wts["wo"],
      ada_p, ada_s, wts["nffn"], ada_p, ada_p, ada_s, ada_s, wts["rw"], wts["rb"])


def _rank_kernel(ti_ref, dest_ref, cnt_ref, run, tot):
    ph = pl.program_id(0)
    i = pl.program_id(1)
    tm = ti_ref.shape[0]
    ti = ti_ref[...]
    lane = _iota((tm, LANES), 1)
    oh = jnp.zeros((tm, LANES), F32)
    for k in range(TOP_K):
        oh = oh + jnp.where(lane == ti[:, k:k + 1], 1.0, 0.0)
    ones = jnp.ones((SUBLANES, tm), MXU_DTYPE)
    colsum = jnp.dot(ones, oh.astype(MXU_DTYPE), preferred_element_type=F32)[0:1, :]

    @pl.when((ph == 0) & (i == 0))
    def _():
        tot[...] = jnp.zeros(tot.shape, F32)

    @pl.when(ph == 0)
    def _():
        tot[...] = tot[...] + colsum

    @pl.when((ph == 1) & (i == 0))
    def _():
        run[...] = jnp.zeros(run.shape, F32)

    @pl.when(ph == 1)
    def _():
        r = _iota((LANES, LANES), 0)
        c = _iota((LANES, LANES), 1)
        upper = jnp.where(r < c, 1.0, 0.0).astype(MXU_DTYPE)
        starts = _mm_sel(jnp.broadcast_to(tot[...], (SUBLANES, LANES)), upper)[0:1, :]
        rr = _iota((tm, tm), 0)
        cc = _iota((tm, tm), 1)
        lower = jnp.where(rr > cc, 1.0, 0.0).astype(MXU_DTYPE)
        pre = jnp.dot(lower, oh.astype(MXU_DTYPE), preferred_element_type=F32)
        pos = pre + run[...] + starts
        dest = jnp.zeros((tm, LANES), F32)
        for k in range(TOP_K):
            dk = jnp.sum(jnp.where(lane == ti[:, k:k + 1], pos, 0.0), axis=1, keepdims=True)
            dest = jnp.where(lane == k, dk, dest)
        dest_ref[...] = dest.astype(I32)
        run[...] = run[...] + colsum

    cnt_ref[...] = jnp.broadcast_to(tot[...], cnt_ref.shape)


def _rank(topi, tm):
    t_all = topi.shape[0]
    return pl.pallas_call(
        _rank_kernel,
        grid=(2, t_all // tm),
        in_specs=[pl.BlockSpec((tm, LANES), lambda ph, i: (i, 0))],
        out_specs=[pl.BlockSpec((tm, LANES), lambda ph, i: (i * ph, 0)),
                   pl.BlockSpec((SUBLANES, LANES), lambda ph, i: (0, 0))],
        out_shape=[jax.ShapeDtypeStruct((t_all, LANES), I32),
                   jax.ShapeDtypeStruct((SUBLANES, LANES), F32)],
        scratch_shapes=[pltpu.VMEM((1, LANES), F32), pltpu.VMEM((1, LANES), F32)],
        compiler_params=_cp("arbitrary", "arbitrary"),
        name="rank",
    )(topi)


def _dispatch_kernel(dest_ref, h2_ref, xs_ref, sem):
    tm = h2_ref.shape[0]

    def issue(r, carry):
        for k in range(TOP_K):
            pltpu.make_async_copy(h2_ref.at[pl.ds(r, 1)], xs_ref.at[pl.ds(dest_ref[r * TOP_K + k], 1)],
                                  sem).start()
        return carry

    lax.fori_loop(0, tm, issue, 0)
    for _ in range(TOP_K):
        pltpu.make_async_copy(h2_ref, xs_ref.at[pl.ds(0, tm)], sem).wait()


def _dispatch(dest_flat, h2, tm=256):
    t_all, d = h2.shape
    return pl.pallas_call(
        _dispatch_kernel,
        grid=(t_all // tm,),
        in_specs=[pl.BlockSpec((tm * TOP_K,), lambda i: (i,), memory_space=pltpu.SMEM),
                  pl.BlockSpec((tm, d), lambda i: (i, 0))],
        out_specs=pl.BlockSpec(memory_space=pl.ANY),
        out_shape=jax.ShapeDtypeStruct((t_all * TOP_K, d), F32),
        scratch_shapes=[pltpu.SemaphoreType.DMA(())],
        compiler_params=_cp("arbitrary"),
        name="dispatch",
    )(dest_flat, h2)


def _expert_kernel(tile_ref, exp_ref, lo_ref, hi_ref, firste_ref, firstt_ref, valid_ref,
                   x_ref, wgu_ref, bgu_ref, wd_ref, bd_ref, o_ref, wgu_s, wd_s):
    w = pl.program_id(0)
    tmx = x_ref.shape[0]
    f = wd_ref.shape[0]

    @pl.when(firste_ref[w] == 1)
    def _():
        wgu_s[...] = wgu_ref[...].astype(wgu_s.dtype)
        wd_s[...] = wd_ref[...].astype(wd_s.dtype)

    @pl.when(valid_ref[w] == 1)
    def _():
        gu = jnp.dot(x_ref[...].astype(MXU_DTYPE), wgu_s[...], preferred_element_type=F32) + bgu_ref[...]
        g = jnp.minimum(gu[:, 0:f], SWIGLU_LIMIT)
        u = jnp.clip(gu[:, f:2 * f], -SWIGLU_LIMIT, SWIGLU_LIMIT)
        act = g * _sigmoid(SWIGLU_ALPHA * g) * (u + 1.0)
        res = jnp.dot(act.astype(MXU_DTYPE), wd_s[...], preferred_element_type=F32) + bd_ref[...]
        rows = tile_ref[w] * tmx + _iota((tmx, 1), 0)
        mine = (rows >= lo_ref[w]) & (rows < hi_ref[w])

        @pl.when(firstt_ref[w] == 1)
        def _():
            o_ref[...] = jnp.where(mine, res, 0.0)

        @pl.when(firstt_ref[w] == 0)
        def _():
            o_ref[...] = jnp.where(mine, res, o_ref[...])


def _experts(meta, xs, w_gu, b_gu, w_d, b_d, layer, tmx):
    tk, d = xs.shape
    depth, n_e, _, f2 = w_gu.shape
    f = f2 // 2
    n_w = meta[0].shape[0]
    grid_spec = pltpu.PrefetchScalarGridSpec(
        num_scalar_prefetch=7,
        grid=(n_w,),
        in_specs=[pl.BlockSpec((tmx, d), lambda w, tl, ex, *_: (tl[w], 0)),
                  pl.BlockSpec((None, None, d, f2), lambda w, tl, ex, *_: (layer, ex[w], 0, 0)),
                  pl.BlockSpec((None, None, 1, f2), lambda w, tl, ex, *_: (layer, ex[w], 0, 0)),
                  pl.BlockSpec((None, None, f, d), lambda w, tl, ex, *_: (layer, ex[w], 0, 0)),
                  pl.BlockSpec((None, None, 1, d), lambda w, tl, ex, *_: (layer, ex[w], 0, 0))],
        out_specs=pl.BlockSpec((tmx, d), lambda w, tl, ex, *_: (tl[w], 0)),
        scratch_shapes=[pltpu.VMEM((d, f2), MXU_DTYPE), pltpu.VMEM((f, d), MXU_DTYPE)],
    )
    return pl.pallas_call(
        _expert_kernel,
        grid_spec=grid_spec,
        out_shape=jax.ShapeDtypeStruct((tk, d), F32),
        compiler_params=_cp("arbitrary"),
        name="experts",
    )(*meta, xs, w_gu, b_gu.reshape(depth, n_e, 1, f2), w_d, b_d.reshape(depth, n_e, 1, d))


def _expert_schedule(counts, tk, tmx):
    n_e = counts.shape[0]
    n_w = tk // tmx + n_e - 1
    e_ids = jnp.arange(n_e, dtype=I32)
    tri = (e_ids[:, None] <= e_ids[None, :]).astype(I32)
    ends = jnp.sum(counts[:, None] * tri, axis=0)
    starts = ends - counts
    first_tile = starts // tmx
    last_tile = jnp.maximum(ends - 1, 0) // tmx
    n_t = jnp.where(counts > 0, last_tile - first_tile + 1, 0)
    cum = jnp.sum(n_t[:, None] * tri, axis=0)
    total = jnp.sum(n_t)

    def item(wv):
        ex = jnp.sum((cum[None, :] <= wv[:, None]).astype(I32), axis=1)
        oh = (ex[:, None] == e_ids[None, :]).astype(I32)
        pick = lambda v: jnp.sum(oh * v[None, :], axis=1)
        tile = pick(first_tile) + wv - (pick(cum) - pick(n_t))
        return ex, tile, pick(starts), pick(ends)

    w = jnp.arange(n_w, dtype=I32)
    wc = jnp.minimum(w, total - 1)
    ex, tile, lo, hi = item(wc)
    ex_prev, tile_prev, _, _ = item(jnp.maximum(wc - 1, 0))
    valid = w < total
    first_e = valid & ((w == 0) | (ex != ex_prev))
    first_t = valid & ((w == 0) | (tile != tile_prev))
    return (tile, ex, lo, hi, first_e.astype(I32), first_t.astype(I32), valid.astype(I32))


def _combine_kernel(dest_ref, eo_ref, x1_ref, tg_ref, g2p_ref, g2s_ref, nf_ref, o_ref, buf, sem, *,
                    n_p, reps, final):
    i = pl.program_id(0)
    tm = x1_ref.shape[0]

    def issue(r, carry):
        for k in range(TOP_K):
            pltpu.make_async_copy(eo_ref.at[pl.ds(dest_ref[r * TOP_K + k], 1)], buf.at[k, pl.ds(r, 1)],
                                  sem).start()
        return carry

    lax.fori_loop(0, tm, issue, 0)
    for k in range(TOP_K):
        pltpu.make_async_copy(eo_ref.at[pl.ds(0, tm)], buf.at[k], sem).wait()
    tg = tg_ref[...]
    y = tg[:, 0:1] * buf[0]
    for k in range(1, TOP_K):
        y = y + tg[:, k:k + 1] * buf[k]

    def finish(g2):
        x2 = x1_ref[...] + g2 * y
        o_ref[...] = _rms(x2, nf_ref[...]) if final else x2

    @pl.when(i < n_p)
    def _():
        finish(g2p_ref[...])

    @pl.when(i >= n_p)
    def _():
        finish(_tile_rows(g2s_ref[...], reps))


def _combine(dest_flat, eo, x1, topg, ada_p, ada_s, norm_final, dims, final, tm_c=256):
    d, tm, seq, b_p, b_s = dims
    t_all = x1.shape[0]
    tm_c = min(tm_c, tm)
    cdims = (d, tm_c, seq, b_p, b_s)
    n_p = b_p * seq // tm_c
    g2p, g2s = _mod_specs(cdims, 5, 1)
    kern = functools.partial(_combine_kernel, n_p=n_p, reps=max(tm_c // b_s, 1), final=final)
    return pl.pallas_call(
        kern,
        grid=(t_all // tm_c,),
        in_specs=[pl.BlockSpec((tm_c * TOP_K,), lambda i: (i,), memory_space=pltpu.SMEM),
                  pl.BlockSpec(memory_space=pl.ANY),
                  pl.BlockSpec((tm_c, d), lambda i: (i, 0)),
                  pl.BlockSpec((tm_c, LANES), lambda i: (i, 0)),
                  g2p, g2s,
                  pl.BlockSpec((1, d), lambda i: (0, 0))],
        out_specs=pl.BlockSpec((tm_c, d), lambda i: (i, 0)),
        out_shape=jax.ShapeDtypeStruct((t_all, d), F32),
        scratch_shapes=[pltpu.VMEM((TOP_K, tm_c, d), F32), pltpu.SemaphoreType.DMA(())],
        compiler_params=_cp("arbitrary"),
        name="combine",
    )(dest_flat, eo, x1, topg, ada_p, ada_s, norm_final.reshape(1, d))


def _rope_tables(pos):
    half = ROPE_DIMS // 2
    inv_freq = jnp.exp(-math.log(ROPE_THETA) * jnp.arange(half, dtype=F32) / half)
    ang = pos.astype(F32)[:, None] * inv_freq[None, :]
    cos, sin = jnp.cos(ang), jnp.sin(ang)
    n = pos.shape[0]
    rest = HEAD_DIM - ROPE_DIMS
    cos_h = jnp.concatenate([cos, cos, jnp.ones((n, rest), F32)], axis=1)
    sa_h = jnp.concatenate([jnp.zeros((n, half), F32), sin, jnp.zeros((n, rest), F32)], axis=1)
    sb_h = jnp.concatenate([-sin, jnp.zeros((n, half + rest), F32)], axis=1)
    rep = LANES // HEAD_DIM
    return tuple(jnp.tile(t, (1, rep)) for t in (cos_h, sa_h, sb_h))


def _pad_lanes(v, n, value=0.0):
    return jnp.pad(v, ((0, 0), (0, n - v.shape[1])), constant_values=value)


def kernel(x_prompt, x_sample, state_conv_a, state_conv_ssm, state_ssm, cache_k, cache_v, c_prompt, c_sample, w_ada, b_ada, norm_mix, norm_ffn, w_in, conv_a_w, ssm_conv_w, ssm_conv_b, ssm_dt_bias, ssm_a_log, ssm_d, ssm_norm, attn_sinks, w_branch_gate, b_branch_gate, w_proj_a, w_proj_b, w_proj_c, w_out, router_w, router_b, w_gate_up, b_gate_up, w_down, b_down, norm_final):
    b_p, seq, d = x_prompt.shape
    b_s, steps, _ = x_sample.shape
    depth = w_ada.shape[0]
    t_p, t_s = b_p * seq, b_s * steps
    t_all = t_p + t_s
    tm = t_s
    dims = (d, tm, seq, b_p, b_s)
    n_e = router_w.shape[2]
    d_ssm = SSM_HEADS * SSM_HEAD_DIM
    n_bc = 2 * SSM_GROUPS * SSM_STATE
    d_kv = KV_HEADS * HEAD_DIM
    pn = SSM_HEAD_DIM * SSM_STATE
    assert d == d_ssm == ATTN_HEADS * HEAD_DIM and t_p % t_s == 0 and seq % t_s == 0
    assert t_s % 256 == 0 or t_s <= 256
    assert steps <= SUBLANES and PAST_LEN >= WINDOW and seq >= WINDOW and b_s % SUBLANES == 0

    x_all = jnp.concatenate([x_prompt.reshape(t_p, d), x_sample.transpose(1, 0, 2).reshape(t_s, d)], axis=0)
    ada = _ada_all(jnp.concatenate([c_prompt, c_sample], axis=0), w_ada, b_ada)

    o_z = 3 * d
    o_bc = o_z + 2 * d_ssm
    o_dt = o_bc + n_bc
    o_q = o_dt + SSM_HEADS
    o_k = o_q + d
    inv_order = [HEAD_ORDER.index(h) for h in range(ATTN_HEADS)]

    def to_head_order(a, axis):
        return jnp.concatenate([lax.slice_in_dim(a, h * HEAD_DIM, (h + 1) * HEAD_DIM, axis=axis)
                                for h in HEAD_ORDER], axis=axis)

    def from_head_order(a, axis):
        return jnp.concatenate([lax.slice_in_dim(a, i * HEAD_DIM, (i + 1) * HEAD_DIM, axis=axis)
                                for i in inv_order], axis=axis)

    w_pack = jnp.concatenate(
        [w_in[:, :, :o_bc], to_head_order(w_in[:, :, o_q:o_k], 2), w_in[:, :, o_bc:o_dt], w_in[:, :, o_k:],
         w_in[:, :, o_dt:o_q], jnp.zeros((depth, d, N_TAIL - T_DT - SSM_HEADS), F32)], axis=2).astype(MXU_DTYPE)

    hp = jnp.arange(d_ssm) // SSM_HEAD_DIM
    rexp = (jnp.arange(LANES)[:, None] == hp[None, :]).astype(MXU_DTYPE)
    lane_pn = jnp.arange(pn)
    e2 = jnp.stack([(jnp.arange(LANES)[:, None] == (lane_pn // SSM_STATE + lh * SSM_HEAD_DIM)[None, :])
                    for lh in range(2)]).astype(MXU_DTYPE)
    g2 = jnp.transpose(e2, (0, 2, 1))
    state_ssm3 = state_ssm.reshape(depth, b_s, SSM_HEADS * pn)

    tabs_p = _rope_tables(jnp.arange(seq, dtype=I32))
    tabs_s = _rope_tables(jnp.repeat(PAST_LEN + jnp.arange(steps, dtype=I32), b_s))
    eye_kv = jnp.eye(KV_HEADS, dtype=F32)

    outs = [[] for _ in range(10)]
    y_final = None
    for l in range(depth):
        ada_p = ada[l, :b_p].reshape(b_p, 1, 6 * d)
        ada_s = ada[l, b_p:]
        ssm_prm = dict(
            wx=ssm_conv_w[l, :, :d_ssm], wbc=ssm_conv_w[l, :, d_ssm:],
            bx=ssm_conv_b[l, :d_ssm].reshape(1, d_ssm), bbc=ssm_conv_b[l, d_ssm:].reshape(1, n_bc),
            dtb=_pad_lanes(ssm_dt_bias[l].reshape(1, SSM_HEADS), LANES),
            alog=_pad_lanes(ssm_a_log[l].reshape(1, SSM_HEADS), LANES),
            de=jnp.repeat(ssm_d[l], SSM_HEAD_DIM).reshape(1, d_ssm),
            gn=ssm_norm[l].reshape(1, d_ssm), rexp=rexp, e2=e2, g2=g2)
        wts = dict(
            wg=w_branch_gate[l].astype(MXU_DTYPE), bg=b_branch_gate[l].reshape(1, 3 * d),
            pa=w_proj_a[l].astype(MXU_DTYPE), pb=w_proj_b[l].astype(MXU_DTYPE),
            pc=to_head_order(w_proj_c[l], 0).astype(MXU_DTYPE), wo=w_out[l].astype(MXU_DTYPE),
            nffn=norm_ffn[l].reshape(1, d),
            rw=_pad_lanes(router_w[l], LANES), rb=_pad_lanes(router_b[l].reshape(1, n_e), LANES, NEG_BIG))

        p, p_tail, h = _inproj(x_all, norm_mix[l], ada_p, ada_s, w_pack, l, dims)

        ya_p, ca_p = _conva_prompt(p, conv_a_w[l], dims, t_p)
        ca_tm = state_conv_a[l].transpose(1, 0, 2).reshape((CONV_A_WIDTH - 1) * b_s, d)
        ya_s, ca_s = _conva_sample(p, ca_tm, conv_a_w[l], dims, steps)
        outs[0].append(ca_p[:, SUBLANES - (CONV_A_WIDTH - 1):])
        outs[1].append(ca_s.reshape(CONV_A_WIDTH - 1, b_s, d).transpose(1, 0, 2))

        yb_p, csx_p, csbc_p, st_p = _ssd_prompt(p, p_tail, ssm_prm, dims, t_p)
        nsc = SSM_CONV_WIDTH - 1
        outs[2].append(jnp.concatenate([csx_p[:, SUBLANES - nsc:], csbc_p[:, SUBLANES - nsc:]], axis=2))
        hg = SSM_HEADS // SSM_GROUPS
        st_g = jnp.stack([st_p[:, g * SSM_STATE:(g + 1) * SSM_STATE, g * (d_ssm // 2):(g + 1) * (d_ssm // 2)]
                          for g in range(SSM_GROUPS)], axis=1)
        outs[4].append(st_g.reshape(b_p, SSM_GROUPS, SSM_STATE, hg, SSM_HEAD_DIM)
                       .transpose(0, 1, 3, 4, 2).reshape(b_p, SSM_HEADS, SSM_HEAD_DIM, SSM_STATE))
        cs_tm = state_conv_ssm[l].transpose(1, 0, 2).reshape(nsc * b_s, d_ssm + n_bc)
        y_raw, st_s, csx_s, csbc_s = _ssd_sample(p, p_tail, cs_tm[:, :d_ssm], cs_tm[:, d_ssm:],
                                                 state_ssm3, l, ssm_prm, dims, steps)
        yb_s = _gnorm_sample(y_raw, p, ssm_prm["gn"], dims, steps)
        outs[3].append(jnp.concatenate([csx_s, csbc_s], axis=1).reshape(nsc, b_s, d_ssm + n_bc).transpose(1, 0, 2))
        outs[5].append(st_s.reshape(b_s, SSM_HEADS, SSM_HEAD_DIM, SSM_STATE))

        yc_p, k_p, v_p = _attn_prompt(p, p_tail, attn_sinks[l], tabs_p, dims, t_p)
        outs[6].append(k_p.reshape(b_p, WINDOW, KV_HEADS, HEAD_DIM))
        outs[8].append(v_p.reshape(b_p, WINDOW, KV_HEADS, HEAD_DIM))
        q_s, k_s = _rope_sample(p, p_tail, tabs_s, dims, steps)
        v_s = p_tail[t_p:, T_V:T_V + d_kv]
        k_new = jnp.concatenate([cache_k[l][:, steps:].reshape(b_s, WINDOW - steps, d_kv),
                                 k_s.reshape(steps, b_s, d_kv).transpose(1, 0, 2)], axis=1)
        v_new = jnp.concatenate([cache_v[l][:, steps:].reshape(b_s, WINDOW - steps, d_kv),
                                 v_s.reshape(steps, b_s, d_kv).transpose(1, 0, 2)], axis=1)
        outs[7].append(k_new.reshape(b_s, WINDOW, KV_HEADS, HEAD_DIM))
        outs[9].append(v_new.reshape(b_s, WINDOW, KV_HEADS, HEAD_DIM))
        q4 = from_head_order(q_s, 1).reshape(steps, b_s, KV_HEADS, Q_PER_KV, HEAD_DIM).transpose(1, 2, 3, 0, 4)
        qbd = (q4.reshape(b_s, KV_HEADS, Q_PER_KV * steps, 1, HEAD_DIM)
               * eye_kv[None, :, None, :, None]).reshape(b_s, KV_HEADS * Q_PER_KV * steps, d_kv)
        sink_col = jnp.broadcast_to(jnp.repeat(attn_sinks[l], steps)[:, None], (ATTN_HEADS * steps, LANES))
        o_s = _attn_sample(qbd, k_new, v_new, cache_k[l].reshape(b_s, WINDOW, d_kv),
                           cache_v[l].reshape(b_s, WINDOW, d_kv), sink_col, steps)
        yc_s = to_head_order(
            o_s.reshape(b_s, KV_HEADS, Q_PER_KV, steps, HEAD_DIM).transpose(3, 0, 1, 2, 4).reshape(t_s, d), 1
        ).astype(MXU_DTYPE)

        x1, h2, topi, topg = _merge(h, (ya_p, yb_p, yc_p), (ya_s, yb_s, yc_s), x_all, wts, ada_p, ada_s, dims)
        dest, cnt = _rank(topi, tm)
        dest_flat = dest[:, :TOP_K].reshape(t_all * TOP_K)
        xs = _dispatch(dest_flat, h2, min(256, tm))
        tmx = min(256, tm)
        meta = _expert_schedule(cnt[0, :n_e].astype(I32), t_all * TOP_K, tmx)
        eo = _experts(meta, xs, w_gate_up, b_gate_up, w_down, b_down, l, tmx)
        x_all = _combine(dest_flat, eo, x1, topg, ada_p, ada_s, norm_final, dims, final=(l == depth - 1))

    y_prompt = x_all[:t_p].reshape(b_p, seq, d)
    y_sample = x_all[t_p:].reshape(steps, b_s, d).transpose(1, 0, 2)
    st = [jnp.stack(o) for o in outs]
    return (y_prompt, y_sample, st[0], st[1], st[2], st[3], st[4], st[5], st[6], st[7], st[8], st[9])
```

```python
import functools
import math

import jax
import jax.numpy as jnp
from jax import lax
from jax.experimental import pallas as pl
from jax.experimental.pallas import tpu as pltpu

F32 = jnp.float32
BF16 = jnp.bfloat16
I32 = jnp.int32

PAST_LEN = 8192
SSM_HEADS = 16
SSM_HEAD_DIM = 64
SSM_GROUPS = 2
SSM_STATE = 64
SSM_CONV_WIDTH = 4
CONV_A_WIDTH = 3
SSD_CHUNK = 128
ATTN_HEADS = 16
KV_HEADS = 4
HEAD_DIM = 64
Q_PER_KV = ATTN_HEADS // KV_HEADS
WINDOW = 128
ROPE_THETA = 500000.0
ROPE_DIMS = HEAD_DIM // 4
N_EXPERTS = 32
TOP_K = 4
SWIGLU_LIMIT = 7.0
SWIGLU_ALPHA = 1.702
RMS_EPS = 1e-5
NEG_BIG = -1e30

LANES = 128
SUBLANES = 8
VMEM_LIMIT = 56 * 1024 * 1024

MXU_DTYPE = BF16
C_B, C_C, C_X, C_Z, C_XS, C_Q, N_MAIN = 0, 1024, 2048, 3072, 4096, 5120, 6144
T_BC, T_K, T_V, T_DT, N_TAIL = 0, 256, 512, 768, 1024
N_COL = N_MAIN + N_TAIL
HEAD_ORDER = tuple(h for j in range(ATTN_HEADS // 2)
                   for h in ((j // Q_PER_KV) * 2 * Q_PER_KV + j % Q_PER_KV,
                             (j // Q_PER_KV) * 2 * Q_PER_KV + Q_PER_KV + j % Q_PER_KV))


def _cp(*sem):
    return pltpu.CompilerParams(dimension_semantics=sem, vmem_limit_bytes=VMEM_LIMIT)


def _mm(a, b):
    return jnp.dot(a.astype(MXU_DTYPE), b.astype(MXU_DTYPE), preferred_element_type=F32)


def _mm_nt(a, b):
    return lax.dot_general(a.astype(MXU_DTYPE), b.astype(MXU_DTYPE), (((1,), (1,)), ((), ())),
                           preferred_element_type=F32)


def _split(v, n):
    if MXU_DTYPE == F32:
        return [v]
    parts, r = [], v
    for _ in range(n):
        p = r.astype(MXU_DTYPE)
        parts.append(p)
        r = r - p.astype(F32)
    return parts


def _mm_sel(v, sel, n=3):
    acc = None
    for p in _split(v, n):
        t = jnp.dot(p, sel, preferred_element_type=F32)
        acc = t if acc is None else acc + t
    return acc


def _sel_mm(sel, v, n=3):
    acc = None
    for p in _split(v, n):
        t = jnp.dot(sel, p, preferred_element_type=F32)
        acc = t if acc is None else acc + t
    return acc


def _rms(x, w):
    return x * lax.rsqrt(jnp.mean(x * x, axis=-1, keepdims=True) + RMS_EPS) * w


def _silu(x):
    return x * (1.0 / (1.0 + jnp.exp(-x)))


def _sigmoid(x):
    return 1.0 / (1.0 + jnp.exp(-x))


def _softplus(x):
    return jnp.maximum(x, 0.0) + jnp.log(1.0 + jnp.exp(-jnp.abs(x)))


def _tile_rows(x, n):
    return x if n == 1 else jnp.concatenate([x] * n, axis=0)


def _tile_lanes(x, n):
    return x if n == 1 else jnp.concatenate([x] * n, axis=1)


def _iota(shape, dim):
    return lax.broadcasted_iota(I32, shape, dim)


def _half_mask(lane, upper):
    return lane >= HEAD_DIM if upper else lane < HEAD_DIM


def _ada_kernel(c_ref, w_ref, b_ref, o_ref):
    a = _silu(c_ref[...])
    o_ref[...] = _mm(a, w_ref[...]) + b_ref[...]


def _ada_all(c_all, w_ada, b_ada):
    depth, d, n6 = w_ada.shape
    nb = c_all.shape[0]
    tn = d
    return pl.pallas_call(
        _ada_kernel,
        grid=(depth, n6 // tn),
        in_specs=[pl.BlockSpec((nb, d), lambda l, j: (0, 0)),
                  pl.BlockSpec((None, d, tn), lambda l, j: (l, 0, j)),
                  pl.BlockSpec((None, 1, tn), lambda l, j: (l, 0, j))],
        out_specs=pl.BlockSpec((None, nb, tn), lambda l, j: (l, 0, j)),
        out_shape=jax.ShapeDtypeStruct((depth, nb, n6), F32),
        compiler_params=_cp("arbitrary", "arbitrary"),
        name="ada",
    )(c_all, w_ada, b_ada.reshape(depth, 1, n6))


def _inproj_kernel(x_ref, nw_ref, shp_ref, scp_ref, shs_ref, scs_ref, w_ref, pm_ref, pt_ref, h_ref, *,
                   n_p, reps, n_main):
    i = pl.program_id(0)
    j = pl.program_id(1)

    def make_h(sh, sc):
        y = _rms(x_ref[...], nw_ref[...])
        h_ref[...] = (y * (1.0 + sc) + sh).astype(h_ref.dtype)

    @pl.when((j == 0) & (i < n_p))
    def _():
        make_h(shp_ref[...], scp_ref[...])

    @pl.when((j == 0) & (i >= n_p))
    def _():
        make_h(_tile_rows(shs_ref[...], reps), _tile_rows(scs_ref[...], reps))

    res = jnp.dot(h_ref[...], w_ref[...], preferred_element_type=F32)

    @pl.when(j < n_main)
    def _():
        pm_ref[...] = res.astype(pm_ref.dtype)

    @pl.when(j == n_main)
    def _():
        pt_ref[...] = res


def _mod_specs(dims, col, grid_rank):
    d, tm, seq, b_p, b_s = dims

    def pidx(i, *_):
        return (jnp.minimum(i * tm // seq, b_p - 1), 0, col)

    def sidx(i, *_):
        return (0, col)

    return pl.BlockSpec((None, 1, d), pidx), pl.BlockSpec((b_s, d), sidx)


def _inproj(x_all, norm_w, ada_p, ada_s, w_pack, layer, dims):
    d, tm, seq, b_p, b_s = dims
    tn = N_TAIL
    t_all = x_all.shape[0]
    n_t = t_all // tm
    n_p = b_p * seq // tm
    n_main = N_MAIN // tn
    shp, shs = _mod_specs(dims, 0, 2)
    scp, scs = _mod_specs(dims, 1, 2)
    kern = functools.partial(_inproj_kernel, n_p=n_p, reps=tm // b_s, n_main=n_main)
    return pl.pallas_call(
        kern,
        grid=(n_t, n_main + 1),
        in_specs=[pl.BlockSpec((tm, d), lambda i, j: (i, 0)),
                  pl.BlockSpec((1, d), lambda i, j: (0, 0)),
                  shp, scp, shs, scs,
                  pl.BlockSpec((None, d, tn), lambda i, j: (layer, 0, j))],
        out_specs=[pl.BlockSpec((tm, tn), lambda i, j: (i, jnp.minimum(j, n_main - 1))),
                   pl.BlockSpec((tm, tn), lambda i, j: (i, 0)),
                   pl.BlockSpec((tm, d), lambda i, j: (i, 0))],
        out_shape=[jax.ShapeDtypeStruct((t_all, N_MAIN), MXU_DTYPE),
                   jax.ShapeDtypeStruct((t_all, N_TAIL), F32),
                   jax.ShapeDtypeStruct((t_all, d), MXU_DTYPE)],
        compiler_params=_cp("arbitrary", "arbitrary"),
        name="inproj",
    )(x_all, norm_w.reshape(1, d), ada_p, ada_p, ada_s, ada_s, w_pack)


def _tokens_per_step(tm, n_j):
    per = -(-tm // n_j)
    return per + per % 2


def _inproj_c_kernel(dcur_ref, dnxt_ref, eo_ref, x1_ref, tg_ref, g2p_ref, g2s_ref,
                     nw_ref, shp_ref, scp_ref, shs_ref, scs_ref, w_ref,
                     pm_ref, pt_ref, h_ref, xn_ref, gbuf, sem, *, n_p, reps, n_main):
    i = pl.program_id(0)
    j = pl.program_id(1)
    n_t = pl.num_programs(0)
    n_j = n_main + 1
    tm = x1_ref.shape[0]
    per = _tokens_per_step(tm, n_j)
    surplus = (per * n_j - tm) * TOP_K

    def issue(d_ref, buf, tok0):
        for u in range(per):
            t = tok0 + u
            ts = jnp.minimum(t, tm - 1)
            for k in range(TOP_K):
                pltpu.make_async_copy(eo_ref.at[pl.ds(d_ref[ts * TOP_K + k], 1)],
                                      gbuf.at[buf, k, pl.ds(t, 1)], sem.at[buf]).start()

    def wait(buf):
        for k in range(TOP_K):
            pltpu.make_async_copy(eo_ref.at[pl.ds(0, tm)], gbuf.at[buf, k, pl.ds(0, tm)], sem.at[buf]).wait()
        if surplus:
            pltpu.make_async_copy(eo_ref.at[pl.ds(0, surplus)], gbuf.at[buf, 0, pl.ds(0, surplus)], sem.at[buf]).wait()

    @pl.when((i == 0) & (j == 0))
    def _():
        def body(jj, carry):
            issue(dcur_ref, 0, jj * per)
            return carry
        lax.fori_loop(0, n_j, body, 0)

    nxt = (i + 1) % 2

    def make(g2, sh, sc):
        cur = i % 2
        wait(cur)
        tg = tg_ref[...]
        y = tg[:, 0:1] * gbuf[cur, 0, pl.ds(0, tm), :]
        for k in range(1, TOP_K):
            y = y + tg[:, k:k + 1] * gbuf[cur, k, pl.ds(0, tm), :]
        x = x1_ref[...] + g2 * y
        xn_ref[...] = x
        h_ref[...] = (_rms(x, nw_ref[...]) * (1.0 + sc) + sh).astype(h_ref.dtype)

    @pl.when((j == 0) & (i < n_p))
    def _():
        make(g2p_ref[...], shp_ref[...], scp_ref[...])

    @pl.when((j == 0) & (i >= n_p))
    def _():
        make(_tile_rows(g2s_ref[...], reps), _tile_rows(shs_ref[...], reps), _tile_rows(scs_ref[...], reps))

    issue(dnxt_ref, nxt, j * per)
    res = jnp.dot(h_ref[...], w_ref[...], preferred_element_type=F32)

    @pl.when(j < n_main)
    def _():
        pm_ref[...] = res.astype(pm_ref.dtype)

    @pl.when(j == n_main)
    def _():
        pt_ref[...] = res

    @pl.when((i == n_t - 1) & (j == n_j - 1))
    def _():
        wait(nxt)


def _inproj_combine(dest_flat, eo, x1, topg, ada_prev_p, ada_prev_s, norm_w, ada_p, ada_s, w_pack, layer, dims):
    d, tm, seq, b_p, b_s = dims
    tn = N_TAIL
    t_all = x1.shape[0]
    n_t = t_all // tm
    n_p = b_p * seq // tm
    n_main = N_MAIN // tn
    n_j = n_main + 1
    buf_rows = -(-(_tokens_per_step(tm, n_j) * n_j) // SUBLANES) * SUBLANES
    g2p, g2s = _mod_specs(dims, 5, 2)
    shp, shs = _mod_specs(dims, 0, 2)
    scp, scs = _mod_specs(dims, 1, 2)
    kern = functools.partial(_inproj_c_kernel, n_p=n_p, reps=tm // b_s, n_main=n_main)
    row = lambda n: pl.BlockSpec((tm, n), lambda i, j: (i, 0))
    return pl.pallas_call(
        kern,
        grid=(n_t, n_j),
        in_specs=[pl.BlockSpec((tm * TOP_K,), lambda i, j: (i,), memory_space=pltpu.SMEM),
                  pl.BlockSpec((tm * TOP_K,), lambda i, j: (jnp.minimum(i + 1, n_t - 1),), memory_space=pltpu.SMEM),
                  pl.BlockSpec(memory_space=pl.ANY),
                  row(d), row(LANES), g2p, g2s,
                  pl.BlockSpec((1, d), lambda i, j: (0, 0)),
                  shp, scp, shs, scs,
                  pl.BlockSpec((None, d, tn), lambda i, j: (layer, 0, j))],
        out_specs=[pl.BlockSpec((tm, tn), lambda i, j: (i, jnp.minimum(j, n_main - 1))),
                   pl.BlockSpec((tm, tn), lambda i, j: (i, 0)),
                   row(d), row(d)],
        out_shape=[jax.ShapeDtypeStruct((t_all, N_MAIN), MXU_DTYPE),
                   jax.ShapeDtypeStruct((t_all, N_TAIL), F32),
                   jax.ShapeDtypeStruct((t_all, d), MXU_DTYPE),
                   jax.ShapeDtypeStruct((t_all, d), F32)],
        scratch_shapes=[pltpu.VMEM((2, TOP_K, buf_rows, d), F32), pltpu.SemaphoreType.DMA((2,))],
        compiler_params=_cp("arbitrary", "arbitrary"),
        name="inproj_combine",
    )(dest_flat, dest_flat, eo, x1, topg, ada_prev_p, ada_prev_s, norm_w.reshape(1, d),
      ada_p, ada_p, ada_s, ada_s, w_pack)


def _conva_p_kernel(b_ref, c_ref, x_ref, w_ref, y_ref, st_ref, buf):
    seq = b_ref.shape[0]
    u = c_ref[...].astype(F32) * x_ref[...].astype(F32)
    buf[0:SUBLANES, :] = jnp.zeros((SUBLANES, buf.shape[1]), F32)
    buf[SUBLANES:, :] = u
    w = w_ref[...]
    acc = w[2:3, :] * u
    acc = acc + w[1:2, :] * buf[pl.ds(SUBLANES - 1, seq), :]
    acc = acc + w[0:1, :] * buf[pl.ds(SUBLANES - 2, seq), :]
    y_ref[...] = (b_ref[...].astype(F32) * acc).astype(y_ref.dtype)
    st_ref[...] = buf[pl.ds(seq, SUBLANES), :]


def _conva_prompt(p, conv_w, dims, t_all, tc=256):
    d, tm, seq, b_p, b_s = dims
    nc = d // tc
    return pl.pallas_call(
        _conva_p_kernel,
        grid=(b_p, nc),
        in_specs=[pl.BlockSpec((seq, tc), lambda b, c: (b, C_B // tc + c)),
                  pl.BlockSpec((seq, tc), lambda b, c: (b, C_C // tc + c)),
                  pl.BlockSpec((seq, tc), lambda b, c: (b, C_X // tc + c)),
                  pl.BlockSpec((CONV_A_WIDTH, tc), lambda b, c: (0, c))],
        out_specs=[pl.BlockSpec((seq, tc), lambda b, c: (b, c)),
                   pl.BlockSpec((None, SUBLANES, tc), lambda b, c: (b, 0, c))],
        out_shape=[jax.ShapeDtypeStruct((t_all, d), MXU_DTYPE),
                   jax.ShapeDtypeStruct((b_p, SUBLANES, d), F32)],
        scratch_shapes=[pltpu.VMEM((seq + SUBLANES, tc), F32)],
        compiler_params=_cp("arbitrary", "arbitrary"),
        name="conva_prompt",
    )(p, p, p, conv_w)


def _conva_s_kernel(b_ref, c_ref, x_ref, s_ref, w_ref, y_ref, st_ref, *, b_s, steps):
    ext = jnp.concatenate([s_ref[...], c_ref[...].astype(F32) * x_ref[...].astype(F32)], axis=0)
    w = w_ref[...]
    acc = None
    for k in range(CONV_A_WIDTH):
        t = w[k:k + 1, :] * ext[k * b_s:(k + steps) * b_s]
        acc = t if acc is None else acc + t
    y_ref[...] = (b_ref[...].astype(F32) * acc).astype(y_ref.dtype)
    st_ref[...] = ext[steps * b_s:]


def _conva_sample(p, state_tm, conv_w, dims, steps, tc=256):
    d, tm, seq, b_p, b_s = dims
    t_s = b_s * steps
    rb = (b_p * seq) // t_s
    ns = (CONV_A_WIDTH - 1) * b_s
    kern = functools.partial(_conva_s_kernel, b_s=b_s, steps=steps)
    return pl.pallas_call(
        kern,
        grid=(d // tc,),
        in_specs=[pl.BlockSpec((t_s, tc), lambda c: (rb, C_B // tc + c)),
                  pl.BlockSpec((t_s, tc), lambda c: (rb, C_C // tc + c)),
                  pl.BlockSpec((t_s, tc), lambda c: (rb, C_X // tc + c)),
                  pl.BlockSpec((ns, tc), lambda c: (0, c)),
                  pl.BlockSpec((CONV_A_WIDTH, tc), lambda c: (0, c))],
        out_specs=[pl.BlockSpec((t_s, tc), lambda c: (0, c)),
                   pl.BlockSpec((ns, tc), lambda c: (0, c))],
        out_shape=[jax.ShapeDtypeStruct((t_s, d), MXU_DTYPE),
                   jax.ShapeDtypeStruct((ns, d), F32)],
        compiler_params=_cp("arbitrary"),
        name="conva_sample",
    )(p, p, p, state_tm, conv_w)


def _ssd_p_kernel(z_ref, xs_ref, bc_ref, dt_ref, wx_ref, wbc_ref, bx_ref, bbc_ref, dtb_ref, alog_ref,
                  de_ref, gn_ref, rexp_ref,
                  y_ref, csx_ref, csbc_ref, stout_ref,
                  xbuf, bcbuf, st, ybuf):
    c = pl.program_id(1)
    q = xs_ref.shape[0]
    d_ssm = xs_ref.shape[1]
    half = d_ssm // SSM_GROUPS

    @pl.when(c == 0)
    def _():
        xbuf[0:SUBLANES, :] = jnp.zeros((SUBLANES, xbuf.shape[1]), F32)
        bcbuf[0:SUBLANES, :] = jnp.zeros((SUBLANES, bcbuf.shape[1]), F32)
        st[...] = jnp.zeros(st.shape, F32)

    xbuf[SUBLANES:, :] = xs_ref[...].astype(F32)
    bcbuf[SUBLANES:, :] = bc_ref[...]

    def conv(buf, w, b):
        acc = b
        for k in range(SSM_CONV_WIDTH):
            acc = acc + w[k:k + 1, :] * buf[pl.ds(SUBLANES - (SSM_CONV_WIDTH - 1) + k, q), :]
        return _silu(acc)

    xa = conv(xbuf, wx_ref[...], bx_ref[...])
    bca = conv(bcbuf, wbc_ref[...], bbc_ref[...])
    csx_ref[...] = xbuf[pl.ds(q, SUBLANES), :]
    csbc_ref[...] = bcbuf[pl.ds(q, SUBLANES), :]
    xbuf[0:SUBLANES, :] = xbuf[pl.ds(q, SUBLANES), :]
    bcbuf[0:SUBLANES, :] = bcbuf[pl.ds(q, SUBLANES), :]

    dtv = _softplus(dt_ref[:, 0:LANES] + dtb_ref[...])
    a = -jnp.exp(alog_ref[...])
    da = dtv * a
    row = _iota((q, q), 0)
    col = _iota((q, q), 1)
    causal = row >= col
    tri = jnp.where(causal, 1.0, 0.0).astype(MXU_DTYPE)
    acs = _sel_mm(tri, da)
    acs_t = acs.T
    rexp = rexp_ref[...]
    acs_e = _mm_sel(acs, rexp)
    dt_e = _mm_sel(dtv, rexp)
    last_e = acs_e[q - 1:q, :]
    exp_e = jnp.exp(acs_e)
    w_e = jnp.exp(last_e - acs_e)
    dec_last = jnp.exp(last_e)

    xdt = xa * dt_e
    xdt_w = xdt * w_e
    bslab = bca[:, 0:LANES]
    cslab = bca[:, LANES:2 * LANES]
    bt = bslab.T
    st_old = st[...]
    y_off = _mm(cslab, st_old) * exp_e
    upd = _mm(bt, xdt_w)
    srow = _iota(st.shape, 0)
    scol = _iota(st.shape, 1)
    diag = (srow >= SSM_STATE) == (scol >= half)
    st[...] = dec_last * st_old + jnp.where(diag, upd, 0.0)

    lane = _iota((q, LANES), 1)
    cbs = []
    for g in range(SSM_GROUPS):
        cm = jnp.where(_half_mask(lane, g == 1), cslab, 0.0)
        cbs.append(_mm(cm, bt))
    heads_per_group = SSM_HEADS // SSM_GROUPS
    for jp in range(SSM_HEADS // 2):
        ms = []
        for hh in (2 * jp, 2 * jp + 1):
            seg = acs[:, hh:hh + 1] - acs_t[hh:hh + 1, :]
            ldec = jnp.exp(jnp.where(causal, seg, NEG_BIG))
            ms.append(cbs[hh // heads_per_group] * ldec)
        lhs = jnp.concatenate(ms, axis=1)
        slab = xdt[:, jp * LANES:(jp + 1) * LANES]
        rhs = jnp.concatenate([jnp.where(lane < SSM_HEAD_DIM, slab, 0.0),
                               jnp.where(lane >= SSM_HEAD_DIM, slab, 0.0)], axis=0)
        ybuf[:, jp * LANES:(jp + 1) * LANES] = _mm(lhs, rhs)

    y = ybuf[...] + y_off + de_ref[...] * xa
    u = y * _silu(z_ref[...].astype(F32))
    outs = []
    for g in range(SSM_GROUPS):
        ug = u[:, g * half:(g + 1) * half]
        outs.append(ug * lax.rsqrt(jnp.mean(ug * ug, axis=-1, keepdims=True) + RMS_EPS))
    y_ref[...] = (jnp.concatenate(outs, axis=1) * gn_ref[...]).astype(y_ref.dtype)

    @pl.when(c == pl.num_programs(1) - 1)
    def _():
        stout_ref[...] = st[...]


def _ssd_prompt(pm, pt, prm, dims, t_all):
    d, tm, seq, b_p, b_s = dims
    q = min(SSD_CHUNK, seq)
    nc = seq // q
    n_bc = 2 * SSM_GROUPS * SSM_STATE
    full = lambda shape: pl.BlockSpec(shape, lambda b, c: tuple(0 for _ in shape))
    return pl.pallas_call(
        _ssd_p_kernel,
        grid=(b_p, nc),
        in_specs=[pl.BlockSpec((q, d), lambda b, c: (b * nc + c, C_Z // d)),
                  pl.BlockSpec((q, d), lambda b, c: (b * nc + c, C_XS // d)),
                  pl.BlockSpec((q, n_bc), lambda b, c: (b * nc + c, T_BC // n_bc)),
                  pl.BlockSpec((q, n_bc), lambda b, c: (b * nc + c, T_DT // n_bc)),
                  full((SSM_CONV_WIDTH, d)), full((SSM_CONV_WIDTH, n_bc)),
                  full((1, d)), full((1, n_bc)), full((1, LANES)), full((1, LANES)),
                  full((1, d)), full((1, d)), full((LANES, d))],
        out_specs=[pl.BlockSpec((q, d), lambda b, c: (b * nc + c, 0)),
                   pl.BlockSpec((None, SUBLANES, d), lambda b, c: (b, 0, 0)),
                   pl.BlockSpec((None, SUBLANES, n_bc), lambda b, c: (b, 0, 0)),
                   pl.BlockSpec((None, 2 * SSM_STATE, d), lambda b, c: (b, 0, 0))],
        out_shape=[jax.ShapeDtypeStruct((t_all, d), MXU_DTYPE),
                   jax.ShapeDtypeStruct((b_p, SUBLANES, d), F32),
                   jax.ShapeDtypeStruct((b_p, SUBLANES, n_bc), F32),
                   jax.ShapeDtypeStruct((b_p, 2 * SSM_STATE, d), F32)],
        scratch_shapes=[pltpu.VMEM((q + SUBLANES, d), F32),
                        pltpu.VMEM((q + SUBLANES, n_bc), F32),
                        pltpu.VMEM((2 * SSM_STATE, d), F32),
                        pltpu.VMEM((q, d), F32)],
        compiler_params=_cp("arbitrary", "arbitrary"),
        name="ssd_prompt",
    )(pm, pm, pt, pt, prm["wx"], prm["wbc"], prm["bx"], prm["bbc"], prm["dtb"], prm["alog"],
      prm["de"], prm["gn"], prm["rexp"])


def _ssd_s_kernel(xs_ref, bc_ref, dt_ref, cx_ref, cbc_ref, wx_ref, wbc_ref, bx_ref, bbc_ref, dtb_ref,
                  alog_ref, de_ref, e2_ref, g2_ref, s_ref,
                  y_ref, snew_ref, csx_ref, csbc_ref, *, b_s, steps):
    j = pl.program_id(0)
    pn = SSM_HEAD_DIM * SSM_STATE
    extx = jnp.concatenate([cx_ref[...], xs_ref[...].astype(F32)], axis=0)
    extbc = jnp.concatenate([cbc_ref[...], bc_ref[...]], axis=0)

    def conv(ext, w, b):
        acc = b
        for k in range(SSM_CONV_WIDTH):
            acc = acc + w[k:k + 1, :] * ext[k * b_s:(k + steps) * b_s]
        return _silu(acc)

    xa = conv(extx, wx_ref[...], bx_ref[...])
    bca = conv(extbc, wbc_ref[...], bbc_ref[...])
    csx_ref[...] = extx[steps * b_s:]
    csbc_ref[...] = extbc[steps * b_s:]
    dtv = _softplus(dt_ref[:, 0:LANES] + dtb_ref[...])
    dec = jnp.exp(dtv * (-jnp.exp(alog_ref[...])))
    lane = _iota(dtv.shape, 1)
    grp = (2 * j) // (SSM_HEADS // SSM_GROUPS)
    own = (lane // SSM_STATE) == grp
    bslab = bca[:, 0:LANES]
    cslab = bca[:, LANES:2 * LANES]
    b2 = jnp.where(own, bslab, pltpu.roll(bslab, SSM_STATE, 1))
    c2 = jnp.where(own, cslab, pltpu.roll(cslab, SSM_STATE, 1))
    ys = [None] * steps
    for lh in range(2):
        hh = 2 * j + lh
        dt_col = jnp.sum(jnp.where(lane == hh, dtv, 0.0), axis=1, keepdims=True)
        dec_col = jnp.sum(jnp.where(lane == hh, dec, 0.0), axis=1, keepdims=True)
        s = s_ref[:, lh * pn:(lh + 1) * pn]
        for t in range(steps):
            rows = slice(t * b_s, (t + 1) * b_s)
            xe = _mm_sel(xa[rows] * dt_col[rows], e2_ref[lh])
            s = dec_col[rows] * s + xe * _tile_lanes(b2[rows], pn // LANES)
            yt = _mm(s * _tile_lanes(c2[rows], pn // LANES), g2_ref[lh])
            ys[t] = yt if ys[t] is None else ys[t] + yt
        snew_ref[:, lh * pn:(lh + 1) * pn] = s
    y_ref[...] = jnp.concatenate(ys, axis=0) + de_ref[...] * xa


def _ssd_sample(pm, pt, cs_x, cs_bc, state3d, layer, prm, dims, steps):
    d, tm, seq, b_p, b_s = dims
    t_s = b_s * steps
    rb = (b_p * seq) // t_s
    n_bc = 2 * SSM_GROUPS * SSM_STATE
    pn = SSM_HEAD_DIM * SSM_STATE
    ns = (SSM_CONV_WIDTH - 1) * b_s
    kern = functools.partial(_ssd_s_kernel, b_s=b_s, steps=steps)
    full = lambda shape: pl.BlockSpec(shape, lambda j: tuple(0 for _ in shape))
    return pl.pallas_call(
        kern,
        grid=(SSM_HEADS // 2,),
        in_specs=[pl.BlockSpec((t_s, LANES), lambda j: (rb, C_XS // LANES + j)),
                  pl.BlockSpec((t_s, n_bc), lambda j: (rb, T_BC // n_bc)),
                  pl.BlockSpec((t_s, n_bc), lambda j: (rb, T_DT // n_bc)),
                  pl.BlockSpec((ns, LANES), lambda j: (0, j)),
                  full((ns, n_bc)),
                  pl.BlockSpec((SSM_CONV_WIDTH, LANES), lambda j: (0, j)),
                  full((SSM_CONV_WIDTH, n_bc)),
                  pl.BlockSpec((1, LANES), lambda j: (0, j)),
                  full((1, n_bc)), full((1, LANES)), full((1, LANES)),
                  pl.BlockSpec((1, LANES), lambda j: (0, j)),
                  full((2, LANES, pn)), full((2, pn, LANES)),
                  pl.BlockSpec((None, b_s, 2 * pn), lambda j: (layer, 0, j))],
        out_specs=[pl.BlockSpec((t_s, LANES), lambda j: (0, j)),
                   pl.BlockSpec((b_s, 2 * pn), lambda j: (0, j)),
                   pl.BlockSpec((ns, LANES), lambda j: (0, j)),
                   full((ns, n_bc))],
        out_shape=[jax.ShapeDtypeStruct((t_s, d), F32),
                   jax.ShapeDtypeStruct(state3d.shape[1:], F32),
                   jax.ShapeDtypeStruct((ns, d), F32),
                   jax.ShapeDtypeStruct((ns, n_bc), F32)],
        compiler_params=_cp("arbitrary"),
        name="ssd_sample",
    )(pm, pt, pt, cs_x, cs_bc, prm["wx"], prm["wbc"], prm["bx"], prm["bbc"], prm["dtb"], prm["alog"],
      prm["de"], prm["e2"], prm["g2"], state3d)


def _gnorm_s_kernel(y_ref, z_ref, gn_ref, o_ref):
    u = y_ref[...] * _silu(z_ref[...].astype(F32))
    half = u.shape[1] // SSM_GROUPS
    outs = []
    for g in range(SSM_GROUPS):
        ug = u[:, g * half:(g + 1) * half]
        outs.append(ug * lax.rsqrt(jnp.mean(ug * ug, axis=-1, keepdims=True) + RMS_EPS))
    o_ref[...] = (jnp.concatenate(outs, axis=1) * gn_ref[...]).astype(o_ref.dtype)


def _gnorm_sample(y_raw, p, gn, dims, steps):
    d, tm, seq, b_p, b_s = dims
    t_s = b_s * steps
    rb = (b_p * seq) // t_s
    return pl.pallas_call(
        _gnorm_s_kernel,
        grid=(1,),
        in_specs=[pl.BlockSpec((t_s, d), lambda i: (0, 0)),
                  pl.BlockSpec((t_s, d), lambda i: (rb, C_Z // d)),
                  pl.BlockSpec((1, d), lambda i: (0, 0))],
        out_specs=pl.BlockSpec((t_s, d), lambda i: (0, 0)),
        out_shape=jax.ShapeDtypeStruct((t_s, d), MXU_DTYPE),
        compiler_params=_cp("arbitrary"),
        name="gnorm_sample",
    )(y_raw, p, gn)


def _rope(x, cos_f, sin_a, sin_b):
    n = x.shape[1] // cos_f.shape[1]
    half = ROPE_DIMS // 2
    return (x * _tile_lanes(cos_f, n) + pltpu.roll(x, half, 1) * _tile_lanes(sin_a, n)
            + pltpu.roll(x, x.shape[1] - half, 1) * _tile_lanes(sin_b, n))


def _attn_p_kernel(sink_ref, q_ref, k_ref, v_ref, cos_ref, sa_ref, sb_ref,
                   o_ref, kn_ref, vn_ref, kprev, vprev):
    c = pl.program_id(1)
    w = q_ref.shape[0]
    cos_f, sin_a, sin_b = cos_ref[...], sa_ref[...], sb_ref[...]
    scale = HEAD_DIM ** -0.5
    k = _rope(k_ref[...], cos_f, sin_a, sin_b)
    v = v_ref[...]

    @pl.when(c == 0)
    def _():
        kprev[...] = jnp.zeros(kprev.shape, F32)
        vprev[...] = jnp.zeros(vprev.shape, F32)

    kk = jnp.concatenate([kprev[...], k], axis=0)
    vv = jnp.concatenate([vprev[...], v], axis=0)
    key = _iota((2 * w, w), 0)
    qry = _iota((2 * w, w), 1)
    valid = (key > qry) & (key <= qry + w) & ((c > 0) | (key >= w))
    valid = _tile_lanes(valid.astype(F32), Q_PER_KV) > 0.5
    lane_k = _iota((2 * w, LANES), 1)
    sub_v = _iota((LANES, 2 * w), 0)
    for a in range(KV_HEADS // 2):
        slabs = range(a * Q_PER_KV, (a + 1) * Q_PER_KV)
        qst = jnp.concatenate(
            [(_rope(q_ref[:, LANES * j:LANES * (j + 1)].astype(F32), cos_f, sin_a, sin_b) * scale).astype(MXU_DTYPE)
             for j in slabs], axis=0)
        ksl = kk[:, LANES * a:LANES * (a + 1)]
        vt = vv[:, LANES * a:LANES * (a + 1)].T
        o_t = None
        for hk in range(2):
            ks = jnp.where(_half_mask(lane_k, hk == 1), ksl, 0.0).astype(MXU_DTYPE)
            vs = jnp.where(_half_mask(sub_v, hk == 1), vt, 0.0).astype(MXU_DTYPE)
            sink = jnp.concatenate([jnp.full((1, w), sink_ref[HEAD_ORDER[2 * j + hk]], F32) for j in slabs], axis=1)
            s = lax.dot_general(ks, qst, (((1,), (1,)), ((), ())), preferred_element_type=F32)
            s = jnp.where(valid, s, NEG_BIG)
            m = jnp.maximum(jnp.max(s, axis=0, keepdims=True), sink)
            pr = jnp.exp(s - m)
            den = jnp.sum(pr, axis=0, keepdims=True) + jnp.exp(sink - m)
            pn = (pr * (1.0 / den)).astype(MXU_DTYPE)
            o = jnp.dot(vs, pn, preferred_element_type=F32)
            o_t = o if o_t is None else o_t + o
        o_all = o_t.T
        for g, j in enumerate(slabs):
            o_ref[:, LANES * j:LANES * (j + 1)] = o_all[g * w:(g + 1) * w].astype(o_ref.dtype)
    kprev[...] = k
    vprev[...] = v
    kn_ref[...] = k
    vn_ref[...] = v


def _attn_prompt(pm, pt, sinks, tabs, dims, t_all):
    d, tm, seq, b_p, b_s = dims
    w = WINDOW
    nb = seq // w
    d_kv = KV_HEADS * HEAD_DIM
    tab_spec = pl.BlockSpec((w, LANES), lambda b, c: (c, 0))
    const = lambda shape: pl.BlockSpec(shape, lambda b, c: tuple(0 for _ in shape))
    return pl.pallas_call(
        _attn_p_kernel,
        grid=(b_p, nb),
        in_specs=[pl.BlockSpec(memory_space=pltpu.SMEM),
                  pl.BlockSpec((w, d), lambda b, c: (b * nb + c, C_Q // d)),
                  pl.BlockSpec((w, d_kv), lambda b, c: (b * nb + c, T_K // d_kv)),
                  pl.BlockSpec((w, d_kv), lambda b, c: (b * nb + c, T_V // d_kv)),
                  tab_spec, tab_spec, tab_spec],
        out_specs=[pl.BlockSpec((w, d), lambda b, c: (b * nb + c, 0)),
                   pl.BlockSpec((None, w, d_kv), lambda b, c: (b, 0, 0)),
                   pl.BlockSpec((None, w, d_kv), lambda b, c: (b, 0, 0))],
        out_shape=[jax.ShapeDtypeStruct((t_all, d), MXU_DTYPE),
                   jax.ShapeDtypeStruct((b_p, w, d_kv), F32),
                   jax.ShapeDtypeStruct((b_p, w, d_kv), F32)],
        scratch_shapes=[pltpu.VMEM((w, d_kv), F32), pltpu.VMEM((w, d_kv), F32)],
        compiler_params=_cp("arbitrary", "arbitrary"),
        name="attn_prompt",
    )(sinks, pm, pt, pt, *tabs)


def _rope_s_kernel(q_ref, k_ref, cos_ref, sa_ref, sb_ref, qo_ref, ko_ref):
    cos_f, sin_a, sin_b = cos_ref[...], sa_ref[...], sb_ref[...]
    qo_ref[...] = _rope(q_ref[...].astype(F32), cos_f, sin_a, sin_b)
    ko_ref[...] = _rope(k_ref[...], cos_f, sin_a, sin_b)


def _rope_sample(pm, pt, tabs, dims, steps):
    d, tm, seq, b_p, b_s = dims
    t_s = b_s * steps
    rb = (b_p * seq) // t_s
    d_kv = KV_HEADS * HEAD_DIM
    tab_spec = pl.BlockSpec((t_s, LANES), lambda i: (0, 0))
    return pl.pallas_call(
        _rope_s_kernel,
        grid=(1,),
        in_specs=[pl.BlockSpec((t_s, d), lambda i: (rb, C_Q // d)),
                  pl.BlockSpec((t_s, d_kv), lambda i: (rb, T_K // d_kv)),
                  tab_spec, tab_spec, tab_spec],
        out_specs=[pl.BlockSpec((t_s, d), lambda i: (0, 0)),
                   pl.BlockSpec((t_s, d_kv), lambda i: (0, 0))],
        out_shape=[jax.ShapeDtypeStruct((t_s, d), F32),
                   jax.ShapeDtypeStruct((t_s, d_kv), F32)],
        compiler_params=_cp("arbitrary"),
        name="rope_sample",
    )(pm, pt, *tabs)


def _attn_s_kernel(q_ref, kn_ref, vn_ref, ko_ref, vo_ref, sink_ref, valid_ref, o_ref):
    tb, nq, d_kv = q_ref.shape
    w = kn_ref.shape[1]
    n_old = ko_ref.shape[1]
    pad = jnp.zeros((w - n_old, d_kv), F32)
    valid = valid_ref[...] > 0.5
    sink = sink_ref[:, 0:1]
    scale = HEAD_DIM ** -0.5
    orow = _iota((nq, HEAD_DIM), 0)
    rows_per_kv = nq // KV_HEADS
    for b in range(tb):
        kk = jnp.concatenate([kn_ref[b], ko_ref[b], pad], axis=0)
        vv = jnp.concatenate([vn_ref[b], vo_ref[b], pad], axis=0)
        s = _mm_nt(q_ref[b], kk) * scale
        s = jnp.where(valid, s, NEG_BIG)
        m = jnp.maximum(jnp.max(s, axis=-1, keepdims=True), sink)
        pr = jnp.exp(s - m)
        den = jnp.sum(pr, axis=-1, keepdims=True) + jnp.exp(sink - m)
        o = _mm(pr, vv) / den
        acc = jnp.zeros((nq, HEAD_DIM), F32)
        for kh in range(KV_HEADS):
            mine = (orow >= kh * rows_per_kv) & (orow < (kh + 1) * rows_per_kv)
            acc = acc + jnp.where(mine, o[:, kh * HEAD_DIM:(kh + 1) * HEAD_DIM], 0.0)
        o_ref[b] = acc


def _attn_sample(qbd, k_new, v_new, k_old, v_old, sink_col, steps, tb=8):
    b_s, nq, d_kv = qbd.shape
    w = k_new.shape[1]
    t = (jnp.arange(nq) % steps)[:, None]
    col = jnp.arange(2 * w)[None, :]
    valid = jnp.where(col < w, col <= (w - steps) + t, (col - w > t) & (col - w < steps)).astype(F32)
    return pl.pallas_call(
        _attn_s_kernel,
        grid=(b_s // tb,),
        in_specs=[pl.BlockSpec((tb, nq, d_kv), lambda i: (i, 0, 0)),
                  pl.BlockSpec((tb, w, d_kv), lambda i: (i, 0, 0)),
                  pl.BlockSpec((tb, w, d_kv), lambda i: (i, 0, 0)),
                  pl.BlockSpec((tb, SUBLANES, d_kv), lambda i: (i, 0, 0)),
                  pl.BlockSpec((tb, SUBLANES, d_kv), lambda i: (i, 0, 0)),
                  pl.BlockSpec((nq, LANES), lambda i: (0, 0)),
                  pl.BlockSpec((nq, 2 * w), lambda i: (0, 0))],
        out_specs=pl.BlockSpec((tb, nq, HEAD_DIM), lambda i: (i, 0, 0)),
        out_shape=jax.ShapeDtypeStruct((b_s, nq, HEAD_DIM), F32),
        compiler_params=_cp("arbitrary"),
        name="attn_sample",
    )(qbd, k_new, v_new, k_old, v_old, sink_col, valid)


def _merge_kernel(h_ref, yap_ref, ybp_ref, ycp_ref, yas_ref, ybs_ref, ycs_ref, x_ref,
                  wg_ref, bg_ref, pa_ref, pb_ref, pc_ref, wo_ref,
                  g1p_ref, g1s_ref, nw_ref, shp_ref, scp_ref, shs_ref, scs_ref, rw_ref, rb_ref,
                  x1_ref, h2_ref, ti_ref, tg_ref, *, n_p, reps):
    i = pl.program_id(0)
    d = x_ref.shape[1]
    is_p = i < n_p
    ya = jnp.where(is_p, yap_ref[...], yas_ref[...])
    yb = jnp.where(is_p, ybp_ref[...], ybs_ref[...])
    yc = jnp.where(is_p, ycp_ref[...], ycs_ref[...])
    g = _sigmoid(jnp.dot(h_ref[...], wg_ref[...], preferred_element_type=F32) + bg_ref[...])
    merged = (g[:, 0:d] * _mm(ya, pa_ref[...]) + g[:, d:2 * d] * _mm(yb, pb_ref[...])
              + g[:, 2 * d:3 * d] * _mm(yc, pc_ref[...]))
    mix = _mm(merged, wo_ref[...])

    def finish(g1, sh, sc):
        x1 = x_ref[...] + g1 * mix
        x1_ref[...] = x1
        h2_ref[...] = _rms(x1, nw_ref[...]) * (1.0 + sc) + sh

    @pl.when(i < n_p)
    def _():
        finish(g1p_ref[...], shp_ref[...], scp_ref[...])

    @pl.when(i >= n_p)
    def _():
        finish(_tile_rows(g1s_ref[...], reps), _tile_rows(shs_ref[...], reps), _tile_rows(scs_ref[...], reps))

    h2 = h2_ref[...]
    rw = rw_ref[...]
    hs = _split(h2, 2)
    ws = _split(rw, 2)
    if len(hs) == 1:
        logits = jnp.dot(h2, rw, preferred_element_type=F32)
    else:
        logits = (jnp.dot(hs[0], ws[0], preferred_element_type=F32)
                  + jnp.dot(hs[0], ws[1], preferred_element_type=F32)
                  + jnp.dot(hs[1], ws[0], preferred_element_type=F32))
    logits = logits + rb_ref[...]
    lane = _iota(logits.shape, 1).astype(F32)
    vals, idxs = [], []
    cur = logits
    for _ in range(TOP_K):
        mx = jnp.max(cur, axis=-1, keepdims=True)
        ix = jnp.min(jnp.where(cur == mx, lane, float(LANES)), axis=-1, keepdims=True)
        vals.append(mx)
        idxs.append(ix)
        cur = jnp.where(lane == ix, -jnp.inf, cur)
    es = [jnp.exp(v - vals[0]) for v in vals]
    den = es[0]
    for e in es[1:]:
        den = den + e
    ti = jnp.zeros(logits.shape, F32)
    tg = jnp.zeros(logits.shape, F32)
    for k in range(TOP_K):
        ti = jnp.where(lane == float(k), idxs[k], ti)
        tg = jnp.where(lane == float(k), es[k] / den, tg)
    ti_ref[...] = ti.astype(I32)
    tg_ref[...] = tg


def _merge(h, ys_p, ys_s, x_all, wts, ada_p, ada_s, dims, tm_m=512):
    d, tm, seq, b_p, b_s = dims
    t_all = x_all.shape[0]
    tm_m = min(tm_m, tm)
    mdims = (d, tm_m, seq, b_p, b_s)
    n_p = b_p * seq // tm_m
    g1p, g1s = _mod_specs(mdims, 2, 1)
    shp, shs = _mod_specs(mdims, 3, 1)
    scp, scs = _mod_specs(mdims, 4, 1)
    row = lambda n: pl.BlockSpec((tm_m, n), lambda i: (i, 0))
    row_p = pl.BlockSpec((tm_m, d), lambda i: (jnp.minimum(i, n_p - 1), 0))
    row_s = pl.BlockSpec((tm_m, d), lambda i: (jnp.maximum(i - n_p, 0), 0))
    const = lambda shape: pl.BlockSpec(shape, lambda i: tuple(0 for _ in shape), pipeline_mode=pl.Buffered(1))
    kern = functools.partial(_merge_kernel, n_p=n_p, reps=max(tm_m // b_s, 1))
    return pl.pallas_call(
        kern,
        grid=(t_all // tm_m,),
        in_specs=[row(d), row_p, row_p, row_p, row_s, row_s, row_s, row(d),
                  const((d, 3 * d)), const((1, 3 * d)), const((d, d)), const((d, d)), const((d, d)),
                  const((d, d)),
                  g1p, g1s, const((1, d)), shp, scp, shs, scs,
                  const((d, LANES)), const((1, LANES))],
        out_specs=[row(d), row(d), row(LANES), row(LANES)],
        out_shape=[jax.ShapeDtypeStruct((t_all, d), F32),
                   jax.ShapeDtypeStruct((t_all, d), F32),
                   jax.ShapeDtypeStruct((t_all, LANES), I32),
                   jax.ShapeDtypeStruct((t_all, LANES), F32)],
        compiler_params=_cp("arbitrary"),
        name="merge",
    )(h, *ys_p, *ys_s, x_all, wts["wg"], wts["bg"], wts["pa"], wts["pb"], wts["pc"], wts["wo"],
      ada_p, ada_s, wts["nffn"], ada_p, ada_p, ada_s, ada_s, wts["rw"], wts["rb"])


def _rank_kernel(ti_ref, dest_ref, cnt_ref, run, tot):
    ph = pl.program_id(0)
    i = pl.program_id(1)
    tm = ti_ref.shape[0]
    ti = ti_ref[...]
    lane = _iota((tm, LANES), 1)
    oh = jnp.zeros((tm, LANES), F32)
    for k in range(TOP_K):
        oh = oh + jnp.where(lane == ti[:, k:k + 1], 1.0, 0.0)
    ones = jnp.ones((SUBLANES, tm), MXU_DTYPE)
    colsum = jnp.dot(ones, oh.astype(MXU_DTYPE), preferred_element_type=F32)[0:1, :]

    @pl.when((ph == 0) & (i == 0))
    def _():
        tot[...] = jnp.zeros(tot.shape, F32)

    @pl.when(ph == 0)
    def _():
        tot[...] = tot[...] + colsum

    @pl.when((ph == 1) & (i == 0))
    def _():
        run[...] = jnp.zeros(run.shape, F32)

    @pl.when(ph == 1)
    def _():
        r = _iota((LANES, LANES), 0)
        c = _iota((LANES, LANES), 1)
        upper = jnp.where(r < c, 1.0, 0.0).astype(MXU_DTYPE)
        starts = _mm_sel(jnp.broadcast_to(tot[...], (SUBLANES, LANES)), upper)[0:1, :]
        rr = _iota((tm, tm), 0)
        cc = _iota((tm, tm), 1)
        lower = jnp.where(rr > cc, 1.0, 0.0).astype(MXU_DTYPE)
        pre = jnp.dot(lower, oh.astype(MXU_DTYPE), preferred_element_type=F32)
        pos = pre + run[...] + starts
        dest = jnp.zeros((tm, LANES), F32)
        for k in range(TOP_K):
            dk = jnp.sum(jnp.where(lane == ti[:, k:k + 1], pos, 0.0), axis=1, keepdims=True)
            dest = jnp.where(lane == k, dk, dest)
        dest_ref[...] = dest.astype(I32)
        run[...] = run[...] + colsum

    cnt_ref[...] = jnp.broadcast_to(tot[...], cnt_ref.shape)


def _rank(topi, tm):
    t_all = topi.shape[0]
    return pl.pallas_call(
        _rank_kernel,
        grid=(2, t_all // tm),
        in_specs=[pl.BlockSpec((tm, LANES), lambda ph, i: (i, 0))],
        out_specs=[pl.BlockSpec((tm, LANES), lambda ph, i: (i * ph, 0)),
                   pl.BlockSpec((SUBLANES, LANES), lambda ph, i: (0, 0))],
        out_shape=[jax.ShapeDtypeStruct((t_all, LANES), I32),
                   jax.ShapeDtypeStruct((SUBLANES, LANES), F32)],
        scratch_shapes=[pltpu.VMEM((1, LANES), F32), pltpu.VMEM((1, LANES), F32)],
        compiler_params=_cp("arbitrary", "arbitrary"),
        name="rank",
    )(topi)


def _dispatch_kernel(dest_ref, h2_ref, xs_ref, sem):
    tm = h2_ref.shape[0]

    def issue(r, carry):
        for k in range(TOP_K):
            pltpu.make_async_copy(h2_ref.at[pl.ds(r, 1)], xs_ref.at[pl.ds(dest_ref[r * TOP_K + k], 1)],
                                  sem).start()
        return carry

    lax.fori_loop(0, tm, issue, 0)
    for _ in range(TOP_K):
        pltpu.make_async_copy(h2_ref, xs_ref.at[pl.ds(0, tm)], sem).wait()


def _dispatch(dest_flat, h2, tm=256):
    t_all, d = h2.shape
    return pl.pallas_call(
        _dispatch_kernel,
        grid=(t_all // tm,),
        in_specs=[pl.BlockSpec((tm * TOP_K,), lambda i: (i,), memory_space=pltpu.SMEM),
                  pl.BlockSpec((tm, d), lambda i: (i, 0))],
        out_specs=pl.BlockSpec(memory_space=pl.ANY),
        out_shape=jax.ShapeDtypeStruct((t_all * TOP_K, d), F32),
        scratch_shapes=[pltpu.SemaphoreType.DMA(())],
        compiler_params=_cp("arbitrary"),
        name="dispatch",
    )(dest_flat, h2)


def _expert_kernel(tile_ref, exp_ref, lo_ref, hi_ref, firste_ref, firstt_ref, valid_ref, slot_ref, nexte_ref,
                   x_ref, wgu_hbm, bgu_ref, wd_hbm, bd_ref, o_ref, wgu_f, wd_f, wgu_s, wd_s, sem, *, layer):
    w = pl.program_id(0)
    tmx = x_ref.shape[0]
    f = wd_s.shape[0]

    def fetch(e, slot):
        return (pltpu.make_async_copy(wgu_hbm.at[layer, e], wgu_f.at[slot], sem.at[slot]),
                pltpu.make_async_copy(wd_hbm.at[layer, e], wd_f.at[slot], sem.at[slot]))

    @pl.when(w == 0)
    def _():
        for cp in fetch(exp_ref[0], slot_ref[0]):
            cp.start()

    @pl.when(firste_ref[w] == 1)
    def _():
        slot = slot_ref[w]
        for cp in fetch(exp_ref[w], slot):
            cp.wait()
        wgu_s[...] = wgu_f[slot].astype(wgu_s.dtype)
        wd_s[...] = wd_f[slot].astype(wd_s.dtype)

        @pl.when(nexte_ref[w] >= 0)
        def _():
            for cp in fetch(nexte_ref[w], 1 - slot):
                cp.start()

    @pl.when(valid_ref[w] == 1)
    def _():
        gu = jnp.dot(x_ref[...].astype(MXU_DTYPE), wgu_s[...], preferred_element_type=F32) + bgu_ref[...]
        g = jnp.minimum(gu[:, 0:f], SWIGLU_LIMIT)
        u = jnp.clip(gu[:, f:2 * f], -SWIGLU_LIMIT, SWIGLU_LIMIT)
        act = g * _sigmoid(SWIGLU_ALPHA * g) * (u + 1.0)
        res = jnp.dot(act.astype(MXU_DTYPE), wd_s[...], preferred_element_type=F32) + bd_ref[...]
        rows = tile_ref[w] * tmx + _iota((tmx, 1), 0)
        mine = (rows >= lo_ref[w]) & (rows < hi_ref[w])

        @pl.when(firstt_ref[w] == 1)
        def _():
            o_ref[...] = jnp.where(mine, res, 0.0)

        @pl.when(firstt_ref[w] == 0)
        def _():
            o_ref[...] = jnp.where(mine, res, o_ref[...])


def _experts(meta, xs, w_gu, b_gu, w_d, b_d, layer, tmx):
    tk, d = xs.shape
    depth, n_e, _, f2 = w_gu.shape
    f = f2 // 2
    n_w = meta[0].shape[0]
    grid_spec = pltpu.PrefetchScalarGridSpec(
        num_scalar_prefetch=9,
        grid=(n_w,),
        in_specs=[pl.BlockSpec((tmx, d), lambda w, tl, ex, *_: (tl[w], 0)),
                  pl.BlockSpec(memory_space=pl.ANY),
                  pl.BlockSpec((None, None, 1, f2), lambda w, tl, ex, *_: (layer, ex[w], 0, 0)),
                  pl.BlockSpec(memory_space=pl.ANY),
                  pl.BlockSpec((None, None, 1, d), lambda w, tl, ex, *_: (layer, ex[w], 0, 0))],
        out_specs=pl.BlockSpec((tmx, d), lambda w, tl, ex, *_: (tl[w], 0)),
        scratch_shapes=[pltpu.VMEM((2, d, f2), F32), pltpu.VMEM((2, f, d), F32),
                        pltpu.VMEM((d, f2), MXU_DTYPE), pltpu.VMEM((f, d), MXU_DTYPE),
                        pltpu.SemaphoreType.DMA((2,))],
    )
    return pl.pallas_call(
        functools.partial(_expert_kernel, layer=layer),
        grid_spec=grid_spec,
        out_shape=jax.ShapeDtypeStruct((tk, d), F32),
        compiler_params=_cp("arbitrary"),
        name="experts",
    )(*meta, xs, w_gu, b_gu.reshape(depth, n_e, 1, f2), w_d, b_d.reshape(depth, n_e, 1, d))


def _expert_schedule(counts, tk, tmx):
    n_e = counts.shape[0]
    n_w = tk // tmx + n_e - 1
    e_ids = jnp.arange(n_e, dtype=I32)
    tri = (e_ids[:, None] <= e_ids[None, :]).astype(I32)
    ends = jnp.sum(counts[:, None] * tri, axis=0)
    starts = ends - counts
    first_tile = starts // tmx
    last_tile = jnp.maximum(ends - 1, 0) // tmx
    n_t = jnp.where(counts > 0, last_tile - first_tile + 1, 0)
    cum = jnp.sum(n_t[:, None] * tri, axis=0)
    total = jnp.sum(n_t)

    def item(wv):
        ex = jnp.sum((cum[None, :] <= wv[:, None]).astype(I32), axis=1)
        oh = (ex[:, None] == e_ids[None, :]).astype(I32)
        pick = lambda v: jnp.sum(oh * v[None, :], axis=1)
        tile = pick(first_tile) + wv - (pick(cum) - pick(n_t))
        return ex, tile, pick(starts), pick(ends)

    w = jnp.arange(n_w, dtype=I32)
    wc = jnp.minimum(w, total - 1)
    ex, tile, lo, hi = item(wc)
    ex_prev, tile_prev, _, _ = item(jnp.maximum(wc - 1, 0))
    valid = w < total
    first_e = valid & ((w == 0) | (ex != ex_prev))
    first_t = valid & ((w == 0) | (tile != tile_prev))
    nonempty = (n_t > 0).astype(I32)
    ordinal = jnp.sum(nonempty[:, None] * tri, axis=0) - nonempty
    later = (e_ids[None, :] > e_ids[:, None]) & (nonempty[None, :] == 1)
    nxt = jnp.min(jnp.where(later, e_ids[None, :], n_e), axis=1)
    nxt = jnp.where(nxt == n_e, -1, nxt)
    oh = (ex[:, None] == e_ids[None, :]).astype(I32)
    slot = jnp.sum(oh * (ordinal % 2)[None, :], axis=1)
    next_e = jnp.sum(oh * nxt[None, :], axis=1)
    return (tile, ex, lo, hi, first_e.astype(I32), first_t.astype(I32), valid.astype(I32),
            slot.astype(I32), next_e.astype(I32))


def _combine_kernel(dest_ref, eo_ref, x1_ref, tg_ref, g2p_ref, g2s_ref, nf_ref, o_ref, buf, sem, *,
                    n_p, reps, final):
    i = pl.program_id(0)
    tm = x1_ref.shape[0]

    def issue(r, carry):
        for k in range(TOP_K):
            pltpu.make_async_copy(eo_ref.at[pl.ds(dest_ref[r * TOP_K + k], 1)], buf.at[k, pl.ds(r, 1)],
                                  sem).start()
        return carry

    lax.fori_loop(0, tm, issue, 0)
    for k in range(TOP_K):
        pltpu.make_async_copy(eo_ref.at[pl.ds(0, tm)], buf.at[k], sem).wait()
    tg = tg_ref[...]
    y = tg[:, 0:1] * buf[0]
    for k in range(1, TOP_K):
        y = y + tg[:, k:k + 1] * buf[k]

    def finish(g2):
        x2 = x1_ref[...] + g2 * y
        o_ref[...] = _rms(x2, nf_ref[...]) if final else x2

    @pl.when(i < n_p)
    def _():
        finish(g2p_ref[...])

    @pl.when(i >= n_p)
    def _():
        finish(_tile_rows(g2s_ref[...], reps))


def _combine(dest_flat, eo, x1, topg, ada_p, ada_s, norm_final, dims, final, tm_c=256):
    d, tm, seq, b_p, b_s = dims
    t_all = x1.shape[0]
    tm_c = min(tm_c, tm)
    cdims = (d, tm_c, seq, b_p, b_s)
    n_p = b_p * seq // tm_c
    g2p, g2s = _mod_specs(cdims, 5, 1)
    kern = functools.partial(_combine_kernel, n_p=n_p, reps=max(tm_c // b_s, 1), final=final)
    return pl.pallas_call(
        kern,
        grid=(t_all // tm_c,),
        in_specs=[pl.BlockSpec((tm_c * TOP_K,), lambda i: (i,), memory_space=pltpu.SMEM),
                  pl.BlockSpec(memory_space=pl.ANY),
                  pl.BlockSpec((tm_c, d), lambda i: (i, 0)),
                  pl.BlockSpec((tm_c, LANES), lambda i: (i, 0)),
                  g2p, g2s,
                  pl.BlockSpec((1, d), lambda i: (0, 0))],
        out_specs=pl.BlockSpec((tm_c, d), lambda i: (i, 0)),
        out_shape=jax.ShapeDtypeStruct((t_all, d), F32),
        scratch_shapes=[pltpu.VMEM((TOP_K, tm_c, d), F32), pltpu.SemaphoreType.DMA(())],
        compiler_params=_cp("arbitrary"),
        name="combine",
    )(dest_flat, eo, x1, topg, ada_p, ada_s, norm_final.reshape(1, d))


def _rope_tables(pos):
    half = ROPE_DIMS // 2
    inv_freq = jnp.exp(-math.log(ROPE_THETA) * jnp.arange(half, dtype=F32) / half)
    ang = pos.astype(F32)[:, None] * inv_freq[None, :]
    cos, sin = jnp.cos(ang), jnp.sin(ang)
    n = pos.shape[0]
    rest = HEAD_DIM - ROPE_DIMS
    cos_h = jnp.concatenate([cos, cos, jnp.ones((n, rest), F32)], axis=1)
    sa_h = jnp.concatenate([jnp.zeros((n, half), F32), sin, jnp.zeros((n, rest), F32)], axis=1)
    sb_h = jnp.concatenate([-sin, jnp.zeros((n, half + rest), F32)], axis=1)
    rep = LANES // HEAD_DIM
    return tuple(jnp.tile(t, (1, rep)) for t in (cos_h, sa_h, sb_h))


def _pad_lanes(v, n, value=0.0):
    return jnp.pad(v, ((0, 0), (0, n - v.shape[1])), constant_values=value)


def kernel(x_prompt, x_sample, state_conv_a, state_conv_ssm, state_ssm, cache_k, cache_v, c_prompt, c_sample, w_ada, b_ada, norm_mix, norm_ffn, w_in, conv_a_w, ssm_conv_w, ssm_conv_b, ssm_dt_bias, ssm_a_log, ssm_d, ssm_norm, attn_sinks, w_branch_gate, b_branch_gate, w_proj_a, w_proj_b, w_proj_c, w_out, router_w, router_b, w_gate_up, b_gate_up, w_down, b_down, norm_final):
    b_p, seq, d = x_prompt.shape
    b_s, steps, _ = x_sample.shape
    depth = w_ada.shape[0]
    t_p, t_s = b_p * seq, b_s * steps
    t_all = t_p + t_s
    tm = t_s
    dims = (d, tm, seq, b_p, b_s)
    n_e = router_w.shape[2]
    d_ssm = SSM_HEADS * SSM_HEAD_DIM
    n_bc = 2 * SSM_GROUPS * SSM_STATE
    d_kv = KV_HEADS * HEAD_DIM
    pn = SSM_HEAD_DIM * SSM_STATE
    assert d == d_ssm == ATTN_HEADS * HEAD_DIM and t_p % t_s == 0 and seq % t_s == 0
    assert t_s % 256 == 0 or t_s <= 256
    assert steps <= SUBLANES and PAST_LEN >= WINDOW and seq >= WINDOW and b_s % SUBLANES == 0

    x_all = jnp.concatenate([x_prompt.reshape(t_p, d), x_sample.transpose(1, 0, 2).reshape(t_s, d)], axis=0)
    ada = _ada_all(jnp.concatenate([c_prompt, c_sample], axis=0), w_ada, b_ada)

    o_z = 3 * d
    o_bc = o_z + 2 * d_ssm
    o_dt = o_bc + n_bc
    o_q = o_dt + SSM_HEADS
    o_k = o_q + d
    inv_order = [HEAD_ORDER.index(h) for h in range(ATTN_HEADS)]

    def to_head_order(a, axis):
        return jnp.concatenate([lax.slice_in_dim(a, h * HEAD_DIM, (h + 1) * HEAD_DIM, axis=axis)
                                for h in HEAD_ORDER], axis=axis)

    def from_head_order(a, axis):
        return jnp.concatenate([lax.slice_in_dim(a, i * HEAD_DIM, (i + 1) * HEAD_DIM, axis=axis)
                                for i in inv_order], axis=axis)

    w_pack = jnp.concatenate(
        [w_in[:, :, :o_bc], to_head_order(w_in[:, :, o_q:o_k], 2), w_in[:, :, o_bc:o_dt], w_in[:, :, o_k:],
         w_in[:, :, o_dt:o_q], jnp.zeros((depth, d, N_TAIL - T_DT - SSM_HEADS), F32)], axis=2).astype(MXU_DTYPE)

    hp = jnp.arange(d_ssm) // SSM_HEAD_DIM
    rexp = (jnp.arange(LANES)[:, None] == hp[None, :]).astype(MXU_DTYPE)
    lane_pn = jnp.arange(pn)
    e2 = jnp.stack([(jnp.arange(LANES)[:, None] == (lane_pn // SSM_STATE + lh * SSM_HEAD_DIM)[None, :])
                    for lh in range(2)]).astype(MXU_DTYPE)
    g2 = jnp.transpose(e2, (0, 2, 1))
    state_ssm3 = state_ssm.reshape(depth, b_s, SSM_HEADS * pn)

    tabs_p = _rope_tables(jnp.arange(seq, dtype=I32))
    tabs_s = _rope_tables(jnp.repeat(PAST_LEN + jnp.arange(steps, dtype=I32), b_s))
    eye_kv = jnp.eye(KV_HEADS, dtype=F32)

    outs = [[] for _ in range(10)]
    pending = None
    for l in range(depth):
        ada_p = ada[l, :b_p].reshape(b_p, 1, 6 * d)
        ada_s = ada[l, b_p:]
        ssm_prm = dict(
            wx=ssm_conv_w[l, :, :d_ssm], wbc=ssm_conv_w[l, :, d_ssm:],
            bx=ssm_conv_b[l, :d_ssm].reshape(1, d_ssm), bbc=ssm_conv_b[l, d_ssm:].reshape(1, n_bc),
            dtb=_pad_lanes(ssm_dt_bias[l].reshape(1, SSM_HEADS), LANES),
            alog=_pad_lanes(ssm_a_log[l].reshape(1, SSM_HEADS), LANES),
            de=jnp.repeat(ssm_d[l], SSM_HEAD_DIM).reshape(1, d_ssm),
            gn=ssm_norm[l].reshape(1, d_ssm), rexp=rexp, e2=e2, g2=g2)
        wts = dict(
            wg=w_branch_gate[l].astype(MXU_DTYPE), bg=b_branch_gate[l].reshape(1, 3 * d),
            pa=w_proj_a[l].astype(MXU_DTYPE), pb=w_proj_b[l].astype(MXU_DTYPE),
            pc=to_head_order(w_proj_c[l], 0).astype(MXU_DTYPE), wo=w_out[l].astype(MXU_DTYPE),
            nffn=norm_ffn[l].reshape(1, d),
            rw=_pad_lanes(router_w[l], LANES), rb=_pad_lanes(router_b[l].reshape(1, n_e), LANES, NEG_BIG))

        if pending is None:
            p, p_tail, h = _inproj(x_all, norm_mix[l], ada_p, ada_s, w_pack, l, dims)
        else:
            p, p_tail, h, x_all = _inproj_combine(*pending, norm_mix[l], ada_p, ada_s, w_pack, l, dims)

        ya_p, ca_p = _conva_prompt(p, conv_a_w[l], dims, t_p)
        ca_tm = state_conv_a[l].transpose(1, 0, 2).reshape((CONV_A_WIDTH - 1) * b_s, d)
        ya_s, ca_s = _conva_sample(p, ca_tm, conv_a_w[l], dims, steps)
        outs[0].append(ca_p[:, SUBLANES - (CONV_A_WIDTH - 1):])
        outs[1].append(ca_s.reshape(CONV_A_WIDTH - 1, b_s, d).transpose(1, 0, 2))

        yb_p, csx_p, csbc_p, st_p = _ssd_prompt(p, p_tail, ssm_prm, dims, t_p)
        nsc = SSM_CONV_WIDTH - 1
        outs[2].append(jnp.concatenate([csx_p[:, SUBLANES - nsc:], csbc_p[:, SUBLANES - nsc:]], axis=2))
        hg = SSM_HEADS // SSM_GROUPS
        st_g = jnp.stack([st_p[:, g * SSM_STATE:(g + 1) * SSM_STATE, g * (d_ssm // 2):(g + 1) * (d_ssm // 2)]
                          for g in range(SSM_GROUPS)], axis=1)
        outs[4].append(st_g.reshape(b_p, SSM_GROUPS, SSM_STATE, hg, SSM_HEAD_DIM)
                       .transpose(0, 1, 3, 4, 2).reshape(b_p, SSM_HEADS, SSM_HEAD_DIM, SSM_STATE))
        cs_tm = state_conv_ssm[l].transpose(1, 0, 2).reshape(nsc * b_s, d_ssm + n_bc)
        y_raw, st_s, csx_s, csbc_s = _ssd_sample(p, p_tail, cs_tm[:, :d_ssm], cs_tm[:, d_ssm:],
                                                 state_ssm3, l, ssm_prm, dims, steps)
        yb_s = _gnorm_sample(y_raw, p, ssm_prm["gn"], dims, steps)
        outs[3].append(jnp.concatenate([csx_s, csbc_s], axis=1).reshape(nsc, b_s, d_ssm + n_bc).transpose(1, 0, 2))
        outs[5].append(st_s.reshape(b_s, SSM_HEADS, SSM_HEAD_DIM, SSM_STATE))

        yc_p, k_p, v_p = _attn_prompt(p, p_tail, attn_sinks[l], tabs_p, dims, t_p)
        outs[6].append(k_p.reshape(b_p, WINDOW, KV_HEADS, HEAD_DIM))
        outs[8].append(v_p.reshape(b_p, WINDOW, KV_HEADS, HEAD_DIM))
        q_s, k_s = _rope_sample(p, p_tail, tabs_s, dims, steps)
        v_s = p_tail[t_p:, T_V:T_V + d_kv]
        k_new = jnp.concatenate([cache_k[l][:, steps:].reshape(b_s, WINDOW - steps, d_kv),
                                 k_s.reshape(steps, b_s, d_kv).transpose(1, 0, 2)], axis=1)
        v_new = jnp.concatenate([cache_v[l][:, steps:].reshape(b_s, WINDOW - steps, d_kv),
                                 v_s.reshape(steps, b_s, d_kv).transpose(1, 0, 2)], axis=1)
        outs[7].append(k_new.reshape(b_s, WINDOW, KV_HEADS, HEAD_DIM))
        outs[9].append(v_new.reshape(b_s, WINDOW, KV_HEADS, HEAD_DIM))
        q4 = from_head_order(q_s, 1).reshape(steps, b_s, KV_HEADS, Q_PER_KV, HEAD_DIM).transpose(1, 2, 3, 0, 4)
        qbd = (q4.reshape(b_s, KV_HEADS, Q_PER_KV * steps, 1, HEAD_DIM)
               * eye_kv[None, :, None, :, None]).reshape(b_s, KV_HEADS * Q_PER_KV * steps, d_kv)
        sink_col = jnp.broadcast_to(jnp.repeat(attn_sinks[l], steps)[:, None], (ATTN_HEADS * steps, LANES))
        o_s = _attn_sample(qbd, k_new, v_new, cache_k[l].reshape(b_s, WINDOW, d_kv),
                           cache_v[l].reshape(b_s, WINDOW, d_kv), sink_col, steps)
        yc_s = to_head_order(
            o_s.reshape(b_s, KV_HEADS, Q_PER_KV, steps, HEAD_DIM).transpose(3, 0, 1, 2, 4).reshape(t_s, d), 1
        ).astype(MXU_DTYPE)

        x1, h2, topi, topg = _merge(h, (ya_p, yb_p, yc_p), (ya_s, yb_s, yc_s), x_all, wts, ada_p, ada_s, dims)
        dest, cnt = _rank(topi, tm)
        dest_flat = dest[:, :TOP_K].reshape(t_all * TOP_K)
        xs = _dispatch(dest_flat, h2, min(256, tm))
        tmx = min(256, tm)
        meta = _expert_schedule(cnt[0, :n_e].astype(I32), t_all * TOP_K, tmx)
        eo = _experts(meta, xs, w_gate_up, b_gate_up, w_down, b_down, l, tmx)
        pending = (dest_flat, eo, x1, topg, ada_p, ada_s)

    x_all = _combine(*pending, norm_final, dims, final=True)
    y_prompt = x_all[:t_p].reshape(b_p, seq, d)
    y_sample = x_all[t_p:].reshape(steps, b_s, d).transpose(1, 0, 2)
    st = [jnp.stack(o) for o in outs]
    return (y_prompt, y_sample, st[0], st[1], st[2], st[3], st[4], st[5], st[6], st[7], st[8], st[9])
```

```python
import functools
import math

import jax
import jax.numpy as jnp
from jax import lax
from jax.experimental import pallas as pl
from jax.experimental.pallas import tpu as pltpu

F32 = jnp.float32
BF16 = jnp.bfloat16
I32 = jnp.int32

PAST_LEN = 8192
SSM_HEADS = 16
SSM_HEAD_DIM = 64
SSM_GROUPS = 2
SSM_STATE = 64
SSM_CONV_WIDTH = 4
CONV_A_WIDTH = 3
SSD_CHUNK = 128
ATTN_HEADS = 16
KV_HEADS = 4
HEAD_DIM = 64
Q_PER_KV = ATTN_HEADS // KV_HEADS
WINDOW = 128
ROPE_THETA = 500000.0
ROPE_DIMS = HEAD_DIM // 4
N_EXPERTS = 32
TOP_K = 4
SWIGLU_LIMIT = 7.0
SWIGLU_ALPHA = 1.702
RMS_EPS = 1e-5
NEG_BIG = -1e30

LANES = 128
SUBLANES = 8
VMEM_LIMIT = 56 * 1024 * 1024

MXU_DTYPE = BF16
C_B, C_C, C_X, C_Z, C_XS, C_Q, N_MAIN = 0, 1024, 2048, 3072, 4096, 5120, 6144
T_BC, T_K, T_V, T_DT, N_TAIL = 0, 256, 512, 768, 1024
N_COL = N_MAIN + N_TAIL
HEAD_ORDER = tuple(h for j in range(ATTN_HEADS // 2)
                   for h in ((j // Q_PER_KV) * 2 * Q_PER_KV + j % Q_PER_KV,
                             (j // Q_PER_KV) * 2 * Q_PER_KV + Q_PER_KV + j % Q_PER_KV))


def _cp(*sem):
    return pltpu.CompilerParams(dimension_semantics=sem, vmem_limit_bytes=VMEM_LIMIT)


def _mm(a, b):
    return jnp.dot(a.astype(MXU_DTYPE), b.astype(MXU_DTYPE), preferred_element_type=F32)


def _mm_nt(a, b):
    return lax.dot_general(a.astype(MXU_DTYPE), b.astype(MXU_DTYPE), (((1,), (1,)), ((), ())),
                           preferred_element_type=F32)


def _split(v, n):
    if MXU_DTYPE == F32:
        return [v]
    parts, r = [], v
    for _ in range(n):
        p = r.astype(MXU_DTYPE)
        parts.append(p)
        r = r - p.astype(F32)
    return parts


def _mm_sel(v, sel, n=3):
    acc = None
    for p in _split(v, n):
        t = jnp.dot(p, sel, preferred_element_type=F32)
        acc = t if acc is None else acc + t
    return acc


def _sel_mm(sel, v, n=3):
    acc = None
    for p in _split(v, n):
        t = jnp.dot(sel, p, preferred_element_type=F32)
        acc = t if acc is None else acc + t
    return acc


def _rms(x, w):
    return x * lax.rsqrt(jnp.mean(x * x, axis=-1, keepdims=True) + RMS_EPS) * w


def _sigmoid(x):
    return 1.0 / (1.0 + jnp.exp(-x))


def _silu(x):
    return x * _sigmoid(x)


def _softplus(x):
    return jnp.maximum(x, 0.0) + jnp.log(1.0 + jnp.exp(-jnp.abs(x)))


def _tile_rows(x, n):
    return x if n == 1 else jnp.concatenate([x] * n, axis=0)


def _tile_lanes(x, n):
    return x if n == 1 else jnp.concatenate([x] * n, axis=1)


def _iota(shape, dim):
    return lax.broadcasted_iota(I32, shape, dim)


def _half_mask(lane, upper):
    return lane >= HEAD_DIM if upper else lane < HEAD_DIM


def _ada_kernel(c_ref, w_ref, b_ref, o_ref):
    a = _silu(c_ref[...])
    o_ref[...] = _mm(a, w_ref[...]) + b_ref[...]


def _ada_all(c_all, w_ada, b_ada):
    depth, d, n6 = w_ada.shape
    nb = c_all.shape[0]
    tn = d
    return pl.pallas_call(
        _ada_kernel,
        grid=(depth, n6 // tn),
        in_specs=[pl.BlockSpec((nb, d), lambda l, j: (0, 0)),
                  pl.BlockSpec((None, d, tn), lambda l, j: (l, 0, j)),
                  pl.BlockSpec((None, 1, tn), lambda l, j: (l, 0, j))],
        out_specs=pl.BlockSpec((None, nb, tn), lambda l, j: (l, 0, j)),
        out_shape=jax.ShapeDtypeStruct((depth, nb, n6), F32),
        compiler_params=_cp("arbitrary", "arbitrary"),
        name="ada",
    )(c_all, w_ada, b_ada.reshape(depth, 1, n6))


def _inproj_kernel(x_ref, nw_ref, shp_ref, scp_ref, shs_ref, scs_ref, w_ref, pm_ref, pt_ref, h_ref, *,
                   n_p, reps, n_main):
    i = pl.program_id(0)
    j = pl.program_id(1)

    def make_h(sh, sc):
        y = _rms(x_ref[...], nw_ref[...])
        h_ref[...] = (y * (1.0 + sc) + sh).astype(h_ref.dtype)

    @pl.when((j == 0) & (i < n_p))
    def _():
        make_h(shp_ref[...], scp_ref[...])

    @pl.when((j == 0) & (i >= n_p))
    def _():
        make_h(_tile_rows(shs_ref[...], reps), _tile_rows(scs_ref[...], reps))

    res = jnp.dot(h_ref[...], w_ref[...], preferred_element_type=F32)

    @pl.when(j < n_main)
    def _():
        pm_ref[...] = res.astype(pm_ref.dtype)

    @pl.when(j == n_main)
    def _():
        pt_ref[...] = res


def _mod_specs(dims, col, grid_rank):
    d, tm, seq, b_p, b_s = dims

    def pidx(i, *_):
        return (jnp.minimum(i * tm // seq, b_p - 1), 0, col)

    def sidx(i, *_):
        return (0, col)

    return pl.BlockSpec((None, 1, d), pidx), pl.BlockSpec((b_s, d), sidx)


def _inproj(x_all, norm_w, ada_p, ada_s, w_pack, layer, dims):
    d, tm, seq, b_p, b_s = dims
    tn = N_TAIL
    t_all = x_all.shape[0]
    n_t = t_all // tm
    n_p = b_p * seq // tm
    n_main = N_MAIN // tn
    shp, shs = _mod_specs(dims, 0, 2)
    scp, scs = _mod_specs(dims, 1, 2)
    kern = functools.partial(_inproj_kernel, n_p=n_p, reps=tm // b_s, n_main=n_main)
    return pl.pallas_call(
        kern,
        grid=(n_t, n_main + 1),
        in_specs=[pl.BlockSpec((tm, d), lambda i, j: (i, 0)),
                  pl.BlockSpec((1, d), lambda i, j: (0, 0)),
                  shp, scp, shs, scs,
                  pl.BlockSpec((None, d, tn), lambda i, j: (layer, 0, j))],
        out_specs=[pl.BlockSpec((tm, tn), lambda i, j: (i, jnp.minimum(j, n_main - 1))),
                   pl.BlockSpec((tm, tn), lambda i, j: (i, 0)),
                   pl.BlockSpec((tm, d), lambda i, j: (i, 0))],
        out_shape=[jax.ShapeDtypeStruct((t_all, N_MAIN), MXU_DTYPE),
                   jax.ShapeDtypeStruct((t_all, N_TAIL), F32),
                   jax.ShapeDtypeStruct((t_all, d), MXU_DTYPE)],
        compiler_params=_cp("arbitrary", "arbitrary"),
        name="inproj",
    )(x_all, norm_w.reshape(1, d), ada_p, ada_p, ada_s, ada_s, w_pack)


def _tokens_per_step(tm, n_j):
    per = -(-tm // n_j)
    return per + per % 2


def _inproj_c_kernel(dcur_ref, dnxt_ref, eo_ref, x1_ref, tg_ref, g2p_ref, g2s_ref,
                     nw_ref, shp_ref, scp_ref, shs_ref, scs_ref, w_ref,
                     pm_ref, pt_ref, h_ref, xn_ref, gbuf, sem, *, n_p, reps, n_main):
    i = pl.program_id(0)
    j = pl.program_id(1)
    n_t = pl.num_programs(0)
    n_j = n_main + 1
    tm = x1_ref.shape[0]
    per = _tokens_per_step(tm, n_j)
    surplus = (per * n_j - tm) * TOP_K

    def issue(d_ref, buf, tok0):
        for u in range(per):
            t = tok0 + u
            ts = jnp.minimum(t, tm - 1)
            for k in range(TOP_K):
                pltpu.make_async_copy(eo_ref.at[pl.ds(d_ref[ts * TOP_K + k], 1)],
                                      gbuf.at[buf, k, pl.ds(t, 1)], sem.at[buf]).start()

    def wait(buf):
        for k in range(TOP_K):
            pltpu.make_async_copy(eo_ref.at[pl.ds(0, tm)], gbuf.at[buf, k, pl.ds(0, tm)], sem.at[buf]).wait()
        if surplus:
            pltpu.make_async_copy(eo_ref.at[pl.ds(0, surplus)], gbuf.at[buf, 0, pl.ds(0, surplus)], sem.at[buf]).wait()

    @pl.when((i == 0) & (j == 0))
    def _():
        def body(jj, carry):
            issue(dcur_ref, 0, jj * per)
            return carry
        lax.fori_loop(0, n_j, body, 0)

    nxt = (i + 1) % 2

    def make(g2, sh, sc):
        cur = i % 2
        wait(cur)
        tg = tg_ref[...]
        y = tg[:, 0:1] * gbuf[cur, 0, pl.ds(0, tm), :]
        for k in range(1, TOP_K):
            y = y + tg[:, k:k + 1] * gbuf[cur, k, pl.ds(0, tm), :]
        x = x1_ref[...] + g2 * y
        xn_ref[...] = x
        h_ref[...] = (_rms(x, nw_ref[...]) * (1.0 + sc) + sh).astype(h_ref.dtype)

    @pl.when((j == 0) & (i < n_p))
    def _():
        make(g2p_ref[...], shp_ref[...], scp_ref[...])

    @pl.when((j == 0) & (i >= n_p))
    def _():
        make(_tile_rows(g2s_ref[...], reps), _tile_rows(shs_ref[...], reps), _tile_rows(scs_ref[...], reps))

    issue(dnxt_ref, nxt, j * per)
    res = jnp.dot(h_ref[...], w_ref[...], preferred_element_type=F32)

    @pl.when(j < n_main)
    def _():
        pm_ref[...] = res.astype(pm_ref.dtype)

    @pl.when(j == n_main)
    def _():
        pt_ref[...] = res

    @pl.when((i == n_t - 1) & (j == n_j - 1))
    def _():
        wait(nxt)


def _inproj_combine(dest_flat, eo, x1, topg, ada_prev_p, ada_prev_s, norm_w, ada_p, ada_s, w_pack, layer, dims):
    d, tm, seq, b_p, b_s = dims
    tn = N_TAIL
    t_all = x1.shape[0]
    n_t = t_all // tm
    n_p = b_p * seq // tm
    n_main = N_MAIN // tn
    n_j = n_main + 1
    buf_rows = -(-(_tokens_per_step(tm, n_j) * n_j) // SUBLANES) * SUBLANES
    g2p, g2s = _mod_specs(dims, 5, 2)
    shp, shs = _mod_specs(dims, 0, 2)
    scp, scs = _mod_specs(dims, 1, 2)
    kern = functools.partial(_inproj_c_kernel, n_p=n_p, reps=tm // b_s, n_main=n_main)
    row = lambda n: pl.BlockSpec((tm, n), lambda i, j: (i, 0))
    return pl.pallas_call(
        kern,
        grid=(n_t, n_j),
        in_specs=[pl.BlockSpec((tm * TOP_K,), lambda i, j: (i,), memory_space=pltpu.SMEM),
                  pl.BlockSpec((tm * TOP_K,), lambda i, j: (jnp.minimum(i + 1, n_t - 1),), memory_space=pltpu.SMEM),
                  pl.BlockSpec(memory_space=pl.ANY),
                  row(d), row(LANES), g2p, g2s,
                  pl.BlockSpec((1, d), lambda i, j: (0, 0)),
                  shp, scp, shs, scs,
                  pl.BlockSpec((None, d, tn), lambda i, j: (layer, 0, j))],
        out_specs=[pl.BlockSpec((tm, tn), lambda i, j: (i, jnp.minimum(j, n_main - 1))),
                   pl.BlockSpec((tm, tn), lambda i, j: (i, 0)),
                   row(d), row(d)],
        out_shape=[jax.ShapeDtypeStruct((t_all, N_MAIN), MXU_DTYPE),
                   jax.ShapeDtypeStruct((t_all, N_TAIL), F32),
                   jax.ShapeDtypeStruct((t_all, d), MXU_DTYPE),
                   jax.ShapeDtypeStruct((t_all, d), F32)],
        scratch_shapes=[pltpu.VMEM((2, TOP_K, buf_rows, d), F32), pltpu.SemaphoreType.DMA((2,))],
        compiler_params=_cp("arbitrary", "arbitrary"),
        name="inproj_combine",
    )(dest_flat, dest_flat, eo, x1, topg, ada_prev_p, ada_prev_s, norm_w.reshape(1, d),
      ada_p, ada_p, ada_s, ada_s, w_pack)


def _conva_p_kernel(b_ref, c_ref, x_ref, w_ref, y_ref, st_ref, buf):
    seq = b_ref.shape[0]
    u = c_ref[...].astype(F32) * x_ref[...].astype(F32)
    buf[0:SUBLANES, :] = jnp.zeros((SUBLANES, buf.shape[1]), F32)
    buf[SUBLANES:, :] = u
    w = w_ref[...]
    acc = w[2:3, :] * u
    acc = acc + w[1:2, :] * buf[pl.ds(SUBLANES - 1, seq), :]
    acc = acc + w[0:1, :] * buf[pl.ds(SUBLANES - 2, seq), :]
    y_ref[...] = (b_ref[...].astype(F32) * acc).astype(y_ref.dtype)
    st_ref[...] = buf[pl.ds(seq, SUBLANES), :]


def _conva_prompt(p, conv_w, dims, t_all, tc=256):
    d, tm, seq, b_p, b_s = dims
    nc = d // tc
    return pl.pallas_call(
        _conva_p_kernel,
        grid=(b_p, nc),
        in_specs=[pl.BlockSpec((seq, tc), lambda b, c: (b, C_B // tc + c)),
                  pl.BlockSpec((seq, tc), lambda b, c: (b, C_C // tc + c)),
                  pl.BlockSpec((seq, tc), lambda b, c: (b, C_X // tc + c)),
                  pl.BlockSpec((CONV_A_WIDTH, tc), lambda b, c: (0, c))],
        out_specs=[pl.BlockSpec((seq, tc), lambda b, c: (b, c)),
                   pl.BlockSpec((None, SUBLANES, tc), lambda b, c: (b, 0, c))],
        out_shape=[jax.ShapeDtypeStruct((t_all, d), MXU_DTYPE),
                   jax.ShapeDtypeStruct((b_p, SUBLANES, d), F32)],
        scratch_shapes=[pltpu.VMEM((seq + SUBLANES, tc), F32)],
        compiler_params=_cp("arbitrary", "arbitrary"),
        name="conva_prompt",
    )(p, p, p, conv_w)


def _conva_s_kernel(b_ref, c_ref, x_ref, s_ref, w_ref, y_ref, st_ref, *, b_s, steps):
    ext = jnp.concatenate([s_ref[...], c_ref[...].astype(F32) * x_ref[...].astype(F32)], axis=0)
    w = w_ref[...]
    acc = None
    for k in range(CONV_A_WIDTH):
        t = w[k:k + 1, :] * ext[k * b_s:(k + steps) * b_s]
        acc = t if acc is None else acc + t
    y_ref[...] = (b_ref[...].astype(F32) * acc).astype(y_ref.dtype)
    st_ref[...] = ext[steps * b_s:]


def _conva_sample(p, state_tm, conv_w, dims, steps, tc=256):
    d, tm, seq, b_p, b_s = dims
    t_s = b_s * steps
    rb = (b_p * seq) // t_s
    ns = (CONV_A_WIDTH - 1) * b_s
    kern = functools.partial(_conva_s_kernel, b_s=b_s, steps=steps)
    return pl.pallas_call(
        kern,
        grid=(d // tc,),
        in_specs=[pl.BlockSpec((t_s, tc), lambda c: (rb, C_B // tc + c)),
                  pl.BlockSpec((t_s, tc), lambda c: (rb, C_C // tc + c)),
                  pl.BlockSpec((t_s, tc), lambda c: (rb, C_X // tc + c)),
                  pl.BlockSpec((ns, tc), lambda c: (0, c)),
                  pl.BlockSpec((CONV_A_WIDTH, tc), lambda c: (0, c))],
        out_specs=[pl.BlockSpec((t_s, tc), lambda c: (0, c)),
                   pl.BlockSpec((ns, tc), lambda c: (0, c))],
        out_shape=[jax.ShapeDtypeStruct((t_s, d), MXU_DTYPE),
                   jax.ShapeDtypeStruct((ns, d), F32)],
        compiler_params=_cp("arbitrary"),
        name="conva_sample",
    )(p, p, p, state_tm, conv_w)


def _ssd_p_kernel(z_ref, xs_ref, bc_ref, dt_ref, wx_ref, wbc_ref, bx_ref, bbc_ref, dtb_ref, alog_ref,
                  de_ref, gn_ref, rexp_ref,
                  y_ref, csx_ref, csbc_ref, stout_ref,
                  xbuf, bcbuf, st, ybuf):
    c = pl.program_id(1)
    q = xs_ref.shape[0]
    d_ssm = xs_ref.shape[1]
    half = d_ssm // SSM_GROUPS

    @pl.when(c == 0)
    def _():
        xbuf[0:SUBLANES, :] = jnp.zeros((SUBLANES, xbuf.shape[1]), F32)
        bcbuf[0:SUBLANES, :] = jnp.zeros((SUBLANES, bcbuf.shape[1]), F32)
        st[...] = jnp.zeros(st.shape, F32)

    xbuf[SUBLANES:, :] = xs_ref[...].astype(F32)
    bcbuf[SUBLANES:, :] = bc_ref[...]

    def conv(buf, w, b):
        acc = b
        for k in range(SSM_CONV_WIDTH):
            acc = acc + w[k:k + 1, :] * buf[pl.ds(SUBLANES - (SSM_CONV_WIDTH - 1) + k, q), :]
        return _silu(acc)

    xa = conv(xbuf, wx_ref[...], bx_ref[...])
    bca = conv(bcbuf, wbc_ref[...], bbc_ref[...])
    csx_ref[...] = xbuf[pl.ds(q, SUBLANES), :]
    csbc_ref[...] = bcbuf[pl.ds(q, SUBLANES), :]
    xbuf[0:SUBLANES, :] = xbuf[pl.ds(q, SUBLANES), :]
    bcbuf[0:SUBLANES, :] = bcbuf[pl.ds(q, SUBLANES), :]

    dtv = _softplus(dt_ref[:, 0:LANES] + dtb_ref[...])
    a = -jnp.exp(alog_ref[...])
    da = dtv * a
    row = _iota((q, q), 0)
    col = _iota((q, q), 1)
    causal = row >= col
    tri = jnp.where(causal, 1.0, 0.0).astype(MXU_DTYPE)
    acs = _sel_mm(tri, da)
    acs_t = acs.T
    rexp = rexp_ref[...]
    acs_e = _mm_sel(acs, rexp)
    dt_e = _mm_sel(dtv, rexp, n=2)
    last_e = acs_e[q - 1:q, :]
    exp_e = jnp.exp(acs_e)
    w_e = jnp.exp(last_e - acs_e)
    dec_last = jnp.exp(last_e)

    xdt = xa * dt_e
    xdt_w = xdt * w_e
    bslab = bca[:, 0:LANES]
    cslab = bca[:, LANES:2 * LANES]
    bt = bslab.T
    st_old = st[...]
    y_off = _mm(cslab, st_old) * exp_e
    upd = _mm(bt, xdt_w)
    srow = _iota(st.shape, 0)
    scol = _iota(st.shape, 1)
    diag = (srow >= SSM_STATE) == (scol >= half)
    st[...] = dec_last * st_old + jnp.where(diag, upd, 0.0)

    lane = _iota((q, LANES), 1)
    cbs = []
    for g in range(SSM_GROUPS):
        cm = jnp.where(_half_mask(lane, g == 1), cslab, 0.0)
        cbs.append(_mm(cm, bt))
    heads_per_group = SSM_HEADS // SSM_GROUPS
    for jp in range(SSM_HEADS // 2):
        ms = []
        for hh in (2 * jp, 2 * jp + 1):
            seg = acs[:, hh:hh + 1] - acs_t[hh:hh + 1, :]
            ldec = jnp.exp(jnp.where(causal, seg, NEG_BIG))
            ms.append(cbs[hh // heads_per_group] * ldec)
        lhs = jnp.concatenate(ms, axis=1)
        slab = xdt[:, jp * LANES:(jp + 1) * LANES]
        rhs = jnp.concatenate([jnp.where(lane < SSM_HEAD_DIM, slab, 0.0),
                               jnp.where(lane >= SSM_HEAD_DIM, slab, 0.0)], axis=0)
        ybuf[:, jp * LANES:(jp + 1) * LANES] = _mm(lhs, rhs)

    y = ybuf[...] + y_off + de_ref[...] * xa
    u = y * _silu(z_ref[...].astype(F32))
    outs = []
    for g in range(SSM_GROUPS):
        ug = u[:, g * half:(g + 1) * half]
        outs.append(ug * lax.rsqrt(jnp.mean(ug * ug, axis=-1, keepdims=True) + RMS_EPS))
    y_ref[...] = (jnp.concatenate(outs, axis=1) * gn_ref[...]).astype(y_ref.dtype)

    @pl.when(c == pl.num_programs(1) - 1)
    def _():
        stout_ref[...] = st[...]


def _ssd_prompt(pm, pt, prm, dims, t_all):
    d, tm, seq, b_p, b_s = dims
    q = min(SSD_CHUNK, seq)
    nc = seq // q
    n_bc = 2 * SSM_GROUPS * SSM_STATE
    full = lambda shape: pl.BlockSpec(shape, lambda b, c: tuple(0 for _ in shape))
    return pl.pallas_call(
        _ssd_p_kernel,
        grid=(b_p, nc),
        in_specs=[pl.BlockSpec((q, d), lambda b, c: (b * nc + c, C_Z // d)),
                  pl.BlockSpec((q, d), lambda b, c: (b * nc + c, C_XS // d)),
                  pl.BlockSpec((q, n_bc), lambda b, c: (b * nc + c, T_BC // n_bc)),
                  pl.BlockSpec((q, n_bc), lambda b, c: (b * nc + c, T_DT // n_bc)),
                  full((SSM_CONV_WIDTH, d)), full((SSM_CONV_WIDTH, n_bc)),
                  full((1, d)), full((1, n_bc)), full((1, LANES)), full((1, LANES)),
                  full((1, d)), full((1, d)), full((LANES, d))],
        out_specs=[pl.BlockSpec((q, d), lambda b, c: (b * nc + c, 0)),
                   pl.BlockSpec((None, SUBLANES, d), lambda b, c: (b, 0, 0)),
                   pl.BlockSpec((None, SUBLANES, n_bc), lambda b, c: (b, 0, 0)),
                   pl.BlockSpec((None, 2 * SSM_STATE, d), lambda b, c: (b, 0, 0))],
        out_shape=[jax.ShapeDtypeStruct((t_all, d), MXU_DTYPE),
                   jax.ShapeDtypeStruct((b_p, SUBLANES, d), F32),
                   jax.ShapeDtypeStruct((b_p, SUBLANES, n_bc), F32),
                   jax.ShapeDtypeStruct((b_p, 2 * SSM_STATE, d), F32)],
        scratch_shapes=[pltpu.VMEM((q + SUBLANES, d), F32),
                        pltpu.VMEM((q + SUBLANES, n_bc), F32),
                        pltpu.VMEM((2 * SSM_STATE, d), F32),
                        pltpu.VMEM((q, d), F32)],
        compiler_params=_cp("arbitrary", "arbitrary"),
        name="ssd_prompt",
    )(pm, pm, pt, pt, prm["wx"], prm["wbc"], prm["bx"], prm["bbc"], prm["dtb"], prm["alog"],
      prm["de"], prm["gn"], prm["rexp"])


def _ssd_s_kernel(xs_ref, bc_ref, dt_ref, cx_ref, cbc_ref, wx_ref, wbc_ref, bx_ref, bbc_ref, dtb_ref,
                  alog_ref, de_ref, e2_ref, g2_ref, s_ref,
                  y_ref, snew_ref, csx_ref, csbc_ref, *, b_s, steps):
    j = pl.program_id(0)
    pn = SSM_HEAD_DIM * SSM_STATE
    extx = jnp.concatenate([cx_ref[...], xs_ref[...].astype(F32)], axis=0)
    extbc = jnp.concatenate([cbc_ref[...], bc_ref[...]], axis=0)

    def conv(ext, w, b):
        acc = b
        for k in range(SSM_CONV_WIDTH):
            acc = acc + w[k:k + 1, :] * ext[k * b_s:(k + steps) * b_s]
        return _silu(acc)

    xa = conv(extx, wx_ref[...], bx_ref[...])
    bca = conv(extbc, wbc_ref[...], bbc_ref[...])
    csx_ref[...] = extx[steps * b_s:]
    csbc_ref[...] = extbc[steps * b_s:]
    dtv = _softplus(dt_ref[:, 0:LANES] + dtb_ref[...])
    dec = jnp.exp(dtv * (-jnp.exp(alog_ref[...])))
    lane = _iota(dtv.shape, 1)
    grp = (2 * j) // (SSM_HEADS // SSM_GROUPS)
    own = (lane // SSM_STATE) == grp
    bslab = bca[:, 0:LANES]
    cslab = bca[:, LANES:2 * LANES]
    b2 = jnp.where(own, bslab, pltpu.roll(bslab, SSM_STATE, 1))
    c2 = jnp.where(own, cslab, pltpu.roll(cslab, SSM_STATE, 1))
    ys = [None] * steps
    for lh in range(2):
        hh = 2 * j + lh
        dt_col = jnp.sum(jnp.where(lane == hh, dtv, 0.0), axis=1, keepdims=True)
        dec_col = jnp.sum(jnp.where(lane == hh, dec, 0.0), axis=1, keepdims=True)
        s = s_ref[:, lh * pn:(lh + 1) * pn]
        for t in range(steps):
            rows = slice(t * b_s, (t + 1) * b_s)
            xe = _mm_sel(xa[rows] * dt_col[rows], e2_ref[lh], n=2)
            s = dec_col[rows] * s + xe * _tile_lanes(b2[rows], pn // LANES)
            yt = _mm(s * _tile_lanes(c2[rows], pn // LANES), g2_ref[lh])
            ys[t] = yt if ys[t] is None else ys[t] + yt
        snew_ref[:, lh * pn:(lh + 1) * pn] = s
    y_ref[...] = jnp.concatenate(ys, axis=0) + de_ref[...] * xa


def _ssd_sample(pm, pt, cs_x, cs_bc, state3d, layer, prm, dims, steps):
    d, tm, seq, b_p, b_s = dims
    t_s = b_s * steps
    rb = (b_p * seq) // t_s
    n_bc = 2 * SSM_GROUPS * SSM_STATE
    pn = SSM_HEAD_DIM * SSM_STATE
    ns = (SSM_CONV_WIDTH - 1) * b_s
    kern = functools.partial(_ssd_s_kernel, b_s=b_s, steps=steps)
    full = lambda shape: pl.BlockSpec(shape, lambda j: tuple(0 for _ in shape))
    return pl.pallas_call(
        kern,
        grid=(SSM_HEADS // 2,),
        in_specs=[pl.BlockSpec((t_s, LANES), lambda j: (rb, C_XS // LANES + j)),
                  pl.BlockSpec((t_s, n_bc), lambda j: (rb, T_BC // n_bc)),
                  pl.BlockSpec((t_s, n_bc), lambda j: (rb, T_DT // n_bc)),
                  pl.BlockSpec((ns, LANES), lambda j: (0, j)),
                  full((ns, n_bc)),
                  pl.BlockSpec((SSM_CONV_WIDTH, LANES), lambda j: (0, j)),
                  full((SSM_CONV_WIDTH, n_bc)),
                  pl.BlockSpec((1, LANES), lambda j: (0, j)),
                  full((1, n_bc)), full((1, LANES)), full((1, LANES)),
                  pl.BlockSpec((1, LANES), lambda j: (0, j)),
                  full((2, LANES, pn)), full((2, pn, LANES)),
                  pl.BlockSpec((None, b_s, 2 * pn), lambda j: (layer, 0, j))],
        out_specs=[pl.BlockSpec((t_s, LANES), lambda j: (0, j)),
                   pl.BlockSpec((b_s, 2 * pn), lambda j: (0, j)),
                   pl.BlockSpec((ns, LANES), lambda j: (0, j)),
                   full((ns, n_bc))],
        out_shape=[jax.ShapeDtypeStruct((t_s, d), F32),
                   jax.ShapeDtypeStruct(state3d.shape[1:], F32),
                   jax.ShapeDtypeStruct((ns, d), F32),
                   jax.ShapeDtypeStruct((ns, n_bc), F32)],
        compiler_params=_cp("arbitrary"),
        name="ssd_sample",
    )(pm, pt, pt, cs_x, cs_bc, prm["wx"], prm["wbc"], prm["bx"], prm["bbc"], prm["dtb"], prm["alog"],
      prm["de"], prm["e2"], prm["g2"], state3d)


def _gnorm_s_kernel(y_ref, z_ref, gn_ref, o_ref):
    u = y_ref[...] * _silu(z_ref[...].astype(F32))
    half = u.shape[1] // SSM_GROUPS
    outs = []
    for g in range(SSM_GROUPS):
        ug = u[:, g * half:(g + 1) * half]
        outs.append(ug * lax.rsqrt(jnp.mean(ug * ug, axis=-1, keepdims=True) + RMS_EPS))
    o_ref[...] = (jnp.concatenate(outs, axis=1) * gn_ref[...]).astype(o_ref.dtype)


def _gnorm_sample(y_raw, p, gn, dims, steps):
    d, tm, seq, b_p, b_s = dims
    t_s = b_s * steps
    rb = (b_p * seq) // t_s
    return pl.pallas_call(
        _gnorm_s_kernel,
        grid=(1,),
        in_specs=[pl.BlockSpec((t_s, d), lambda i: (0, 0)),
                  pl.BlockSpec((t_s, d), lambda i: (rb, C_Z // d)),
                  pl.BlockSpec((1, d), lambda i: (0, 0))],
        out_specs=pl.BlockSpec((t_s, d), lambda i: (0, 0)),
        out_shape=jax.ShapeDtypeStruct((t_s, d), MXU_DTYPE),
        compiler_params=_cp("arbitrary"),
        name="gnorm_sample",
    )(y_raw, p, gn)


def _rope(x, cos_f, sin_a, sin_b):
    n = x.shape[1] // cos_f.shape[1]
    half = ROPE_DIMS // 2
    return (x * _tile_lanes(cos_f, n) + pltpu.roll(x, half, 1) * _tile_lanes(sin_a, n)
            + pltpu.roll(x, x.shape[1] - half, 1) * _tile_lanes(sin_b, n))


def _attn_p_kernel(sink_ref, q_ref, k_ref, v_ref, cos_ref, sa_ref, sb_ref,
                   o_ref, kn_ref, vn_ref, kprev, vprev):
    c = pl.program_id(1)
    w = q_ref.shape[0]
    cos_f, sin_a, sin_b = cos_ref[...], sa_ref[...], sb_ref[...]
    scale = HEAD_DIM ** -0.5
    k = _rope(k_ref[...], cos_f, sin_a, sin_b)
    v = v_ref[...]

    @pl.when(c == 0)
    def _():
        kprev[...] = jnp.zeros(kprev.shape, F32)
        vprev[...] = jnp.zeros(vprev.shape, F32)

    kk = jnp.concatenate([kprev[...], k], axis=0)
    vv = jnp.concatenate([vprev[...], v], axis=0)
    key = _iota((2 * w, w), 0)
    qry = _iota((2 * w, w), 1)
    valid = (key > qry) & (key <= qry + w) & ((c > 0) | (key >= w))
    valid = _tile_lanes(valid.astype(F32), Q_PER_KV) > 0.5
    lane_k = _iota((2 * w, LANES), 1)
    sub_v = _iota((LANES, 2 * w), 0)
    for a in range(KV_HEADS // 2):
        slabs = range(a * Q_PER_KV, (a + 1) * Q_PER_KV)
        qst = jnp.concatenate(
            [(_rope(q_ref[:, LANES * j:LANES * (j + 1)].astype(F32), cos_f, sin_a, sin_b) * scale).astype(MXU_DTYPE)
             for j in slabs], axis=0)
        ksl = kk[:, LANES * a:LANES * (a + 1)]
        vt = vv[:, LANES * a:LANES * (a + 1)].T
        o_t = None
        for hk in range(2):
            ks = jnp.where(_half_mask(lane_k, hk == 1), ksl, 0.0).astype(MXU_DTYPE)
            vs = jnp.where(_half_mask(sub_v, hk == 1), vt, 0.0).astype(MXU_DTYPE)
            sink = jnp.concatenate([jnp.full((1, w), sink_ref[HEAD_ORDER[2 * j + hk]], F32) for j in slabs], axis=1)
            s = lax.dot_general(ks, qst, (((1,), (1,)), ((), ())), preferred_element_type=F32)
            s = jnp.where(valid, s, NEG_BIG)
            m = jnp.maximum(jnp.max(s, axis=0, keepdims=True), sink)
            pr = jnp.exp(s - m)
            den = jnp.sum(pr, axis=0, keepdims=True) + jnp.exp(sink - m)
            pn = (pr * (1.0 / den)).astype(MXU_DTYPE)
            o = jnp.dot(vs, pn, preferred_element_type=F32)
            o_t = o if o_t is None else o_t + o
        o_all = o_t.T
        for g, j in enumerate(slabs):
            o_ref[:, LANES * j:LANES * (j + 1)] = o_all[g * w:(g + 1) * w].astype(o_ref.dtype)
    kprev[...] = k
    vprev[...] = v
    kn_ref[...] = k
    vn_ref[...] = v


def _attn_prompt(pm, pt, sinks, tabs, dims, t_all):
    d, tm, seq, b_p, b_s = dims
    w = WINDOW
    nb = seq // w
    d_kv = KV_HEADS * HEAD_DIM
    tab_spec = pl.BlockSpec((w, LANES), lambda b, c: (c, 0))
    const = lambda shape: pl.BlockSpec(shape, lambda b, c: tuple(0 for _ in shape))
    return pl.pallas_call(
        _attn_p_kernel,
        grid=(b_p, nb),
        in_specs=[pl.BlockSpec(memory_space=pltpu.SMEM),
                  pl.BlockSpec((w, d), lambda b, c: (b * nb + c, C_Q // d)),
                  pl.BlockSpec((w, d_kv), lambda b, c: (b * nb + c, T_K // d_kv)),
                  pl.BlockSpec((w, d_kv), lambda b, c: (b * nb + c, T_V // d_kv)),
                  tab_spec, tab_spec, tab_spec],
        out_specs=[pl.BlockSpec((w, d), lambda b, c: (b * nb + c, 0)),
                   pl.BlockSpec((None, w, d_kv), lambda b, c: (b, 0, 0)),
                   pl.BlockSpec((None, w, d_kv), lambda b, c: (b, 0, 0))],
        out_shape=[jax.ShapeDtypeStruct((t_all, d), MXU_DTYPE),
                   jax.ShapeDtypeStruct((b_p, w, d_kv), F32),
                   jax.ShapeDtypeStruct((b_p, w, d_kv), F32)],
        scratch_shapes=[pltpu.VMEM((w, d_kv), F32), pltpu.VMEM((w, d_kv), F32)],
        compiler_params=_cp("arbitrary", "arbitrary"),
        name="attn_prompt",
    )(sinks, pm, pt, pt, *tabs)


def _rope_s_kernel(q_ref, k_ref, cos_ref, sa_ref, sb_ref, qo_ref, ko_ref):
    cos_f, sin_a, sin_b = cos_ref[...], sa_ref[...], sb_ref[...]
    qo_ref[...] = _rope(q_ref[...].astype(F32), cos_f, sin_a, sin_b)
    ko_ref[...] = _rope(k_ref[...], cos_f, sin_a, sin_b)


def _rope_sample(pm, pt, tabs, dims, steps):
    d, tm, seq, b_p, b_s = dims
    t_s = b_s * steps
    rb = (b_p * seq) // t_s
    d_kv = KV_HEADS * HEAD_DIM
    tab_spec = pl.BlockSpec((t_s, LANES), lambda i: (0, 0))
    return pl.pallas_call(
        _rope_s_kernel,
        grid=(1,),
        in_specs=[pl.BlockSpec((t_s, d), lambda i: (rb, C_Q // d)),
                  pl.BlockSpec((t_s, d_kv), lambda i: (rb, T_K // d_kv)),
                  tab_spec, tab_spec, tab_spec],
        out_specs=[pl.BlockSpec((t_s, d), lambda i: (0, 0)),
                   pl.BlockSpec((t_s, d_kv), lambda i: (0, 0))],
        out_shape=[jax.ShapeDtypeStruct((t_s, d), F32),
                   jax.ShapeDtypeStruct((t_s, d_kv), F32)],
        compiler_params=_cp("arbitrary"),
        name="rope_sample",
    )(pm, pt, *tabs)


def _attn_s_kernel(q_ref, kn_ref, vn_ref, ko_ref, vo_ref, sink_ref, valid_ref, o_ref):
    tb, nq, d_kv = q_ref.shape
    w = kn_ref.shape[1]
    n_old = ko_ref.shape[1]
    pad = jnp.zeros((w - n_old, d_kv), F32)
    valid = valid_ref[...] > 0.5
    sink = sink_ref[:, 0:1]
    scale = HEAD_DIM ** -0.5
    orow = _iota((nq, HEAD_DIM), 0)
    rows_per_kv = nq // KV_HEADS
    for b in range(tb):
        kk = jnp.concatenate([kn_ref[b], ko_ref[b], pad], axis=0)
        vv = jnp.concatenate([vn_ref[b], vo_ref[b], pad], axis=0)
        s = _mm_nt(q_ref[b], kk) * scale
        s = jnp.where(valid, s, NEG_BIG)
        m = jnp.maximum(jnp.max(s, axis=-1, keepdims=True), sink)
        pr = jnp.exp(s - m)
        den = jnp.sum(pr, axis=-1, keepdims=True) + jnp.exp(sink - m)
        o = _mm(pr, vv) / den
        acc = jnp.zeros((nq, HEAD_DIM), F32)
        for kh in range(KV_HEADS):
            mine = (orow >= kh * rows_per_kv) & (orow < (kh + 1) * rows_per_kv)
            acc = acc + jnp.where(mine, o[:, kh * HEAD_DIM:(kh + 1) * HEAD_DIM], 0.0)
        o_ref[b] = acc


def _attn_sample(qbd, k_cache, v_cache, k_new, v_new, layer, sink_col, steps, tb=8):
    b_s, nq, d_kv = qbd.shape
    w = k_cache.shape[2]
    t = (jnp.arange(nq) % steps)[:, None]
    col = jnp.arange(2 * w)[None, :]
    valid = jnp.where(col < w, col > t, col - w <= t).astype(F32)
    return pl.pallas_call(
        _attn_s_kernel,
        grid=(b_s // tb,),
        in_specs=[pl.BlockSpec((tb, nq, d_kv), lambda i: (i, 0, 0)),
                  pl.BlockSpec((None, tb, w, d_kv), lambda i: (layer, i, 0, 0)),
                  pl.BlockSpec((None, tb, w, d_kv), lambda i: (layer, i, 0, 0)),
                  pl.BlockSpec((tb, SUBLANES, d_kv), lambda i: (i, 0, 0)),
                  pl.BlockSpec((tb, SUBLANES, d_kv), lambda i: (i, 0, 0)),
                  pl.BlockSpec((nq, LANES), lambda i: (0, 0)),
                  pl.BlockSpec((nq, 2 * w), lambda i: (0, 0))],
        out_specs=pl.BlockSpec((tb, nq, HEAD_DIM), lambda i: (i, 0, 0)),
        out_shape=jax.ShapeDtypeStruct((b_s, nq, HEAD_DIM), F32),
        compiler_params=_cp("arbitrary"),
        name="attn_sample",
    )(qbd, k_cache, v_cache, k_new, v_new, sink_col, valid)


def _merge_kernel(h_ref, yap_ref, ybp_ref, ycp_ref, yas_ref, ybs_ref, ycs_ref, x_ref,
                  wg_ref, bg_ref, pa_ref, pb_ref, pc_ref, wo_ref,
                  g1p_ref, g1s_ref, nw_ref, shp_ref, scp_ref, shs_ref, scs_ref, rw_ref, rb_ref,
                  x1_ref, h2_ref, ti_ref, tg_ref, *, n_p, reps):
    i = pl.program_id(0)
    d = x_ref.shape[1]
    is_p = i < n_p
    ya = jnp.where(is_p, yap_ref[...], yas_ref[...])
    yb = jnp.where(is_p, ybp_ref[...], ybs_ref[...])
    yc = jnp.where(is_p, ycp_ref[...], ycs_ref[...])
    g = _sigmoid(jnp.dot(h_ref[...], wg_ref[...], preferred_element_type=F32) + bg_ref[...])
    merged = (g[:, 0:d] * _mm(ya, pa_ref[...]) + g[:, d:2 * d] * _mm(yb, pb_ref[...])
              + g[:, 2 * d:3 * d] * _mm(yc, pc_ref[...]))
    mix = _mm(merged, wo_ref[...])

    def finish(g1, sh, sc):
        x1 = x_ref[...] + g1 * mix
        x1_ref[...] = x1
        h2_ref[...] = _rms(x1, nw_ref[...]) * (1.0 + sc) + sh

    @pl.when(i < n_p)
    def _():
        finish(g1p_ref[...], shp_ref[...], scp_ref[...])

    @pl.when(i >= n_p)
    def _():
        finish(_tile_rows(g1s_ref[...], reps), _tile_rows(shs_ref[...], reps), _tile_rows(scs_ref[...], reps))

    h2 = h2_ref[...]
    rw = rw_ref[...]
    hs = _split(h2, 2)
    ws = _split(rw, 2)
    if len(hs) == 1:
        logits = jnp.dot(h2, rw, preferred_element_type=F32)
    else:
        logits = (jnp.dot(hs[0], ws[0], preferred_element_type=F32)
                  + jnp.dot(hs[0], ws[1], preferred_element_type=F32)
                  + jnp.dot(hs[1], ws[0], preferred_element_type=F32))
    logits = logits + rb_ref[...]
    lane = _iota(logits.shape, 1).astype(F32)
    vals, idxs = [], []
    cur = logits
    for _ in range(TOP_K):
        mx = jnp.max(cur, axis=-1, keepdims=True)
        ix = jnp.min(jnp.where(cur == mx, lane, float(LANES)), axis=-1, keepdims=True)
        vals.append(mx)
        idxs.append(ix)
        cur = jnp.where(lane == ix, -jnp.inf, cur)
    es = [jnp.exp(v - vals[0]) for v in vals]
    den = es[0]
    for e in es[1:]:
        den = den + e
    ti = jnp.zeros(logits.shape, F32)
    tg = jnp.zeros(logits.shape, F32)
    for k in range(TOP_K):
        ti = jnp.where(lane == float(k), idxs[k], ti)
        tg = jnp.where(lane == float(k), es[k] / den, tg)
    ti_ref[...] = ti.astype(I32)
    tg_ref[...] = tg


def _merge(h, ys_p, ys_s, x_all, wts, ada_p, ada_s, dims, tm_m=512):
    d, tm, seq, b_p, b_s = dims
    t_all = x_all.shape[0]
    tm_m = min(tm_m, tm)
    mdims = (d, tm_m, seq, b_p, b_s)
    n_p = b_p * seq // tm_m
    g1p, g1s = _mod_specs(mdims, 2, 1)
    shp, shs = _mod_specs(mdims, 3, 1)
    scp, scs = _mod_specs(mdims, 4, 1)
    row = lambda n: pl.BlockSpec((tm_m, n), lambda i: (i, 0))
    row_p = pl.BlockSpec((tm_m, d), lambda i: (jnp.minimum(i, n_p - 1), 0))
    row_s = pl.BlockSpec((tm_m, d), lambda i: (jnp.maximum(i - n_p, 0), 0))
    const = lambda shape: pl.BlockSpec(shape, lambda i: tuple(0 for _ in shape), pipeline_mode=pl.Buffered(1))
    kern = functools.partial(_merge_kernel, n_p=n_p, reps=max(tm_m // b_s, 1))
    return pl.pallas_call(
        kern,
        grid=(t_all // tm_m,),
        in_specs=[row(d), row_p, row_p, row_p, row_s, row_s, row_s, row(d),
                  const((d, 3 * d)), const((1, 3 * d)), const((d, d)), const((d, d)), const((d, d)),
                  const((d, d)),
                  g1p, g1s, const((1, d)), shp, scp, shs, scs,
                  const((d, LANES)), const((1, LANES))],
        out_specs=[row(d), row(d), row(LANES), row(LANES)],
        out_shape=[jax.ShapeDtypeStruct((t_all, d), F32),
                   jax.ShapeDtypeStruct((t_all, d), F32),
                   jax.ShapeDtypeStruct((t_all, LANES), I32),
                   jax.ShapeDtypeStruct((t_all, LANES), F32)],
        compiler_params=_cp("arbitrary"),
        name="merge",
    )(h, *ys_p, *ys_s, x_all, wts["wg"], wts["bg"], wts["pa"], wts["pb"], wts["pc"], wts["wo"],
      ada_p, ada_s, wts["nffn"], ada_p, ada_p, ada_s, ada_s, wts["rw"], wts["rb"])


def _rank_kernel(ti_ref, dest_ref, cnt_ref, run, tot):
    ph = pl.program_id(0)
    i = pl.program_id(1)
    tm = ti_ref.shape[0]
    ti = ti_ref[...]
    lane = _iota((tm, LANES), 1)
    oh = jnp.zeros((tm, LANES), F32)
    for k in range(TOP_K):
        oh = oh + jnp.where(lane == ti[:, k:k + 1], 1.0, 0.0)
    ones = jnp.ones((SUBLANES, tm), MXU_DTYPE)
    colsum = jnp.dot(ones, oh.astype(MXU_DTYPE), preferred_element_type=F32)[0:1, :]

    @pl.when((ph == 0) & (i == 0))
    def _():
        tot[...] = jnp.zeros(tot.shape, F32)

    @pl.when(ph == 0)
    def _():
        tot[...] = tot[...] + colsum

    @pl.when((ph == 1) & (i == 0))
    def _():
        run[...] = jnp.zeros(run.shape, F32)

    @pl.when(ph == 1)
    def _():
        r = _iota((LANES, LANES), 0)
        c = _iota((LANES, LANES), 1)
        upper = jnp.where(r < c, 1.0, 0.0).astype(MXU_DTYPE)
        starts = _mm_sel(jnp.broadcast_to(tot[...], (SUBLANES, LANES)), upper)[0:1, :]
        rr = _iota((tm, tm), 0)
        cc = _iota((tm, tm), 1)
        lower = jnp.where(rr > cc, 1.0, 0.0).astype(MXU_DTYPE)
        pre = jnp.dot(lower, oh.astype(MXU_DTYPE), preferred_element_type=F32)
        pos = pre + run[...] + starts
        dest = jnp.zeros((tm, LANES), F32)
        for k in range(TOP_K):
            dk = jnp.sum(jnp.where(lane == ti[:, k:k + 1], pos, 0.0), axis=1, keepdims=True)
            dest = jnp.where(lane == k, dk, dest)
        dest_ref[...] = dest.astype(I32)
        run[...] = run[...] + colsum

    cnt_ref[...] = jnp.broadcast_to(tot[...], cnt_ref.shape)


def _rank(topi, tm):
    t_all = topi.shape[0]
    return pl.pallas_call(
        _rank_kernel,
        grid=(2, t_all // tm),
        in_specs=[pl.BlockSpec((tm, LANES), lambda ph, i: (i, 0))],
        out_specs=[pl.BlockSpec((tm, LANES), lambda ph, i: (i * ph, 0)),
                   pl.BlockSpec((SUBLANES, LANES), lambda ph, i: (0, 0))],
        out_shape=[jax.ShapeDtypeStruct((t_all, LANES), I32),
                   jax.ShapeDtypeStruct((SUBLANES, LANES), F32)],
        scratch_shapes=[pltpu.VMEM((1, LANES), F32), pltpu.VMEM((1, LANES), F32)],
        compiler_params=_cp("arbitrary", "arbitrary"),
        name="rank",
    )(topi)


def _dispatch_kernel(dest_ref, h2_ref, xs_ref, sem):
    tm = h2_ref.shape[0]

    def issue(r, carry):
        for k in range(TOP_K):
            pltpu.make_async_copy(h2_ref.at[pl.ds(r, 1)], xs_ref.at[pl.ds(dest_ref[r * TOP_K + k], 1)],
                                  sem).start()
        return carry

    lax.fori_loop(0, tm, issue, 0)
    for _ in range(TOP_K):
        pltpu.make_async_copy(h2_ref, xs_ref.at[pl.ds(0, tm)], sem).wait()


def _dispatch(dest_flat, h2, tm=256):
    t_all, d = h2.shape
    return pl.pallas_call(
        _dispatch_kernel,
        grid=(t_all // tm,),
        in_specs=[pl.BlockSpec((tm * TOP_K,), lambda i: (i,), memory_space=pltpu.SMEM),
                  pl.BlockSpec((tm, d), lambda i: (i, 0))],
        out_specs=pl.BlockSpec(memory_space=pl.ANY),
        out_shape=jax.ShapeDtypeStruct((t_all * TOP_K, d), F32),
        scratch_shapes=[pltpu.SemaphoreType.DMA(())],
        compiler_params=_cp("arbitrary"),
        name="dispatch",
    )(dest_flat, h2)


def _expert_kernel(tile_ref, exp_ref, lo_ref, hi_ref, firste_ref, firstt_ref, valid_ref, slot_ref, nexte_ref,
                   x_ref, wgu_hbm, bgu_ref, wd_hbm, bd_ref, o_ref, wgu_f, wd_f, wgu_s, wd_s, sem, *, layer):
    w = pl.program_id(0)
    tmx = x_ref.shape[0]
    f = wd_s.shape[0]

    def fetch(e, slot):
        return (pltpu.make_async_copy(wgu_hbm.at[layer, e], wgu_f.at[slot], sem.at[slot]),
                pltpu.make_async_copy(wd_hbm.at[layer, e], wd_f.at[slot], sem.at[slot]))

    @pl.when(w == 0)
    def _():
        for cp in fetch(exp_ref[0], slot_ref[0]):
            cp.start()

    @pl.when(firste_ref[w] == 1)
    def _():
        slot = slot_ref[w]
        for cp in fetch(exp_ref[w], slot):
            cp.wait()
        wgu_s[...] = wgu_f[slot].astype(wgu_s.dtype)
        wd_s[...] = wd_f[slot].astype(wd_s.dtype)

        @pl.when(nexte_ref[w] >= 0)
        def _():
            for cp in fetch(nexte_ref[w], 1 - slot):
                cp.start()

    @pl.when(valid_ref[w] == 1)
    def _():
        gu = jnp.dot(x_ref[...].astype(MXU_DTYPE), wgu_s[...], preferred_element_type=F32) + bgu_ref[...]
        g = jnp.minimum(gu[:, 0:f], SWIGLU_LIMIT)
        u = jnp.clip(gu[:, f:2 * f], -SWIGLU_LIMIT, SWIGLU_LIMIT)
        act = g * _sigmoid(SWIGLU_ALPHA * g) * (u + 1.0)
        res = jnp.dot(act.astype(MXU_DTYPE), wd_s[...], preferred_element_type=F32) + bd_ref[...]
        rows = tile_ref[w] * tmx + _iota((tmx, 1), 0)
        mine = (rows >= lo_ref[w]) & (rows < hi_ref[w])

        @pl.when(firstt_ref[w] == 1)
        def _():
            o_ref[...] = jnp.where(mine, res, 0.0)

        @pl.when(firstt_ref[w] == 0)
        def _():
            o_ref[...] = jnp.where(mine, res, o_ref[...])


def _experts(meta, xs, w_gu, b_gu, w_d, b_d, layer, tmx):
    tk, d = xs.shape
    depth, n_e, _, f2 = w_gu.shape
    f = f2 // 2
    n_w = meta[0].shape[0]
    grid_spec = pltpu.PrefetchScalarGridSpec(
        num_scalar_prefetch=9,
        grid=(n_w,),
        in_specs=[pl.BlockSpec((tmx, d), lambda w, tl, ex, *_: (tl[w], 0)),
                  pl.BlockSpec(memory_space=pl.ANY),
                  pl.BlockSpec((None, None, 1, f2), lambda w, tl, ex, *_: (layer, ex[w], 0, 0)),
                  pl.BlockSpec(memory_space=pl.ANY),
                  pl.BlockSpec((None, None, 1, d), lambda w, tl, ex, *_: (layer, ex[w], 0, 0))],
        out_specs=pl.BlockSpec((tmx, d), lambda w, tl, ex, *_: (tl[w], 0)),
        scratch_shapes=[pltpu.VMEM((2, d, f2), F32), pltpu.VMEM((2, f, d), F32),
                        pltpu.VMEM((d, f2), MXU_DTYPE), pltpu.VMEM((f, d), MXU_DTYPE),
                        pltpu.SemaphoreType.DMA((2,))],
    )
    return pl.pallas_call(
        functools.partial(_expert_kernel, layer=layer),
        grid_spec=grid_spec,
        out_shape=jax.ShapeDtypeStruct((tk, d), F32),
        compiler_params=_cp("arbitrary"),
        name="experts",
    )(*meta, xs, w_gu, b_gu.reshape(depth, n_e, 1, f2), w_d, b_d.reshape(depth, n_e, 1, d))


def _expert_schedule(counts, tk, tmx):
    n_e = counts.shape[0]
    n_w = tk // tmx + n_e - 1
    e_ids = jnp.arange(n_e, dtype=I32)
    tri = (e_ids[:, None] <= e_ids[None, :]).astype(I32)
    ends = jnp.sum(counts[:, None] * tri, axis=0)
    starts = ends - counts
    first_tile = starts // tmx
    last_tile = jnp.maximum(ends - 1, 0) // tmx
    n_t = jnp.where(counts > 0, last_tile - first_tile + 1, 0)
    cum = jnp.sum(n_t[:, None] * tri, axis=0)
    total = jnp.sum(n_t)

    def item(wv):
        ex = jnp.sum((cum[None, :] <= wv[:, None]).astype(I32), axis=1)
        oh = (ex[:, None] == e_ids[None, :]).astype(I32)
        pick = lambda v: jnp.sum(oh * v[None, :], axis=1)
        tile = pick(first_tile) + wv - (pick(cum) - pick(n_t))
        return ex, tile, pick(starts), pick(ends)

    w = jnp.arange(n_w, dtype=I32)
    wc = jnp.minimum(w, total - 1)
    ex, tile, lo, hi = item(wc)
    ex_prev, tile_prev, _, _ = item(jnp.maximum(wc - 1, 0))
    valid = w < total
    first_e = valid & ((w == 0) | (ex != ex_prev))
    first_t = valid & ((w == 0) | (tile != tile_prev))
    nonempty = (n_t > 0).astype(I32)
    ordinal = jnp.sum(nonempty[:, None] * tri, axis=0) - nonempty
    later = (e_ids[None, :] > e_ids[:, None]) & (nonempty[None, :] == 1)
    nxt = jnp.min(jnp.where(later, e_ids[None, :], n_e), axis=1)
    nxt = jnp.where(nxt == n_e, -1, nxt)
    oh = (ex[:, None] == e_ids[None, :]).astype(I32)
    slot = jnp.sum(oh * (ordinal % 2)[None, :], axis=1)
    next_e = jnp.sum(oh * nxt[None, :], axis=1)
    return (tile, ex, lo, hi, first_e.astype(I32), first_t.astype(I32), valid.astype(I32),
            slot.astype(I32), next_e.astype(I32))


def _combine_kernel(dest_ref, eo_ref, x1_ref, tg_ref, g2p_ref, g2s_ref, nf_ref, op_ref, os_ref, buf, sem, *,
                    n_p, reps):
    i = pl.program_id(0)
    tm = x1_ref.shape[0]

    def issue(r, carry):
        for k in range(TOP_K):
            pltpu.make_async_copy(eo_ref.at[pl.ds(dest_ref[r * TOP_K + k], 1)], buf.at[k, pl.ds(r, 1)],
                                  sem).start()
        return carry

    lax.fori_loop(0, tm, issue, 0)
    for k in range(TOP_K):
        pltpu.make_async_copy(eo_ref.at[pl.ds(0, tm)], buf.at[k], sem).wait()
    tg = tg_ref[...]
    y = tg[:, 0:1] * buf[0]
    for k in range(1, TOP_K):
        y = y + tg[:, k:k + 1] * buf[k]

    def finish(g2, o_ref):
        o_ref[...] = _rms(x1_ref[...] + g2 * y, nf_ref[...])

    @pl.when(i < n_p)
    def _():
        finish(g2p_ref[...], op_ref)

    @pl.when(i >= n_p)
    def _():
        finish(_tile_rows(g2s_ref[...], reps), os_ref)


def _combine(dest_flat, eo, x1, topg, ada_p, ada_s, norm_final, dims, tm_c=256):
    d, tm, seq, b_p, b_s = dims
    t_all = x1.shape[0]
    tm_c = min(tm_c, tm)
    cdims = (d, tm_c, seq, b_p, b_s)
    n_p = b_p * seq // tm_c
    g2p, g2s = _mod_specs(cdims, 5, 1)
    kern = functools.partial(_combine_kernel, n_p=n_p, reps=max(tm_c // b_s, 1))
    return pl.pallas_call(
        kern,
        grid=(t_all // tm_c,),
        in_specs=[pl.BlockSpec((tm_c * TOP_K,), lambda i: (i,), memory_space=pltpu.SMEM),
                  pl.BlockSpec(memory_space=pl.ANY),
                  pl.BlockSpec((tm_c, d), lambda i: (i, 0)),
                  pl.BlockSpec((tm_c, LANES), lambda i: (i, 0)),
                  g2p, g2s,
                  pl.BlockSpec((1, d), lambda i: (0, 0))],
        out_specs=[pl.BlockSpec((tm_c, d), lambda i: (jnp.minimum(i, n_p - 1), 0)),
                   pl.BlockSpec((tm_c, d), lambda i: (jnp.maximum(i - n_p, 0), 0))],
        out_shape=[jax.ShapeDtypeStruct((n_p * tm_c, d), F32),
                   jax.ShapeDtypeStruct((t_all - n_p * tm_c, d), F32)],
        scratch_shapes=[pltpu.VMEM((TOP_K, tm_c, d), F32), pltpu.SemaphoreType.DMA(())],
        compiler_params=_cp("arbitrary"),
        name="combine",
    )(dest_flat, eo, x1, topg, ada_p, ada_s, norm_final.reshape(1, d))


def _rope_tables(pos):
    half = ROPE_DIMS // 2
    inv_freq = jnp.exp(-math.log(ROPE_THETA) * jnp.arange(half, dtype=F32) / half)
    ang = pos.astype(F32)[:, None] * inv_freq[None, :]
    cos, sin = jnp.cos(ang), jnp.sin(ang)
    n = pos.shape[0]
    rest = HEAD_DIM - ROPE_DIMS
    cos_h = jnp.concatenate([cos, cos, jnp.ones((n, rest), F32)], axis=1)
    sa_h = jnp.concatenate([jnp.zeros((n, half), F32), sin, jnp.zeros((n, rest), F32)], axis=1)
    sb_h = jnp.concatenate([-sin, jnp.zeros((n, half + rest), F32)], axis=1)
    rep = LANES // HEAD_DIM
    return tuple(jnp.tile(t, (1, rep)) for t in (cos_h, sa_h, sb_h))


def _pad_lanes(v, n, value=0.0):
    return jnp.pad(v, ((0, 0), (0, n - v.shape[1])), constant_values=value)


def kernel(x_prompt, x_sample, state_conv_a, state_conv_ssm, state_ssm, cache_k, cache_v, c_prompt, c_sample, w_ada, b_ada, norm_mix, norm_ffn, w_in, conv_a_w, ssm_conv_w, ssm_conv_b, ssm_dt_bias, ssm_a_log, ssm_d, ssm_norm, attn_sinks, w_branch_gate, b_branch_gate, w_proj_a, w_proj_b, w_proj_c, w_out, router_w, router_b, w_gate_up, b_gate_up, w_down, b_down, norm_final):
    b_p, seq, d = x_prompt.shape
    b_s, steps, _ = x_sample.shape
    depth = w_ada.shape[0]
    t_p, t_s = b_p * seq, b_s * steps
    t_all = t_p + t_s
    tm = t_s
    dims = (d, tm, seq, b_p, b_s)
    n_e = router_w.shape[2]
    d_ssm = SSM_HEADS * SSM_HEAD_DIM
    n_bc = 2 * SSM_GROUPS * SSM_STATE
    d_kv = KV_HEADS * HEAD_DIM
    pn = SSM_HEAD_DIM * SSM_STATE
    assert d == d_ssm == ATTN_HEADS * HEAD_DIM and t_p % t_s == 0 and seq % t_s == 0
    assert t_s % 256 == 0 or t_s <= 256
    assert steps <= SUBLANES and PAST_LEN >= WINDOW and seq >= WINDOW and b_s % SUBLANES == 0

    x_all = jnp.concatenate([x_prompt.reshape(t_p, d), x_sample.transpose(1, 0, 2).reshape(t_s, d)], axis=0)
    ada = _ada_all(jnp.concatenate([c_prompt, c_sample], axis=0), w_ada, b_ada)

    o_z = 3 * d
    o_bc = o_z + 2 * d_ssm
    o_dt = o_bc + n_bc
    o_q = o_dt + SSM_HEADS
    o_k = o_q + d
    inv_order = [HEAD_ORDER.index(h) for h in range(ATTN_HEADS)]

    def to_head_order(a, axis):
        return jnp.concatenate([lax.slice_in_dim(a, h * HEAD_DIM, (h + 1) * HEAD_DIM, axis=axis)
                                for h in HEAD_ORDER], axis=axis)

    def from_head_order(a, axis):
        return jnp.concatenate([lax.slice_in_dim(a, i * HEAD_DIM, (i + 1) * HEAD_DIM, axis=axis)
                                for i in inv_order], axis=axis)

    w_pack = jnp.concatenate(
        [w_in[:, :, :o_bc], to_head_order(w_in[:, :, o_q:o_k], 2), w_in[:, :, o_bc:o_dt], w_in[:, :, o_k:],
         w_in[:, :, o_dt:o_q], jnp.zeros((depth, d, N_TAIL - T_DT - SSM_HEADS), F32)], axis=2).astype(MXU_DTYPE)

    hp = jnp.arange(d_ssm) // SSM_HEAD_DIM
    rexp = (jnp.arange(LANES)[:, None] == hp[None, :]).astype(MXU_DTYPE)
    lane_pn = jnp.arange(pn)
    e2 = jnp.stack([(jnp.arange(LANES)[:, None] == (lane_pn // SSM_STATE + lh * SSM_HEAD_DIM)[None, :])
                    for lh in range(2)]).astype(MXU_DTYPE)
    g2 = jnp.transpose(e2, (0, 2, 1))
    state_ssm3 = state_ssm.reshape(depth, b_s, SSM_HEADS * pn)
    cache_k4 = cache_k.reshape(depth, b_s, WINDOW, d_kv)
    cache_v4 = cache_v.reshape(depth, b_s, WINDOW, d_kv)

    tabs_p = _rope_tables(jnp.arange(seq, dtype=I32))
    tabs_s = _rope_tables(jnp.repeat(PAST_LEN + jnp.arange(steps, dtype=I32), b_s))
    eye_kv = jnp.eye(KV_HEADS, dtype=F32)

    outs = [[] for _ in range(10)]
    pending = None
    for l in range(depth):
        ada_p = ada[l, :b_p].reshape(b_p, 1, 6 * d)
        ada_s = ada[l, b_p:]
        ssm_prm = dict(
            wx=ssm_conv_w[l, :, :d_ssm], wbc=ssm_conv_w[l, :, d_ssm:],
            bx=ssm_conv_b[l, :d_ssm].reshape(1, d_ssm), bbc=ssm_conv_b[l, d_ssm:].reshape(1, n_bc),
            dtb=_pad_lanes(ssm_dt_bias[l].reshape(1, SSM_HEADS), LANES),
            alog=_pad_lanes(ssm_a_log[l].reshape(1, SSM_HEADS), LANES),
            de=jnp.repeat(ssm_d[l], SSM_HEAD_DIM).reshape(1, d_ssm),
            gn=ssm_norm[l].reshape(1, d_ssm), rexp=rexp, e2=e2, g2=g2)
        wts = dict(
            wg=w_branch_gate[l].astype(MXU_DTYPE), bg=b_branch_gate[l].reshape(1, 3 * d),
            pa=w_proj_a[l].astype(MXU_DTYPE), pb=w_proj_b[l].astype(MXU_DTYPE),
            pc=to_head_order(w_proj_c[l], 0).astype(MXU_DTYPE), wo=w_out[l].astype(MXU_DTYPE),
            nffn=norm_ffn[l].reshape(1, d),
            rw=_pad_lanes(router_w[l], LANES), rb=_pad_lanes(router_b[l].reshape(1, n_e), LANES, NEG_BIG))

        if pending is None:
            p, p_tail, h = _inproj(x_all, norm_mix[l], ada_p, ada_s, w_pack, l, dims)
        else:
            p, p_tail, h, x_all = _inproj_combine(*pending, norm_mix[l], ada_p, ada_s, w_pack, l, dims)

        ya_p, ca_p = _conva_prompt(p, conv_a_w[l], dims, t_p)
        ca_tm = state_conv_a[l].transpose(1, 0, 2).reshape((CONV_A_WIDTH - 1) * b_s, d)
        ya_s, ca_s = _conva_sample(p, ca_tm, conv_a_w[l], dims, steps)
        outs[0].append(ca_p[:, SUBLANES - (CONV_A_WIDTH - 1):])
        outs[1].append(ca_s.reshape(CONV_A_WIDTH - 1, b_s, d).transpose(1, 0, 2))

        yb_p, csx_p, csbc_p, st_p = _ssd_prompt(p, p_tail, ssm_prm, dims, t_p)
        nsc = SSM_CONV_WIDTH - 1
        outs[2].append(jnp.concatenate([csx_p[:, SUBLANES - nsc:], csbc_p[:, SUBLANES - nsc:]], axis=2))
        hg = SSM_HEADS // SSM_GROUPS
        st_g = jnp.stack([st_p[:, g * SSM_STATE:(g + 1) * SSM_STATE, g * (d_ssm // 2):(g + 1) * (d_ssm // 2)]
                          for g in range(SSM_GROUPS)], axis=1)
        outs[4].append(st_g.reshape(b_p, SSM_GROUPS, SSM_STATE, hg, SSM_HEAD_DIM)
                       .transpose(0, 1, 3, 4, 2).reshape(b_p, SSM_HEADS, SSM_HEAD_DIM, SSM_STATE))
        cs_tm = state_conv_ssm[l].transpose(1, 0, 2).reshape(nsc * b_s, d_ssm + n_bc)
        y_raw, st_s, csx_s, csbc_s = _ssd_sample(p, p_tail, cs_tm[:, :d_ssm], cs_tm[:, d_ssm:],
                                                 state_ssm3, l, ssm_prm, dims, steps)
        yb_s = _gnorm_sample(y_raw, p, ssm_prm["gn"], dims, steps)
        outs[3].append(jnp.concatenate([csx_s, csbc_s], axis=1).reshape(nsc, b_s, d_ssm + n_bc).transpose(1, 0, 2))
        outs[5].append(st_s.reshape(b_s, SSM_HEADS, SSM_HEAD_DIM, SSM_STATE))

        yc_p, k_p, v_p = _attn_prompt(p, p_tail, attn_sinks[l], tabs_p, dims, t_p)
        outs[6].append(k_p.reshape(b_p, WINDOW, KV_HEADS, HEAD_DIM))
        outs[8].append(v_p.reshape(b_p, WINDOW, KV_HEADS, HEAD_DIM))
        q_s, k_s = _rope_sample(p, p_tail, tabs_s, dims, steps)
        v_s = p_tail[t_p:, T_V:T_V + d_kv]
        k_rows = k_s.reshape(steps, b_s, d_kv).transpose(1, 0, 2)
        v_rows = v_s.reshape(steps, b_s, d_kv).transpose(1, 0, 2)
        outs[7].append(k_rows)
        outs[9].append(v_rows)
        pad8 = ((0, 0), (0, SUBLANES - steps), (0, 0))
        q4 = from_head_order(q_s, 1).reshape(steps, b_s, KV_HEADS, Q_PER_KV, HEAD_DIM).transpose(1, 2, 3, 0, 4)
        qbd = (q4.reshape(b_s, KV_HEADS, Q_PER_KV * steps, 1, HEAD_DIM)
               * eye_kv[None, :, None, :, None]).reshape(b_s, KV_HEADS * Q_PER_KV * steps, d_kv)
        sink_col = jnp.broadcast_to(jnp.repeat(attn_sinks[l], steps)[:, None], (ATTN_HEADS * steps, LANES))
        o_s = _attn_sample(qbd, cache_k4, cache_v4, jnp.pad(k_rows, pad8), jnp.pad(v_rows, pad8), l,
                           sink_col, steps)
        yc_s = to_head_order(
            o_s.reshape(b_s, KV_HEADS, Q_PER_KV, steps, HEAD_DIM).transpose(3, 0, 1, 2, 4).reshape(t_s, d), 1
        ).astype(MXU_DTYPE)

        x1, h2, topi, topg = _merge(h, (ya_p, yb_p, yc_p), (ya_s, yb_s, yc_s), x_all, wts, ada_p, ada_s, dims)
        dest, cnt = _rank(topi, tm)
        dest_flat = dest[:, :TOP_K].reshape(t_all * TOP_K)
        xs = _dispatch(dest_flat, h2, min(256, tm))
        tmx = min(256, tm)
        meta = _expert_schedule(cnt[0, :n_e].astype(I32), t_all * TOP_K, tmx)
        eo = _experts(meta, xs, w_gate_up, b_gate_up, w_down, b_down, l, tmx)
        pending = (dest_flat, eo, x1, topg, ada_p, ada_s)

    y_p, y_s = _combine(*pending, norm_final, dims)
    y_prompt = y_p.reshape(b_p, seq, d)
    y_sample = y_s.reshape(steps, b_s, d).transpose(1, 0, 2)
    st = [jnp.stack(o) for o in outs]
    k_sample = jnp.concatenate([cache_k[:, :, steps:], st[7].reshape(depth, b_s, steps, KV_HEADS, HEAD_DIM)], axis=2)
    v_sample = jnp.concatenate([cache_v[:, :, steps:], st[9].reshape(depth, b_s, steps, KV_HEADS, HEAD_DIM)], axis=2)
    return (y_prompt, y_sample, st[0], st[1], st[2], st[3], st[4], st[5], st[6], k_sample, st[8], v_sample)
```

```python
import functools
import math

import jax
import jax.numpy as jnp
from jax import lax
from jax.experimental import pallas as pl
from jax.experimental.pallas import tpu as pltpu

F32 = jnp.float32
BF16 = jnp.bfloat16
I32 = jnp.int32

PAST_LEN = 8192
SSM_HEADS = 16
SSM_HEAD_DIM = 64
SSM_GROUPS = 2
SSM_STATE = 64
SSM_CONV_WIDTH = 4
CONV_A_WIDTH = 3
SSD_CHUNK = 128
ATTN_HEADS = 16
KV_HEADS = 4
HEAD_DIM = 64
Q_PER_KV = ATTN_HEADS // KV_HEADS
WINDOW = 128
ROPE_THETA = 500000.0
ROPE_DIMS = HEAD_DIM // 4
N_EXPERTS = 32
TOP_K = 4
SWIGLU_LIMIT = 7.0
SWIGLU_ALPHA = 1.702
RMS_EPS = 1e-5
NEG_BIG = -1e30

LANES = 128
SUBLANES = 8
VMEM_LIMIT = 56 * 1024 * 1024

MXU_DTYPE = BF16
C_B, C_C, C_X, C_Z, C_XS, C_Q, N_MAIN = 0, 1024, 2048, 3072, 4096, 5120, 6144
T_BC, T_K, T_V, T_DT, N_TAIL = 0, 256, 512, 768, 1024
N_COL = N_MAIN + N_TAIL
HEAD_ORDER = tuple(h for j in range(ATTN_HEADS // 2)
                   for h in ((j // Q_PER_KV) * 2 * Q_PER_KV + j % Q_PER_KV,
                             (j // Q_PER_KV) * 2 * Q_PER_KV + Q_PER_KV + j % Q_PER_KV))


def _cp(*sem):
    return pltpu.CompilerParams(dimension_semantics=sem, vmem_limit_bytes=VMEM_LIMIT)


def _mm(a, b):
    return jnp.dot(a.astype(MXU_DTYPE), b.astype(MXU_DTYPE), preferred_element_type=F32)


def _mm_nt(a, b):
    return lax.dot_general(a.astype(MXU_DTYPE), b.astype(MXU_DTYPE), (((1,), (1,)), ((), ())),
                           preferred_element_type=F32)


def _split(v, n):
    if MXU_DTYPE == F32:
        return [v]
    parts, r = [], v
    for _ in range(n):
        p = r.astype(MXU_DTYPE)
        parts.append(p)
        r = r - p.astype(F32)
    return parts


def _mm_sel(v, sel, n=3):
    acc = None
    for p in _split(v, n):
        t = jnp.dot(p, sel, preferred_element_type=F32)
        acc = t if acc is None else acc + t
    return acc


def _sel_mm(sel, v, n=3):
    acc = None
    for p in _split(v, n):
        t = jnp.dot(sel, p, preferred_element_type=F32)
        acc = t if acc is None else acc + t
    return acc


def _rms(x, w):
    return x * lax.rsqrt(jnp.mean(x * x, axis=-1, keepdims=True) + RMS_EPS) * w


def _sigmoid(x):
    return 1.0 / (1.0 + jnp.exp(-x))


def _silu(x):
    return x * _sigmoid(x)


def _softplus(x):
    return jnp.maximum(x, 0.0) + jnp.log(1.0 + jnp.exp(-jnp.abs(x)))


def _tile_rows(x, n):
    return x if n == 1 else jnp.concatenate([x] * n, axis=0)


def _tile_lanes(x, n):
    return x if n == 1 else jnp.concatenate([x] * n, axis=1)


def _iota(shape, dim):
    return lax.broadcasted_iota(I32, shape, dim)


def _half_mask(lane, upper):
    return lane >= HEAD_DIM if upper else lane < HEAD_DIM


def _ada_kernel(c_ref, w_ref, b_ref, o_ref):
    a = _silu(c_ref[...])
    o_ref[...] = _mm(a, w_ref[...]) + b_ref[...]


def _ada_all(c_all, w_ada, b_ada):
    depth, d, n6 = w_ada.shape
    nb = c_all.shape[0]
    tn = d
    return pl.pallas_call(
        _ada_kernel,
        grid=(depth, n6 // tn),
        in_specs=[pl.BlockSpec((nb, d), lambda l, j: (0, 0)),
                  pl.BlockSpec((None, d, tn), lambda l, j: (l, 0, j)),
                  pl.BlockSpec((None, 1, tn), lambda l, j: (l, 0, j))],
        out_specs=pl.BlockSpec((None, nb, tn), lambda l, j: (l, 0, j)),
        out_shape=jax.ShapeDtypeStruct((depth, nb, n6), F32),
        compiler_params=_cp("arbitrary", "arbitrary"),
        name="ada",
    )(c_all, w_ada, b_ada.reshape(depth, 1, n6))


def _inproj_kernel(x_ref, nw_ref, shp_ref, scp_ref, shs_ref, scs_ref, w_ref, pm_ref, pt_ref, h_ref, *,
                   n_p, reps, n_main):
    i = pl.program_id(0)
    j = pl.program_id(1)

    def make_h(sh, sc):
        y = _rms(x_ref[...], nw_ref[...])
        h_ref[...] = (y * (1.0 + sc) + sh).astype(h_ref.dtype)

    @pl.when((j == 0) & (i < n_p))
    def _():
        make_h(shp_ref[...], scp_ref[...])

    @pl.when((j == 0) & (i >= n_p))
    def _():
        make_h(_tile_rows(shs_ref[...], reps), _tile_rows(scs_ref[...], reps))

    res = jnp.dot(h_ref[...], w_ref[...], preferred_element_type=F32)

    @pl.when(j < n_main)
    def _():
        pm_ref[...] = res.astype(pm_ref.dtype)

    @pl.when(j == n_main)
    def _():
        pt_ref[...] = res


def _mod_specs(dims, col, grid_rank):
    d, tm, seq, b_p, b_s = dims

    def pidx(i, *_):
        return (jnp.minimum(i * tm // seq, b_p - 1), 0, col)

    def sidx(i, *_):
        return (0, col)

    return pl.BlockSpec((None, 1, d), pidx), pl.BlockSpec((b_s, d), sidx)


def _inproj(x_all, norm_w, ada_p, ada_s, w_pack, layer, dims):
    d, tm, seq, b_p, b_s = dims
    tn = N_TAIL
    t_all = x_all.shape[0]
    n_t = t_all // tm
    n_p = b_p * seq // tm
    n_main = N_MAIN // tn
    shp, shs = _mod_specs(dims, 0, 2)
    scp, scs = _mod_specs(dims, 1, 2)
    kern = functools.partial(_inproj_kernel, n_p=n_p, reps=tm // b_s, n_main=n_main)
    return pl.pallas_call(
        kern,
        grid=(n_t, n_main + 1),
        in_specs=[pl.BlockSpec((tm, d), lambda i, j: (i, 0)),
                  pl.BlockSpec((1, d), lambda i, j: (0, 0)),
                  shp, scp, shs, scs,
                  pl.BlockSpec((None, d, tn), lambda i, j: (layer, 0, j))],
        out_specs=[pl.BlockSpec((tm, tn), lambda i, j: (i, jnp.minimum(j, n_main - 1))),
                   pl.BlockSpec((tm, tn), lambda i, j: (i, 0)),
                   pl.BlockSpec((tm, d), lambda i, j: (i, 0))],
        out_shape=[jax.ShapeDtypeStruct((t_all, N_MAIN), MXU_DTYPE),
                   jax.ShapeDtypeStruct((t_all, N_TAIL), F32),
                   jax.ShapeDtypeStruct((t_all, d), MXU_DTYPE)],
        compiler_params=_cp("arbitrary", "arbitrary"),
        name="inproj",
    )(x_all, norm_w.reshape(1, d), ada_p, ada_p, ada_s, ada_s, w_pack)


def _tokens_per_step(tm, n_j):
    per = -(-tm // n_j)
    return per + per % 2


def _inproj_c_kernel(dcur_ref, dnxt_ref, eo_ref, x1_ref, tg_ref, g2p_ref, g2s_ref,
                     nw_ref, shp_ref, scp_ref, shs_ref, scs_ref, w_ref,
                     pm_ref, pt_ref, h_ref, xn_ref, gbuf, sem, *, n_p, reps, n_main):
    i = pl.program_id(0)
    j = pl.program_id(1)
    n_t = pl.num_programs(0)
    n_j = n_main + 1
    tm = x1_ref.shape[0]
    per = _tokens_per_step(tm, n_j)
    surplus = (per * n_j - tm) * TOP_K

    def issue(d_ref, buf, tok0):
        for u in range(per):
            t = tok0 + u
            ts = jnp.minimum(t, tm - 1)
            for k in range(TOP_K):
                pltpu.make_async_copy(eo_ref.at[pl.ds(d_ref[ts * TOP_K + k], 1)],
                                      gbuf.at[buf, k, pl.ds(t, 1)], sem.at[buf]).start()

    def wait(buf):
        for k in range(TOP_K):
            pltpu.make_async_copy(eo_ref.at[pl.ds(0, tm)], gbuf.at[buf, k, pl.ds(0, tm)], sem.at[buf]).wait()
        if surplus:
            pltpu.make_async_copy(eo_ref.at[pl.ds(0, surplus)], gbuf.at[buf, 0, pl.ds(0, surplus)], sem.at[buf]).wait()

    @pl.when((i == 0) & (j == 0))
    def _():
        def body(jj, carry):
            issue(dcur_ref, 0, jj * per)
            return carry
        lax.fori_loop(0, n_j, body, 0)

    nxt = (i + 1) % 2

    def make(g2, sh, sc):
        cur = i % 2
        wait(cur)
        tg = tg_ref[...]
        y = tg[:, 0:1] * gbuf[cur, 0, pl.ds(0, tm), :]
        for k in range(1, TOP_K):
            y = y + tg[:, k:k + 1] * gbuf[cur, k, pl.ds(0, tm), :]
        x = x1_ref[...] + g2 * y
        xn_ref[...] = x
        h_ref[...] = (_rms(x, nw_ref[...]) * (1.0 + sc) + sh).astype(h_ref.dtype)

    @pl.when((j == 0) & (i < n_p))
    def _():
        make(g2p_ref[...], shp_ref[...], scp_ref[...])

    @pl.when((j == 0) & (i >= n_p))
    def _():
        make(_tile_rows(g2s_ref[...], reps), _tile_rows(shs_ref[...], reps), _tile_rows(scs_ref[...], reps))

    issue(dnxt_ref, nxt, j * per)
    res = jnp.dot(h_ref[...], w_ref[...], preferred_element_type=F32)

    @pl.when(j < n_main)
    def _():
        pm_ref[...] = res.astype(pm_ref.dtype)

    @pl.when(j == n_main)
    def _():
        pt_ref[...] = res

    @pl.when((i == n_t - 1) & (j == n_j - 1))
    def _():
        wait(nxt)


def _inproj_combine(dest_flat, eo, x1, topg, ada_prev_p, ada_prev_s, norm_w, ada_p, ada_s, w_pack, layer, dims):
    d, tm, seq, b_p, b_s = dims
    tn = N_TAIL
    t_all = x1.shape[0]
    n_t = t_all // tm
    n_p = b_p * seq // tm
    n_main = N_MAIN // tn
    n_j = n_main + 1
    buf_rows = -(-(_tokens_per_step(tm, n_j) * n_j) // SUBLANES) * SUBLANES
    g2p, g2s = _mod_specs(dims, 5, 2)
    shp, shs = _mod_specs(dims, 0, 2)
    scp, scs = _mod_specs(dims, 1, 2)
    kern = functools.partial(_inproj_c_kernel, n_p=n_p, reps=tm // b_s, n_main=n_main)
    row = lambda n: pl.BlockSpec((tm, n), lambda i, j: (i, 0))
    return pl.pallas_call(
        kern,
        grid=(n_t, n_j),
        in_specs=[pl.BlockSpec((tm * TOP_K,), lambda i, j: (i,), memory_space=pltpu.SMEM),
                  pl.BlockSpec((tm * TOP_K,), lambda i, j: (jnp.minimum(i + 1, n_t - 1),), memory_space=pltpu.SMEM),
                  pl.BlockSpec(memory_space=pl.ANY),
                  row(d), row(LANES), g2p, g2s,
                  pl.BlockSpec((1, d), lambda i, j: (0, 0)),
                  shp, scp, shs, scs,
                  pl.BlockSpec((None, d, tn), lambda i, j: (layer, 0, j))],
        out_specs=[pl.BlockSpec((tm, tn), lambda i, j: (i, jnp.minimum(j, n_main - 1))),
                   pl.BlockSpec((tm, tn), lambda i, j: (i, 0)),
                   row(d), row(d)],
        out_shape=[jax.ShapeDtypeStruct((t_all, N_MAIN), MXU_DTYPE),
                   jax.ShapeDtypeStruct((t_all, N_TAIL), F32),
                   jax.ShapeDtypeStruct((t_all, d), MXU_DTYPE),
                   jax.ShapeDtypeStruct((t_all, d), F32)],
        scratch_shapes=[pltpu.VMEM((2, TOP_K, buf_rows, d), F32), pltpu.SemaphoreType.DMA((2,))],
        compiler_params=_cp("arbitrary", "arbitrary"),
        name="inproj_combine",
    )(dest_flat, dest_flat, eo, x1, topg, ada_prev_p, ada_prev_s, norm_w.reshape(1, d),
      ada_p, ada_p, ada_s, ada_s, w_pack)


def _conva_p_kernel(b_ref, c_ref, x_ref, w_ref, y_ref, st_ref, buf):
    seq = b_ref.shape[0]
    u = c_ref[...].astype(F32) * x_ref[...].astype(F32)
    buf[0:SUBLANES, :] = jnp.zeros((SUBLANES, buf.shape[1]), F32)
    buf[SUBLANES:, :] = u
    w = w_ref[...]
    acc = w[2:3, :] * u
    acc = acc + w[1:2, :] * buf[pl.ds(SUBLANES - 1, seq), :]
    acc = acc + w[0:1, :] * buf[pl.ds(SUBLANES - 2, seq), :]
    y_ref[...] = (b_ref[...].astype(F32) * acc).astype(y_ref.dtype)
    st_ref[...] = buf[pl.ds(seq, SUBLANES), :]


def _conva_prompt(p, conv_w, dims, t_all, tc=256):
    d, tm, seq, b_p, b_s = dims
    nc = d // tc
    return pl.pallas_call(
        _conva_p_kernel,
        grid=(b_p, nc),
        in_specs=[pl.BlockSpec((seq, tc), lambda b, c: (b, C_B // tc + c)),
                  pl.BlockSpec((seq, tc), lambda b, c: (b, C_C // tc + c)),
                  pl.BlockSpec((seq, tc), lambda b, c: (b, C_X // tc + c)),
                  pl.BlockSpec((CONV_A_WIDTH, tc), lambda b, c: (0, c))],
        out_specs=[pl.BlockSpec((seq, tc), lambda b, c: (b, c)),
                   pl.BlockSpec((None, SUBLANES, tc), lambda b, c: (b, 0, c))],
        out_shape=[jax.ShapeDtypeStruct((t_all, d), MXU_DTYPE),
                   jax.ShapeDtypeStruct((b_p, SUBLANES, d), F32)],
        scratch_shapes=[pltpu.VMEM((seq + SUBLANES, tc), F32)],
        compiler_params=_cp("arbitrary", "arbitrary"),
        name="conva_prompt",
    )(p, p, p, conv_w)


def _conva_s_kernel(b_ref, c_ref, x_ref, s_ref, w_ref, y_ref, st_ref, *, b_s, steps):
    ext = jnp.concatenate([s_ref[...], c_ref[...].astype(F32) * x_ref[...].astype(F32)], axis=0)
    w = w_ref[...]
    acc = None
    for k in range(CONV_A_WIDTH):
        t = w[k:k + 1, :] * ext[k * b_s:(k + steps) * b_s]
        acc = t if acc is None else acc + t
    y_ref[...] = (b_ref[...].astype(F32) * acc).astype(y_ref.dtype)
    st_ref[...] = ext[steps * b_s:]


def _conva_sample(p, state_tm, conv_w, dims, steps, tc=256):
    d, tm, seq, b_p, b_s = dims
    t_s = b_s * steps
    rb = (b_p * seq) // t_s
    ns = (CONV_A_WIDTH - 1) * b_s
    kern = functools.partial(_conva_s_kernel, b_s=b_s, steps=steps)
    return pl.pallas_call(
        kern,
        grid=(d // tc,),
        in_specs=[pl.BlockSpec((t_s, tc), lambda c: (rb, C_B // tc + c)),
                  pl.BlockSpec((t_s, tc), lambda c: (rb, C_C // tc + c)),
                  pl.BlockSpec((t_s, tc), lambda c: (rb, C_X // tc + c)),
                  pl.BlockSpec((ns, tc), lambda c: (0, c)),
                  pl.BlockSpec((CONV_A_WIDTH, tc), lambda c: (0, c))],
        out_specs=[pl.BlockSpec((t_s, tc), lambda c: (0, c)),
                   pl.BlockSpec((ns, tc), lambda c: (0, c))],
        out_shape=[jax.ShapeDtypeStruct((t_s, d), MXU_DTYPE),
                   jax.ShapeDtypeStruct((ns, d), F32)],
        compiler_params=_cp("arbitrary"),
        name="conva_sample",
    )(p, p, p, state_tm, conv_w)


def _ssd_p_kernel(z_ref, xs_ref, bc_ref, dt_ref, wx_ref, wbc_ref, bx_ref, bbc_ref, dtb_ref, alog_ref,
                  de_ref, gn_ref, rexp_ref, shift_ref,
                  y_ref, csx_ref, csbc_ref, stout_ref,
                  xbuf, bcbuf, st, ybuf):
    c = pl.program_id(1)
    q = xs_ref.shape[0]
    d_ssm = xs_ref.shape[1]
    half = d_ssm // SSM_GROUPS

    @pl.when(c == 0)
    def _():
        xbuf[...] = jnp.zeros(xbuf.shape, xbuf.dtype)
        bcbuf[0:SUBLANES, :] = jnp.zeros((SUBLANES, bcbuf.shape[1]), F32)
        st[...] = jnp.zeros(st.shape, F32)

    bcbuf[SUBLANES:, :] = bc_ref[...]

    acc = bbc_ref[...]
    for k in range(SSM_CONV_WIDTH):
        acc = acc + wbc_ref[k:k + 1, :] * bcbuf[pl.ds(SUBLANES - (SSM_CONV_WIDTH - 1) + k, q), :]
    bcbuf[0:SUBLANES, :] = bcbuf[pl.ds(q, SUBLANES), :]
    bca = _silu(acc)
    csbc_ref[...] = bcbuf[0:SUBLANES, :]

    dtv = _softplus(dt_ref[:, 0:LANES] + dtb_ref[...])
    a = -jnp.exp(alog_ref[...])
    da = dtv * a
    row = _iota((q, q), 0)
    col = _iota((q, q), 1)
    causal = row >= col
    tri = jnp.where(causal, 1.0, 0.0).astype(MXU_DTYPE)
    acs = _sel_mm(tri, da)
    acs_t = acs.T
    bslab = bca[:, 0:LANES]
    cslab = bca[:, LANES:2 * LANES]
    bt = bslab.T
    lane = _iota((q, LANES), 1)
    srow = _iota((st.shape[0], half), 0)
    pairs_per_group = SSM_HEADS // SSM_GROUPS // 2
    for g in range(SSM_GROUPS):
        cols = slice(g * half, (g + 1) * half)
        xcur = xs_ref[:, cols]
        xext = jnp.concatenate([xbuf[:, cols], xcur], axis=0)
        acc = bx_ref[:, cols] + wx_ref[SSM_CONV_WIDTH - 1:SSM_CONV_WIDTH, cols] * xcur.astype(F32)
        for s in range(1, SSM_CONV_WIDTH):
            k = SSM_CONV_WIDTH - 1 - s
            acc = acc + wx_ref[k:k + 1, cols] * jnp.dot(shift_ref[s - 1], xext, preferred_element_type=F32)
        xbuf[:, cols] = xcur
        xa = _silu(acc)
        csx_ref[:, cols] = xs_ref[q - SUBLANES:q, cols].astype(F32)
        rexp = rexp_ref[:, cols]
        acs_e = _mm_sel(acs, rexp)
        dt_e = _mm_sel(dtv, rexp, n=2)
        last_e = acs_e[q - 1:q, :]
        exp_e = jnp.exp(acs_e)
        w_e = jnp.exp(last_e - acs_e)
        dec_last = jnp.exp(last_e)
        xdt = xa * dt_e
        st_old = st[:, cols]
        y_off = _mm(cslab, st_old) * exp_e
        upd = _mm(bt, xdt * w_e)
        st[:, cols] = dec_last * st_old + jnp.where(_half_mask(srow, g == 1), upd, 0.0)

        cb = _mm(jnp.where(_half_mask(lane, g == 1), cslab, 0.0), bt)
        for jl in range(pairs_per_group):
            jp = g * pairs_per_group + jl
            ms = []
            for hh in (2 * jp, 2 * jp + 1):
                seg = acs[:, hh:hh + 1] - acs_t[hh:hh + 1, :]
                ms.append(cb * jnp.exp(jnp.where(causal, seg, NEG_BIG)))
            lhs = jnp.concatenate(ms, axis=1)
            slab = xdt[:, jl * LANES:(jl + 1) * LANES]
            rhs = jnp.concatenate([jnp.where(lane < SSM_HEAD_DIM, slab, 0.0),
                                   jnp.where(lane >= SSM_HEAD_DIM, slab, 0.0)], axis=0)
            ybuf[:, jp * LANES:(jp + 1) * LANES] = _mm(lhs, rhs)

        u = (ybuf[:, cols] + y_off + de_ref[:, cols] * xa) * _silu(z_ref[:, cols].astype(F32))
        un = u * lax.rsqrt(jnp.mean(u * u, axis=-1, keepdims=True) + RMS_EPS)
        y_ref[:, cols] = (un * gn_ref[:, cols]).astype(y_ref.dtype)

    @pl.when(c == pl.num_programs(1) - 1)
    def _():
        stout_ref[...] = st[...]


def _ssd_prompt(pm, pt, prm, dims, t_all):
    d, tm, seq, b_p, b_s = dims
    q = min(SSD_CHUNK, seq)
    nc = seq // q
    n_bc = 2 * SSM_GROUPS * SSM_STATE
    full = lambda shape: pl.BlockSpec(shape, lambda b, c: tuple(0 for _ in shape))
    return pl.pallas_call(
        _ssd_p_kernel,
        grid=(b_p, nc),
        in_specs=[pl.BlockSpec((q, d), lambda b, c: (b * nc + c, C_Z // d)),
                  pl.BlockSpec((q, d), lambda b, c: (b * nc + c, C_XS // d)),
                  pl.BlockSpec((q, n_bc), lambda b, c: (b * nc + c, T_BC // n_bc)),
                  pl.BlockSpec((q, n_bc), lambda b, c: (b * nc + c, T_DT // n_bc)),
                  full((SSM_CONV_WIDTH, d)), full((SSM_CONV_WIDTH, n_bc)),
                  full((1, d)), full((1, n_bc)), full((1, LANES)), full((1, LANES)),
                  full((1, d)), full((1, d)), full((LANES, d)), full((SSM_CONV_WIDTH - 1, q, 2 * q))],
        out_specs=[pl.BlockSpec((q, d), lambda b, c: (b * nc + c, 0)),
                   pl.BlockSpec((None, SUBLANES, d), lambda b, c: (b, 0, 0)),
                   pl.BlockSpec((None, SUBLANES, n_bc), lambda b, c: (b, 0, 0)),
                   pl.BlockSpec((None, 2 * SSM_STATE, d), lambda b, c: (b, 0, 0))],
        out_shape=[jax.ShapeDtypeStruct((t_all, d), MXU_DTYPE),
                   jax.ShapeDtypeStruct((b_p, SUBLANES, d), F32),
                   jax.ShapeDtypeStruct((b_p, SUBLANES, n_bc), F32),
                   jax.ShapeDtypeStruct((b_p, 2 * SSM_STATE, d), F32)],
        scratch_shapes=[pltpu.VMEM((q, d), MXU_DTYPE),
                        pltpu.VMEM((q + SUBLANES, n_bc), F32),
                        pltpu.VMEM((2 * SSM_STATE, d), F32),
                        pltpu.VMEM((q, d), F32)],
        compiler_params=_cp("arbitrary", "arbitrary"),
        name="ssd_prompt",
    )(pm, pm, pt, pt, prm["wx"], prm["wbc"], prm["bx"], prm["bbc"], prm["dtb"], prm["alog"],
      prm["de"], prm["gn"], prm["rexp"], prm["shift"])


def _ssd_s_kernel(xs_ref, bc_ref, dt_ref, cx_ref, cbc_ref, wx_ref, wbc_ref, bx_ref, bbc_ref, dtb_ref,
                  alog_ref, de_ref, e2_ref, g2_ref, s_ref,
                  y_ref, snew_ref, csx_ref, csbc_ref, *, b_s, steps):
    j = pl.program_id(0)
    pn = SSM_HEAD_DIM * SSM_STATE
    extx = jnp.concatenate([cx_ref[...], xs_ref[...].astype(F32)], axis=0)
    extbc = jnp.concatenate([cbc_ref[...], bc_ref[...]], axis=0)

    def conv(ext, w, b):
        acc = b
        for k in range(SSM_CONV_WIDTH):
            acc = acc + w[k:k + 1, :] * ext[k * b_s:(k + steps) * b_s]
        return _silu(acc)

    xa = conv(extx, wx_ref[...], bx_ref[...])
    bca = conv(extbc, wbc_ref[...], bbc_ref[...])
    csx_ref[...] = extx[steps * b_s:]
    csbc_ref[...] = extbc[steps * b_s:]
    dtv = _softplus(dt_ref[:, 0:LANES] + dtb_ref[...])
    dec = jnp.exp(dtv * (-jnp.exp(alog_ref[...])))
    lane = _iota(dtv.shape, 1)
    grp = (2 * j) // (SSM_HEADS // SSM_GROUPS)
    own = (lane // SSM_STATE) == grp
    bslab = bca[:, 0:LANES]
    cslab = bca[:, LANES:2 * LANES]
    b2 = jnp.where(own, bslab, pltpu.roll(bslab, SSM_STATE, 1))
    c2 = jnp.where(own, cslab, pltpu.roll(cslab, SSM_STATE, 1))
    ys = [None] * steps
    for lh in range(2):
        hh = 2 * j + lh
        dt_col = jnp.sum(jnp.where(lane == hh, dtv, 0.0), axis=1, keepdims=True)
        dec_col = jnp.sum(jnp.where(lane == hh, dec, 0.0), axis=1, keepdims=True)
        s = s_ref[:, lh * pn:(lh + 1) * pn]
        for t in range(steps):
            rows = slice(t * b_s, (t + 1) * b_s)
            xe = _mm_sel(xa[rows] * dt_col[rows], e2_ref[lh], n=2)
            s = dec_col[rows] * s + xe * _tile_lanes(b2[rows], pn // LANES)
            yt = _mm(s * _tile_lanes(c2[rows], pn // LANES), g2_ref[lh])
            ys[t] = yt if ys[t] is None else ys[t] + yt
        snew_ref[:, lh * pn:(lh + 1) * pn] = s
    y_ref[...] = jnp.concatenate(ys, axis=0) + de_ref[...] * xa


def _ssd_sample(pm, pt, cs_x, cs_bc, state3d, layer, prm, dims, steps):
    d, tm, seq, b_p, b_s = dims
    t_s = b_s * steps
    rb = (b_p * seq) // t_s
    n_bc = 2 * SSM_GROUPS * SSM_STATE
    pn = SSM_HEAD_DIM * SSM_STATE
    ns = (SSM_CONV_WIDTH - 1) * b_s
    kern = functools.partial(_ssd_s_kernel, b_s=b_s, steps=steps)
    full = lambda shape: pl.BlockSpec(shape, lambda j: tuple(0 for _ in shape))
    return pl.pallas_call(
        kern,
        grid=(SSM_HEADS // 2,),
        in_specs=[pl.BlockSpec((t_s, LANES), lambda j: (rb, C_XS // LANES + j)),
                  pl.BlockSpec((t_s, n_bc), lambda j: (rb, T_BC // n_bc)),
                  pl.BlockSpec((t_s, n_bc), lambda j: (rb, T_DT // n_bc)),
                  pl.BlockSpec((ns, LANES), lambda j: (0, j)),
                  full((ns, n_bc)),
                  pl.BlockSpec((SSM_CONV_WIDTH, LANES), lambda j: (0, j)),
                  full((SSM_CONV_WIDTH, n_bc)),
                  pl.BlockSpec((1, LANES), lambda j: (0, j)),
                  full((1, n_bc)), full((1, LANES)), full((1, LANES)),
                  pl.BlockSpec((1, LANES), lambda j: (0, j)),
                  full((2, LANES, pn)), full((2, pn, LANES)),
                  pl.BlockSpec((None, b_s, 2 * pn), lambda j: (layer, 0, j))],
        out_specs=[pl.BlockSpec((t_s, LANES), lambda j: (0, j)),
                   pl.BlockSpec((b_s, 2 * pn), lambda j: (0, j)),
                   pl.BlockSpec((ns, LANES), lambda j: (0, j)),
                   full((ns, n_bc))],
        out_shape=[jax.ShapeDtypeStruct((t_s, d), F32),
                   jax.ShapeDtypeStruct(state3d.shape[1:], F32),
                   jax.ShapeDtypeStruct((ns, d), F32),
                   jax.ShapeDtypeStruct((ns, n_bc), F32)],
        compiler_params=_cp("arbitrary"),
        name="ssd_sample",
    )(pm, pt, pt, cs_x, cs_bc, prm["wx"], prm["wbc"], prm["bx"], prm["bbc"], prm["dtb"], prm["alog"],
      prm["de"], prm["e2"], prm["g2"], state3d)


def _gnorm_s_kernel(y_ref, z_ref, gn_ref, o_ref):
    u = y_ref[...] * _silu(z_ref[...].astype(F32))
    half = u.shape[1] // SSM_GROUPS
    outs = []
    for g in range(SSM_GROUPS):
        ug = u[:, g * half:(g + 1) * half]
        outs.append(ug * lax.rsqrt(jnp.mean(ug * ug, axis=-1, keepdims=True) + RMS_EPS))
    o_ref[...] = (jnp.concatenate(outs, axis=1) * gn_ref[...]).astype(o_ref.dtype)


def _gnorm_sample(y_raw, p, gn, dims, steps):
    d, tm, seq, b_p, b_s = dims
    t_s = b_s * steps
    rb = (b_p * seq) // t_s
    return pl.pallas_call(
        _gnorm_s_kernel,
        grid=(1,),
        in_specs=[pl.BlockSpec((t_s, d), lambda i: (0, 0)),
                  pl.BlockSpec((t_s, d), lambda i: (rb, C_Z // d)),
                  pl.BlockSpec((1, d), lambda i: (0, 0))],
        out_specs=pl.BlockSpec((t_s, d), lambda i: (0, 0)),
        out_shape=jax.ShapeDtypeStruct((t_s, d), MXU_DTYPE),
        compiler_params=_cp("arbitrary"),
        name="gnorm_sample",
    )(y_raw, p, gn)


def _rope(x, cos_f, sin_a, sin_b):
    n = x.shape[1] // cos_f.shape[1]
    half = ROPE_DIMS // 2
    return (x * _tile_lanes(cos_f, n) + pltpu.roll(x, half, 1) * _tile_lanes(sin_a, n)
            + pltpu.roll(x, x.shape[1] - half, 1) * _tile_lanes(sin_b, n))


def _attn_p_kernel(sink_ref, q_ref, k_ref, v_ref, cos_ref, sa_ref, sb_ref,
                   o_ref, kn_ref, vn_ref, kprev, vprev):
    c = pl.program_id(1)
    w = q_ref.shape[0]
    cos_f, sin_a, sin_b = cos_ref[...], sa_ref[...], sb_ref[...]
    scale = HEAD_DIM ** -0.5
    k = _rope(k_ref[...], cos_f, sin_a, sin_b)
    v = v_ref[...]

    @pl.when(c == 0)
    def _():
        kprev[...] = jnp.zeros(kprev.shape, F32)
        vprev[...] = jnp.zeros(vprev.shape, F32)

    kk = jnp.concatenate([kprev[...], k], axis=0)
    vv = jnp.concatenate([vprev[...], v], axis=0)
    key = _iota((2 * w, w), 0)
    qry = _iota((2 * w, w), 1)
    valid = (key > qry) & (key <= qry + w) & ((c > 0) | (key >= w))
    valid = _tile_lanes(valid.astype(F32), Q_PER_KV) > 0.5
    lane_k = _iota((2 * w, LANES), 1)
    sub_v = _iota((LANES, 2 * w), 0)
    for a in range(KV_HEADS // 2):
        slabs = range(a * Q_PER_KV, (a + 1) * Q_PER_KV)
        qst = jnp.concatenate(
            [(_rope(q_ref[:, LANES * j:LANES * (j + 1)].astype(F32), cos_f, sin_a, sin_b) * scale).astype(MXU_DTYPE)
             for j in slabs], axis=0)
        ksl = kk[:, LANES * a:LANES * (a + 1)]
        vt = vv[:, LANES * a:LANES * (a + 1)].T
        o_t = None
        for hk in range(2):
            ks = jnp.where(_half_mask(lane_k, hk == 1), ksl, 0.0).astype(MXU_DTYPE)
            vs = jnp.where(_half_mask(sub_v, hk == 1), vt, 0.0).astype(MXU_DTYPE)
            sink = jnp.concatenate([jnp.full((1, w), sink_ref[HEAD_ORDER[2 * j + hk]], F32) for j in slabs], axis=1)
            s = lax.dot_general(ks, qst, (((1,), (1,)), ((), ())), preferred_element_type=F32)
            s = jnp.where(valid, s, NEG_BIG)
            m = jnp.maximum(jnp.max(s, axis=0, keepdims=True), sink)
            pr = jnp.exp(s - m)
            den = jnp.sum(pr, axis=0, keepdims=True) + jnp.exp(sink - m)
            pn = (pr * (1.0 / den)).astype(MXU_DTYPE)
            o = jnp.dot(vs, pn, preferred_element_type=F32)
            o_t = o if o_t is None else o_t + o
        o_all = o_t.T
        for g, j in enumerate(slabs):
            o_ref[:, LANES * j:LANES * (j + 1)] = o_all[g * w:(g + 1) * w].astype(o_ref.dtype)
    kprev[...] = k
    vprev[...] = v
    kn_ref[...] = k
    vn_ref[...] = v


def _attn_prompt(pm, pt, sinks, tabs, dims, t_all):
    d, tm, seq, b_p, b_s = dims
    w = WINDOW
    nb = seq // w
    d_kv = KV_HEADS * HEAD_DIM
    tab_spec = pl.BlockSpec((w, LANES), lambda b, c: (c, 0))
    const = lambda shape: pl.BlockSpec(shape, lambda b, c: tuple(0 for _ in shape))
    return pl.pallas_call(
        _attn_p_kernel,
        grid=(b_p, nb),
        in_specs=[pl.BlockSpec(memory_space=pltpu.SMEM),
                  pl.BlockSpec((w, d), lambda b, c: (b * nb + c, C_Q // d)),
                  pl.BlockSpec((w, d_kv), lambda b, c: (b * nb + c, T_K // d_kv)),
                  pl.BlockSpec((w, d_kv), lambda b, c: (b * nb + c, T_V // d_kv)),
                  tab_spec, tab_spec, tab_spec],
        out_specs=[pl.BlockSpec((w, d), lambda b, c: (b * nb + c, 0)),
                   pl.BlockSpec((None, w, d_kv), lambda b, c: (b, 0, 0)),
                   pl.BlockSpec((None, w, d_kv), lambda b, c: (b, 0, 0))],
        out_shape=[jax.ShapeDtypeStruct((t_all, d), MXU_DTYPE),
                   jax.ShapeDtypeStruct((b_p, w, d_kv), F32),
                   jax.ShapeDtypeStruct((b_p, w, d_kv), F32)],
        scratch_shapes=[pltpu.VMEM((w, d_kv), F32), pltpu.VMEM((w, d_kv), F32)],
        compiler_params=_cp("arbitrary", "arbitrary"),
        name="attn_prompt",
    )(sinks, pm, pt, pt, *tabs)


def _rope_s_kernel(q_ref, k_ref, cos_ref, sa_ref, sb_ref, qo_ref, ko_ref):
    cos_f, sin_a, sin_b = cos_ref[...], sa_ref[...], sb_ref[...]
    qo_ref[...] = _rope(q_ref[...].astype(F32), cos_f, sin_a, sin_b)
    ko_ref[...] = _rope(k_ref[...], cos_f, sin_a, sin_b)


def _rope_sample(pm, pt, tabs, dims, steps):
    d, tm, seq, b_p, b_s = dims
    t_s = b_s * steps
    rb = (b_p * seq) // t_s
    d_kv = KV_HEADS * HEAD_DIM
    tab_spec = pl.BlockSpec((t_s, LANES), lambda i: (0, 0))
    return pl.pallas_call(
        _rope_s_kernel,
        grid=(1,),
        in_specs=[pl.BlockSpec((t_s, d), lambda i: (rb, C_Q // d)),
                  pl.BlockSpec((t_s, d_kv), lambda i: (rb, T_K // d_kv)),
                  tab_spec, tab_spec, tab_spec],
        out_specs=[pl.BlockSpec((t_s, d), lambda i: (0, 0)),
                   pl.BlockSpec((t_s, d_kv), lambda i: (0, 0))],
        out_shape=[jax.ShapeDtypeStruct((t_s, d), F32),
                   jax.ShapeDtypeStruct((t_s, d_kv), F32)],
        compiler_params=_cp("arbitrary"),
        name="rope_sample",
    )(pm, pt, *tabs)


def _attn_s_kernel(q_ref, kn_ref, vn_ref, ko_ref, vo_ref, sink_ref, valid_ref, o_ref):
    tb, nq, d_kv = q_ref.shape
    w = kn_ref.shape[1]
    n_old = ko_ref.shape[1]
    pad = jnp.zeros((w - n_old, d_kv), F32)
    valid = valid_ref[...] > 0.5
    sink = sink_ref[:, 0:1]
    scale = HEAD_DIM ** -0.5
    orow = _iota((nq, HEAD_DIM), 0)
    rows_per_kv = nq // KV_HEADS
    for b in range(tb):
        kk = jnp.concatenate([kn_ref[b], ko_ref[b], pad], axis=0)
        vv = jnp.concatenate([vn_ref[b], vo_ref[b], pad], axis=0)
        s = _mm_nt(q_ref[b], kk) * scale
        s = jnp.where(valid, s, NEG_BIG)
        m = jnp.maximum(jnp.max(s, axis=-1, keepdims=True), sink)
        pr = jnp.exp(s - m)
        den = jnp.sum(pr, axis=-1, keepdims=True) + jnp.exp(sink - m)
        o = _mm(pr, vv) / den
        acc = jnp.zeros((nq, HEAD_DIM), F32)
        for kh in range(KV_HEADS):
            mine = (orow >= kh * rows_per_kv) & (orow < (kh + 1) * rows_per_kv)
            acc = acc + jnp.where(mine, o[:, kh * HEAD_DIM:(kh + 1) * HEAD_DIM], 0.0)
        o_ref[b] = acc


def _attn_sample(qbd, k_cache, v_cache, k_new, v_new, layer, sink_col, steps, tb=8):
    b_s, nq, d_kv = qbd.shape
    w = k_cache.shape[2]
    t = (jnp.arange(nq) % steps)[:, None]
    col = jnp.arange(2 * w)[None, :]
    valid = jnp.where(col < w, col > t, col - w <= t).astype(F32)
    return pl.pallas_call(
        _attn_s_kernel,
        grid=(b_s // tb,),
        in_specs=[pl.BlockSpec((tb, nq, d_kv), lambda i: (i, 0, 0)),
                  pl.BlockSpec((None, tb, w, d_kv), lambda i: (layer, i, 0, 0)),
                  pl.BlockSpec((None, tb, w, d_kv), lambda i: (layer, i, 0, 0)),
                  pl.BlockSpec((tb, SUBLANES, d_kv), lambda i: (i, 0, 0)),
                  pl.BlockSpec((tb, SUBLANES, d_kv), lambda i: (i, 0, 0)),
                  pl.BlockSpec((nq, LANES), lambda i: (0, 0)),
                  pl.BlockSpec((nq, 2 * w), lambda i: (0, 0))],
        out_specs=pl.BlockSpec((tb, nq, HEAD_DIM), lambda i: (i, 0, 0)),
        out_shape=jax.ShapeDtypeStruct((b_s, nq, HEAD_DIM), F32),
        compiler_params=_cp("arbitrary"),
        name="attn_sample",
    )(qbd, k_cache, v_cache, k_new, v_new, sink_col, valid)


def _merge_kernel(h_ref, yap_ref, ybp_ref, ycp_ref, yas_ref, ybs_ref, ycs_ref, x_ref,
                  wg_ref, bg_ref, pa_ref, pb_ref, pc_ref, wo_ref,
                  g1p_ref, g1s_ref, nw_ref, shp_ref, scp_ref, shs_ref, scs_ref, rw_ref, rb_ref,
                  x1_ref, h2_ref, ti_ref, tg_ref, *, n_p, reps):
    i = pl.program_id(0)
    d = x_ref.shape[1]
    is_p = i < n_p
    ya = jnp.where(is_p, yap_ref[...], yas_ref[...])
    yb = jnp.where(is_p, ybp_ref[...], ybs_ref[...])
    yc = jnp.where(is_p, ycp_ref[...], ycs_ref[...])
    g = _sigmoid(jnp.dot(h_ref[...], wg_ref[...], preferred_element_type=F32) + bg_ref[...])
    merged = (g[:, 0:d] * _mm(ya, pa_ref[...]) + g[:, d:2 * d] * _mm(yb, pb_ref[...])
              + g[:, 2 * d:3 * d] * _mm(yc, pc_ref[...]))
    mix = _mm(merged, wo_ref[...])

    def finish(g1, sh, sc):
        x1 = x_ref[...] + g1 * mix
        x1_ref[...] = x1
        h2_ref[...] = _rms(x1, nw_ref[...]) * (1.0 + sc) + sh

    @pl.when(i < n_p)
    def _():
        finish(g1p_ref[...], shp_ref[...], scp_ref[...])

    @pl.when(i >= n_p)
    def _():
        finish(_tile_rows(g1s_ref[...], reps), _tile_rows(shs_ref[...], reps), _tile_rows(scs_ref[...], reps))

    h2 = h2_ref[...]
    rw = rw_ref[...]
    hs = _split(h2, 2)
    ws = _split(rw, 2)
    if len(hs) == 1:
        logits = jnp.dot(h2, rw, preferred_element_type=F32)
    else:
        logits = (jnp.dot(hs[0], ws[0], preferred_element_type=F32)
                  + jnp.dot(hs[0], ws[1], preferred_element_type=F32)
                  + jnp.dot(hs[1], ws[0], preferred_element_type=F32))
    logits = logits + rb_ref[...]
    lane = _iota(logits.shape, 1).astype(F32)
    vals, idxs = [], []
    cur = logits
    for _ in range(TOP_K):
        mx = jnp.max(cur, axis=-1, keepdims=True)
        ix = jnp.min(jnp.where(cur == mx, lane, float(LANES)), axis=-1, keepdims=True)
        vals.append(mx)
        idxs.append(ix)
        cur = jnp.where(lane == ix, -jnp.inf, cur)
    es = [jnp.exp(v - vals[0]) for v in vals]
    den = es[0]
    for e in es[1:]:
        den = den + e
    ti = jnp.zeros(logits.shape, F32)
    tg = jnp.zeros(logits.shape, F32)
    for k in range(TOP_K):
        ti = jnp.where(lane == float(k), idxs[k], ti)
        tg = jnp.where(lane == float(k), es[k] / den, tg)
    ti_ref[...] = ti.astype(I32)
    tg_ref[...] = tg


def _merge(h, ys_p, ys_s, x_all, wts, ada_p, ada_s, dims, tm_m=512):
    d, tm, seq, b_p, b_s = dims
    t_all = x_all.shape[0]
    tm_m = min(tm_m, tm)
    mdims = (d, tm_m, seq, b_p, b_s)
    n_p = b_p * seq // tm_m
    g1p, g1s = _mod_specs(mdims, 2, 1)
    shp, shs = _mod_specs(mdims, 3, 1)
    scp, scs = _mod_specs(mdims, 4, 1)
    row = lambda n: pl.BlockSpec((tm_m, n), lambda i: (i, 0))
    row_p = pl.BlockSpec((tm_m, d), lambda i: (jnp.minimum(i, n_p - 1), 0))
    row_s = pl.BlockSpec((tm_m, d), lambda i: (jnp.maximum(i - n_p, 0), 0))
    const = lambda shape: pl.BlockSpec(shape, lambda i: tuple(0 for _ in shape), pipeline_mode=pl.Buffered(1))
    kern = functools.partial(_merge_kernel, n_p=n_p, reps=max(tm_m // b_s, 1))
    return pl.pallas_call(
        kern,
        grid=(t_all // tm_m,),
        in_specs=[row(d), row_p, row_p, row_p, row_s, row_s, row_s, row(d),
                  const((d, 3 * d)), const((1, 3 * d)), const((d, d)), const((d, d)), const((d, d)),
                  const((d, d)),
                  g1p, g1s, const((1, d)), shp, scp, shs, scs,
                  const((d, LANES)), const((1, LANES))],
        out_specs=[row(d), row(d), row(LANES), row(LANES)],
        out_shape=[jax.ShapeDtypeStruct((t_all, d), F32),
                   jax.ShapeDtypeStruct((t_all, d), F32),
                   jax.ShapeDtypeStruct((t_all, LANES), I32),
                   jax.ShapeDtypeStruct((t_all, LANES), F32)],
        compiler_params=_cp("arbitrary"),
        name="merge",
    )(h, *ys_p, *ys_s, x_all, wts["wg"], wts["bg"], wts["pa"], wts["pb"], wts["pc"], wts["wo"],
      ada_p, ada_s, wts["nffn"], ada_p, ada_p, ada_s, ada_s, wts["rw"], wts["rb"])


def _rank_kernel(ti_ref, dest_ref, cnt_ref, run, tot):
    ph = pl.program_id(0)
    i = pl.program_id(1)
    tm = ti_ref.shape[0]
    ti = ti_ref[...]
    lane = _iota((tm, LANES), 1)
    oh = jnp.zeros((tm, LANES), F32)
    for k in range(TOP_K):
        oh = oh + jnp.where(lane == ti[:, k:k + 1], 1.0, 0.0)
    ones = jnp.ones((SUBLANES, tm), MXU_DTYPE)
    colsum = jnp.dot(ones, oh.astype(MXU_DTYPE), preferred_element_type=F32)[0:1, :]

    @pl.when((ph == 0) & (i == 0))
    def _():
        tot[...] = jnp.zeros(tot.shape, F32)

    @pl.when(ph == 0)
    def _():
        tot[...] = tot[...] + colsum

    @pl.when((ph == 1) & (i == 0))
    def _():
        run[...] = jnp.zeros(run.shape, F32)

    @pl.when(ph == 1)
    def _():
        r = _iota((LANES, LANES), 0)
        c = _iota((LANES, LANES), 1)
        upper = jnp.where(r < c, 1.0, 0.0).astype(MXU_DTYPE)
        starts = _mm_sel(jnp.broadcast_to(tot[...], (SUBLANES, LANES)), upper)[0:1, :]
        rr = _iota((tm, tm), 0)
        cc = _iota((tm, tm), 1)
        lower = jnp.where(rr > cc, 1.0, 0.0).astype(MXU_DTYPE)
        pre = jnp.dot(lower, oh.astype(MXU_DTYPE), preferred_element_type=F32)
        pos = pre + run[...] + starts
        dest = jnp.zeros((tm, LANES), F32)
        for k in range(TOP_K):
            dk = jnp.sum(jnp.where(lane == ti[:, k:k + 1], pos, 0.0), axis=1, keepdims=True)
            dest = jnp.where(lane == k, dk, dest)
        dest_ref[...] = dest.astype(I32)
        run[...] = run[...] + colsum

    cnt_ref[...] = jnp.broadcast_to(tot[...], cnt_ref.shape)


def _rank(topi, tm):
    t_all = topi.shape[0]
    return pl.pallas_call(
        _rank_kernel,
        grid=(2, t_all // tm),
        in_specs=[pl.BlockSpec((tm, LANES), lambda ph, i: (i, 0))],
        out_specs=[pl.BlockSpec((tm, LANES), lambda ph, i: (i * ph, 0)),
                   pl.BlockSpec((SUBLANES, LANES), lambda ph, i: (0, 0))],
        out_shape=[jax.ShapeDtypeStruct((t_all, LANES), I32),
                   jax.ShapeDtypeStruct((SUBLANES, LANES), F32)],
        scratch_shapes=[pltpu.VMEM((1, LANES), F32), pltpu.VMEM((1, LANES), F32)],
        compiler_params=_cp("arbitrary", "arbitrary"),
        name="rank",
    )(topi)


def _dispatch_kernel(dest_ref, h2_ref, xs_ref, sem):
    tm = h2_ref.shape[0]

    def issue(r, carry):
        for k in range(TOP_K):
            pltpu.make_async_copy(h2_ref.at[pl.ds(r, 1)], xs_ref.at[pl.ds(dest_ref[r * TOP_K + k], 1)],
                                  sem).start()
        return carry

    lax.fori_loop(0, tm, issue, 0)
    for _ in range(TOP_K):
        pltpu.make_async_copy(h2_ref, xs_ref.at[pl.ds(0, tm)], sem).wait()


def _dispatch(dest_flat, h2, tm=256):
    t_all, d = h2.shape
    return pl.pallas_call(
        _dispatch_kernel,
        grid=(t_all // tm,),
        in_specs=[pl.BlockSpec((tm * TOP_K,), lambda i: (i,), memory_space=pltpu.SMEM),
                  pl.BlockSpec((tm, d), lambda i: (i, 0))],
        out_specs=pl.BlockSpec(memory_space=pl.ANY),
        out_shape=jax.ShapeDtypeStruct((t_all * TOP_K, d), F32),
        scratch_shapes=[pltpu.SemaphoreType.DMA(())],
        compiler_params=_cp("arbitrary"),
        name="dispatch",
    )(dest_flat, h2)


def _expert_kernel(tile_ref, exp_ref, lo_ref, hi_ref, firste_ref, firstt_ref, valid_ref, slot_ref, nexte_ref,
                   x_ref, wgu_hbm, bgu_ref, wd_hbm, bd_ref, o_ref, wgu_f, wd_f, wgu_s, wd_s, sem, *, layer):
    w = pl.program_id(0)
    tmx = x_ref.shape[0]
    f = wd_s.shape[0]

    def fetch(e, slot):
        return (pltpu.make_async_copy(wgu_hbm.at[layer, e], wgu_f.at[slot], sem.at[slot]),
                pltpu.make_async_copy(wd_hbm.at[layer, e], wd_f.at[slot], sem.at[slot]))

    @pl.when(w == 0)
    def _():
        for cp in fetch(exp_ref[0], slot_ref[0]):
            cp.start()

    @pl.when(firste_ref[w] == 1)
    def _():
        slot = slot_ref[w]
        for cp in fetch(exp_ref[w], slot):
            cp.wait()
        wgu_s[...] = wgu_f[slot].astype(wgu_s.dtype)
        wd_s[...] = wd_f[slot].astype(wd_s.dtype)

        @pl.when(nexte_ref[w] >= 0)
        def _():
            for cp in fetch(nexte_ref[w], 1 - slot):
                cp.start()

    @pl.when(valid_ref[w] == 1)
    def _():
        gu = jnp.dot(x_ref[...].astype(MXU_DTYPE), wgu_s[...], preferred_element_type=F32) + bgu_ref[...]
        g = jnp.minimum(gu[:, 0:f], SWIGLU_LIMIT)
        u = jnp.clip(gu[:, f:2 * f], -SWIGLU_LIMIT, SWIGLU_LIMIT)
        act = g * _sigmoid(SWIGLU_ALPHA * g) * (u + 1.0)
        res = jnp.dot(act.astype(MXU_DTYPE), wd_s[...], preferred_element_type=F32) + bd_ref[...]
        rows = tile_ref[w] * tmx + _iota((tmx, 1), 0)
        mine = (rows >= lo_ref[w]) & (rows < hi_ref[w])

        @pl.when(firstt_ref[w] == 1)
        def _():
            o_ref[...] = jnp.where(mine, res, 0.0)

        @pl.when(firstt_ref[w] == 0)
        def _():
            o_ref[...] = jnp.where(mine, res, o_ref[...])


def _experts(meta, xs, w_gu, b_gu, w_d, b_d, layer, tmx):
    tk, d = xs.shape
    depth, n_e, _, f2 = w_gu.shape
    f = f2 // 2
    n_w = meta[0].shape[0]
    grid_spec = pltpu.PrefetchScalarGridSpec(
        num_scalar_prefetch=9,
        grid=(n_w,),
        in_specs=[pl.BlockSpec((tmx, d), lambda w, tl, ex, *_: (tl[w], 0)),
                  pl.BlockSpec(memory_space=pl.ANY),
                  pl.BlockSpec((None, None, 1, f2), lambda w, tl, ex, *_: (layer, ex[w], 0, 0)),
                  pl.BlockSpec(memory_space=pl.ANY),
                  pl.BlockSpec((None, None, 1, d), lambda w, tl, ex, *_: (layer, ex[w], 0, 0))],
        out_specs=pl.BlockSpec((tmx, d), lambda w, tl, ex, *_: (tl[w], 0)),
        scratch_shapes=[pltpu.VMEM((2, d, f2), F32), pltpu.VMEM((2, f, d), F32),
                        pltpu.VMEM((d, f2), MXU_DTYPE), pltpu.VMEM((f, d), MXU_DTYPE),
                        pltpu.SemaphoreType.DMA((2,))],
    )
    return pl.pallas_call(
        functools.partial(_expert_kernel, layer=layer),
        grid_spec=grid_spec,
        out_shape=jax.ShapeDtypeStruct((tk, d), F32),
        compiler_params=_cp("arbitrary"),
        name="experts",
    )(*meta, xs, w_gu, b_gu.reshape(depth, n_e, 1, f2), w_d, b_d.reshape(depth, n_e, 1, d))


def _expert_schedule(counts, tk, tmx):
    n_e = counts.shape[0]
    n_w = tk // tmx + n_e - 1
    e_ids = jnp.arange(n_e, dtype=I32)
    tri = (e_ids[:, None] <= e_ids[None, :]).astype(I32)
    ends = jnp.sum(counts[:, None] * tri, axis=0)
    starts = ends - counts
    first_tile = starts // tmx
    last_tile = jnp.maximum(ends - 1, 0) // tmx
    n_t = jnp.where(counts > 0, last_tile - first_tile + 1, 0)
    cum = jnp.sum(n_t[:, None] * tri, axis=0)
    total = jnp.sum(n_t)

    def item(wv):
        ex = jnp.sum((cum[None, :] <= wv[:, None]).astype(I32), axis=1)
        oh = (ex[:, None] == e_ids[None, :]).astype(I32)
        pick = lambda v: jnp.sum(oh * v[None, :], axis=1)
        tile = pick(first_tile) + wv - (pick(cum) - pick(n_t))
        return ex, tile, pick(starts), pick(ends)

    w = jnp.arange(n_w, dtype=I32)
    wc = jnp.minimum(w, total - 1)
    ex, tile, lo, hi = item(wc)
    ex_prev, tile_prev, _, _ = item(jnp.maximum(wc - 1, 0))
    valid = w < total
    first_e = valid & ((w == 0) | (ex != ex_prev))
    first_t = valid & ((w == 0) | (tile != tile_prev))
    nonempty = (n_t > 0).astype(I32)
    ordinal = jnp.sum(nonempty[:, None] * tri, axis=0) - nonempty
    later = (e_ids[None, :] > e_ids[:, None]) & (nonempty[None, :] == 1)
    nxt = jnp.min(jnp.where(later, e_ids[None, :], n_e), axis=1)
    nxt = jnp.where(nxt == n_e, -1, nxt)
    oh = (ex[:, None] == e_ids[None, :]).astype(I32)
    slot = jnp.sum(oh * (ordinal % 2)[None, :], axis=1)
    next_e = jnp.sum(oh * nxt[None, :], axis=1)
    return (tile, ex, lo, hi, first_e.astype(I32), first_t.astype(I32), valid.astype(I32),
            slot.astype(I32), next_e.astype(I32))


def _combine_kernel(dest_ref, eo_ref, x1_ref, tg_ref, g2p_ref, g2s_ref, nf_ref, op_ref, os_ref, buf, sem, *,
                    n_p, reps):
    i = pl.program_id(0)
    tm = x1_ref.shape[0]

    def issue(r, carry):
        for k in range(TOP_K):
            pltpu.make_async_copy(eo_ref.at[pl.ds(dest_ref[r * TOP_K + k], 1)], buf.at[k, pl.ds(r, 1)],
                                  sem).start()
        return carry

    lax.fori_loop(0, tm, issue, 0)
    for k in range(TOP_K):
        pltpu.make_async_copy(eo_ref.at[pl.ds(0, tm)], buf.at[k], sem).wait()
    tg = tg_ref[...]
    y = tg[:, 0:1] * buf[0]
    for k in range(1, TOP_K):
        y = y + tg[:, k:k + 1] * buf[k]

    def finish(g2, o_ref):
        o_ref[...] = _rms(x1_ref[...] + g2 * y, nf_ref[...])

    @pl.when(i < n_p)
    def _():
        finish(g2p_ref[...], op_ref)

    @pl.when(i >= n_p)
    def _():
        finish(_tile_rows(g2s_ref[...], reps), os_ref)


def _combine(dest_flat, eo, x1, topg, ada_p, ada_s, norm_final, dims, tm_c=256):
    d, tm, seq, b_p, b_s = dims
    t_all = x1.shape[0]
    tm_c = min(tm_c, tm)
    cdims = (d, tm_c, seq, b_p, b_s)
    n_p = b_p * seq // tm_c
    g2p, g2s = _mod_specs(cdims, 5, 1)
    kern = functools.partial(_combine_kernel, n_p=n_p, reps=max(tm_c // b_s, 1))
    return pl.pallas_call(
        kern,
        grid=(t_all // tm_c,),
        in_specs=[pl.BlockSpec((tm_c * TOP_K,), lambda i: (i,), memory_space=pltpu.SMEM),
                  pl.BlockSpec(memory_space=pl.ANY),
                  pl.BlockSpec((tm_c, d), lambda i: (i, 0)),
                  pl.BlockSpec((tm_c, LANES), lambda i: (i, 0)),
                  g2p, g2s,
                  pl.BlockSpec((1, d), lambda i: (0, 0))],
        out_specs=[pl.BlockSpec((tm_c, d), lambda i: (jnp.minimum(i, n_p - 1), 0)),
                   pl.BlockSpec((tm_c, d), lambda i: (jnp.maximum(i - n_p, 0), 0))],
        out_shape=[jax.ShapeDtypeStruct((n_p * tm_c, d), F32),
                   jax.ShapeDtypeStruct((t_all - n_p * tm_c, d), F32)],
        scratch_shapes=[pltpu.VMEM((TOP_K, tm_c, d), F32), pltpu.SemaphoreType.DMA(())],
        compiler_params=_cp("arbitrary"),
        name="combine",
    )(dest_flat, eo, x1, topg, ada_p, ada_s, norm_final.reshape(1, d))


def _rope_tables(pos):
    half = ROPE_DIMS // 2
    inv_freq = jnp.exp(-math.log(ROPE_THETA) * jnp.arange(half, dtype=F32) / half)
    ang = pos.astype(F32)[:, None] * inv_freq[None, :]
    cos, sin = jnp.cos(ang), jnp.sin(ang)
    n = pos.shape[0]
    rest = HEAD_DIM - ROPE_DIMS
    cos_h = jnp.concatenate([cos, cos, jnp.ones((n, rest), F32)], axis=1)
    sa_h = jnp.concatenate([jnp.zeros((n, half), F32), sin, jnp.zeros((n, rest), F32)], axis=1)
    sb_h = jnp.concatenate([-sin, jnp.zeros((n, half + rest), F32)], axis=1)
    rep = LANES // HEAD_DIM
    return tuple(jnp.tile(t, (1, rep)) for t in (cos_h, sa_h, sb_h))


def _pad_lanes(v, n, value=0.0):
    return jnp.pad(v, ((0, 0), (0, n - v.shape[1])), constant_values=value)


def kernel(x_prompt, x_sample, state_conv_a, state_conv_ssm, state_ssm, cache_k, cache_v, c_prompt, c_sample, w_ada, b_ada, norm_mix, norm_ffn, w_in, conv_a_w, ssm_conv_w, ssm_conv_b, ssm_dt_bias, ssm_a_log, ssm_d, ssm_norm, attn_sinks, w_branch_gate, b_branch_gate, w_proj_a, w_proj_b, w_proj_c, w_out, router_w, router_b, w_gate_up, b_gate_up, w_down, b_down, norm_final):
    b_p, seq, d = x_prompt.shape
    b_s, steps, _ = x_sample.shape
    depth = w_ada.shape[0]
    t_p, t_s = b_p * seq, b_s * steps
    t_all = t_p + t_s
    tm = t_s
    dims = (d, tm, seq, b_p, b_s)
    n_e = router_w.shape[2]
    d_ssm = SSM_HEADS * SSM_HEAD_DIM
    n_bc = 2 * SSM_GROUPS * SSM_STATE
    d_kv = KV_HEADS * HEAD_DIM
    pn = SSM_HEAD_DIM * SSM_STATE
    assert d == d_ssm == ATTN_HEADS * HEAD_DIM and t_p % t_s == 0 and seq % t_s == 0
    assert t_s % 256 == 0 or t_s <= 256
    assert steps <= SUBLANES and PAST_LEN >= WINDOW and seq >= WINDOW and b_s % SUBLANES == 0

    x_all = jnp.concatenate([x_prompt.reshape(t_p, d), x_sample.transpose(1, 0, 2).reshape(t_s, d)], axis=0)
    ada = _ada_all(jnp.concatenate([c_prompt, c_sample], axis=0), w_ada, b_ada)

    o_z = 3 * d
    o_bc = o_z + 2 * d_ssm
    o_dt = o_bc + n_bc
    o_q = o_dt + SSM_HEADS
    o_k = o_q + d
    inv_order = [HEAD_ORDER.index(h) for h in range(ATTN_HEADS)]

    def to_head_order(a, axis):
        return jnp.concatenate([lax.slice_in_dim(a, h * HEAD_DIM, (h + 1) * HEAD_DIM, axis=axis)
                                for h in HEAD_ORDER], axis=axis)

    def from_head_order(a, axis):
        return jnp.concatenate([lax.slice_in_dim(a, i * HEAD_DIM, (i + 1) * HEAD_DIM, axis=axis)
                                for i in inv_order], axis=axis)

    w_pack = jnp.concatenate(
        [w_in[:, :, :o_bc], to_head_order(w_in[:, :, o_q:o_k], 2), w_in[:, :, o_bc:o_dt], w_in[:, :, o_k:],
         w_in[:, :, o_dt:o_q], jnp.zeros((depth, d, N_TAIL - T_DT - SSM_HEADS), F32)], axis=2).astype(MXU_DTYPE)

    hp = jnp.arange(d_ssm) // SSM_HEAD_DIM
    rexp = (jnp.arange(LANES)[:, None] == hp[None, :]).astype(MXU_DTYPE)
    lane_pn = jnp.arange(pn)
    e2 = jnp.stack([(jnp.arange(LANES)[:, None] == (lane_pn // SSM_STATE + lh * SSM_HEAD_DIM)[None, :])
                    for lh in range(2)]).astype(MXU_DTYPE)
    g2 = jnp.transpose(e2, (0, 2, 1))
    qc = min(SSD_CHUNK, seq)
    shift = jnp.stack([jnp.arange(2 * qc)[None, :] == (qc - s + jnp.arange(qc))[:, None]
                       for s in range(1, SSM_CONV_WIDTH)]).astype(MXU_DTYPE)
    state_ssm3 = state_ssm.reshape(depth, b_s, SSM_HEADS * pn)
    cache_k4 = cache_k.reshape(depth, b_s, WINDOW, d_kv)
    cache_v4 = cache_v.reshape(depth, b_s, WINDOW, d_kv)

    tabs_p = _rope_tables(jnp.arange(seq, dtype=I32))
    tabs_s = _rope_tables(jnp.repeat(PAST_LEN + jnp.arange(steps, dtype=I32), b_s))
    eye_kv = jnp.eye(KV_HEADS, dtype=F32)

    outs = [[] for _ in range(10)]
    pending = None
    for l in range(depth):
        ada_p = ada[l, :b_p].reshape(b_p, 1, 6 * d)
        ada_s = ada[l, b_p:]
        ssm_prm = dict(
            wx=ssm_conv_w[l, :, :d_ssm], wbc=ssm_conv_w[l, :, d_ssm:],
            bx=ssm_conv_b[l, :d_ssm].reshape(1, d_ssm), bbc=ssm_conv_b[l, d_ssm:].reshape(1, n_bc),
            dtb=_pad_lanes(ssm_dt_bias[l].reshape(1, SSM_HEADS), LANES),
            alog=_pad_lanes(ssm_a_log[l].reshape(1, SSM_HEADS), LANES),
            de=jnp.repeat(ssm_d[l], SSM_HEAD_DIM).reshape(1, d_ssm),
            gn=ssm_norm[l].reshape(1, d_ssm), rexp=rexp, e2=e2, g2=g2, shift=shift)
        wts = dict(
            wg=w_branch_gate[l].astype(MXU_DTYPE), bg=b_branch_gate[l].reshape(1, 3 * d),
            pa=w_proj_a[l].astype(MXU_DTYPE), pb=w_proj_b[l].astype(MXU_DTYPE),
            pc=to_head_order(w_proj_c[l], 0).astype(MXU_DTYPE), wo=w_out[l].astype(MXU_DTYPE),
            nffn=norm_ffn[l].reshape(1, d),
            rw=_pad_lanes(router_w[l], LANES), rb=_pad_lanes(router_b[l].reshape(1, n_e), LANES, NEG_BIG))

        if pending is None:
            p, p_tail, h = _inproj(x_all, norm_mix[l], ada_p, ada_s, w_pack, l, dims)
        else:
            p, p_tail, h, x_all = _inproj_combine(*pending, norm_mix[l], ada_p, ada_s, w_pack, l, dims)

        ya_p, ca_p = _conva_prompt(p, conv_a_w[l], dims, t_p)
        ca_tm = state_conv_a[l].transpose(1, 0, 2).reshape((CONV_A_WIDTH - 1) * b_s, d)
        ya_s, ca_s = _conva_sample(p, ca_tm, conv_a_w[l], dims, steps)
        outs[0].append(ca_p[:, SUBLANES - (CONV_A_WIDTH - 1):])
        outs[1].append(ca_s.reshape(CONV_A_WIDTH - 1, b_s, d).transpose(1, 0, 2))

        yb_p, csx_p, csbc_p, st_p = _ssd_prompt(p, p_tail, ssm_prm, dims, t_p)
        nsc = SSM_CONV_WIDTH - 1
        outs[2].append(jnp.concatenate([csx_p[:, SUBLANES - nsc:], csbc_p[:, SUBLANES - nsc:]], axis=2))
        hg = SSM_HEADS // SSM_GROUPS
        st_g = jnp.stack([st_p[:, g * SSM_STATE:(g + 1) * SSM_STATE, g * (d_ssm // 2):(g + 1) * (d_ssm // 2)]
                          for g in range(SSM_GROUPS)], axis=1)
        outs[4].append(st_g.reshape(b_p, SSM_GROUPS, SSM_STATE, hg, SSM_HEAD_DIM)
                       .transpose(0, 1, 3, 4, 2).reshape(b_p, SSM_HEADS, SSM_HEAD_DIM, SSM_STATE))
        cs_tm = state_conv_ssm[l].transpose(1, 0, 2).reshape(nsc * b_s, d_ssm + n_bc)
        y_raw, st_s, csx_s, csbc_s = _ssd_sample(p, p_tail, cs_tm[:, :d_ssm], cs_tm[:, d_ssm:],
                                                 state_ssm3, l, ssm_prm, dims, steps)
        yb_s = _gnorm_sample(y_raw, p, ssm_prm["gn"], dims, steps)
        outs[3].append(jnp.concatenate([csx_s, csbc_s], axis=1).reshape(nsc, b_s, d_ssm + n_bc).transpose(1, 0, 2))
        outs[5].append(st_s.reshape(b_s, SSM_HEADS, SSM_HEAD_DIM, SSM_STATE))

        yc_p, k_p, v_p = _attn_prompt(p, p_tail, attn_sinks[l], tabs_p, dims, t_p)
        outs[6].append(k_p.reshape(b_p, WINDOW, KV_HEADS, HEAD_DIM))
        outs[8].append(v_p.reshape(b_p, WINDOW, KV_HEADS, HEAD_DIM))
        q_s, k_s = _rope_sample(p, p_tail, tabs_s, dims, steps)
        v_s = p_tail[t_p:, T_V:T_V + d_kv]
        k_rows = k_s.reshape(steps, b_s, d_kv).transpose(1, 0, 2)
        v_rows = v_s.reshape(steps, b_s, d_kv).transpose(1, 0, 2)
        outs[7].append(k_rows)
        outs[9].append(v_rows)
        pad8 = ((0, 0), (0, SUBLANES - steps), (0, 0))
        q4 = from_head_order(q_s, 1).reshape(steps, b_s, KV_HEADS, Q_PER_KV, HEAD_DIM).transpose(1, 2, 3, 0, 4)
        qbd = (q4.reshape(b_s, KV_HEADS, Q_PER_KV * steps, 1, HEAD_DIM)
               * eye_kv[None, :, None, :, None]).reshape(b_s, KV_HEADS * Q_PER_KV * steps, d_kv)
        sink_col = jnp.broadcast_to(jnp.repeat(attn_sinks[l], steps)[:, None], (ATTN_HEADS * steps, LANES))
        o_s = _attn_sample(qbd, cache_k4, cache_v4, jnp.pad(k_rows, pad8), jnp.pad(v_rows, pad8), l,
                           sink_col, steps)
        yc_s = to_head_order(
            o_s.reshape(b_s, KV_HEADS, Q_PER_KV, steps, HEAD_DIM).transpose(3, 0, 1, 2, 4).reshape(t_s, d), 1
        ).astype(MXU_DTYPE)

        x1, h2, topi, topg = _merge(h, (ya_p, yb_p, yc_p), (ya_s, yb_s, yc_s), x_all, wts, ada_p, ada_s, dims)
        dest, cnt = _rank(topi, tm)
        dest_flat = dest[:, :TOP_K].reshape(t_all * TOP_K)
        xs = _dispatch(dest_flat, h2, min(256, tm))
        tmx = min(256, tm)
        meta = _expert_schedule(cnt[0, :n_e].astype(I32), t_all * TOP_K, tmx)
        eo = _experts(meta, xs, w_gate_up, b_gate_up, w_down, b_down, l, tmx)
        pending = (dest_flat, eo, x1, topg, ada_p, ada_s)

    y_p, y_s = _combine(*pending, norm_final, dims)
    y_prompt = y_p.reshape(b_p, seq, d)
    y_sample = y_s.reshape(steps, b_s, d).transpose(1, 0, 2)
    st = [jnp.stack(o) for o in outs]
    k_sample = jnp.concatenate([cache_k[:, :, steps:], st[7].reshape(depth, b_s, steps, KV_HEADS, HEAD_DIM)], axis=2)
    v_sample = jnp.concatenate([cache_v[:, :, steps:], st[9].reshape(depth, b_s, steps, KV_HEADS, HEAD_DIM)], axis=2)
    return (y_prompt, y_sample, st[0], st[1], st[2], st[3], st[4], st[5], st[6], k_sample, st[8], v_sample)
```

```python
import functools
import math

import jax
import jax.numpy as jnp
from jax import lax
from jax.experimental import pallas as pl
from jax.experimental.pallas import tpu as pltpu

F32 = jnp.float32
BF16 = jnp.bfloat16
I32 = jnp.int32

PAST_LEN = 8192
SSM_HEADS = 16
SSM_HEAD_DIM = 64
SSM_GROUPS = 2
SSM_STATE = 64
SSM_CONV_WIDTH = 4
CONV_A_WIDTH = 3
SSD_CHUNK = 128
ATTN_HEADS = 16
KV_HEADS = 4
HEAD_DIM = 64
Q_PER_KV = ATTN_HEADS // KV_HEADS
WINDOW = 128
ROPE_THETA = 500000.0
ROPE_DIMS = HEAD_DIM // 4
N_EXPERTS = 32
TOP_K = 4
SWIGLU_LIMIT = 7.0
SWIGLU_ALPHA = 1.702
RMS_EPS = 1e-5
NEG_BIG = -1e30

LANES = 128
SUBLANES = 8
VMEM_LIMIT = 56 * 1024 * 1024

MXU_DTYPE = BF16
C_B, C_C, C_X, C_Z, C_XS, C_Q, N_MAIN = 0, 1024, 2048, 3072, 4096, 5120, 6144
T_BC, T_K, T_V, T_DT, N_TAIL = 0, 256, 512, 768, 1024
N_COL = N_MAIN + N_TAIL
HEAD_ORDER = tuple(h for j in range(ATTN_HEADS // 2)
                   for h in ((j // Q_PER_KV) * 2 * Q_PER_KV + j % Q_PER_KV,
                             (j // Q_PER_KV) * 2 * Q_PER_KV + Q_PER_KV + j % Q_PER_KV))


def _cp(*sem):
    return pltpu.CompilerParams(dimension_semantics=sem, vmem_limit_bytes=VMEM_LIMIT)


def _mm(a, b):
    return jnp.dot(a.astype(MXU_DTYPE), b.astype(MXU_DTYPE), preferred_element_type=F32)


def _mm_nt(a, b):
    return lax.dot_general(a.astype(MXU_DTYPE), b.astype(MXU_DTYPE), (((1,), (1,)), ((), ())),
                           preferred_element_type=F32)


def _split(v, n):
    if MXU_DTYPE == F32:
        return [v]
    parts, r = [], v
    for _ in range(n):
        p = r.astype(MXU_DTYPE)
        parts.append(p)
        r = r - p.astype(F32)
    return parts


def _mm_sel(v, sel, n=3):
    acc = None
    for p in _split(v, n):
        t = jnp.dot(p, sel, preferred_element_type=F32)
        acc = t if acc is None else acc + t
    return acc


def _sel_mm(sel, v, n=3):
    acc = None
    for p in _split(v, n):
        t = jnp.dot(sel, p, preferred_element_type=F32)
        acc = t if acc is None else acc + t
    return acc


def _rms(x, w):
    return x * lax.rsqrt(jnp.mean(x * x, axis=-1, keepdims=True) + RMS_EPS) * w


def _sigmoid(x):
    return 1.0 / (1.0 + jnp.exp(-x))


def _silu(x):
    return x * _sigmoid(x)


def _softplus(x):
    return jnp.maximum(x, 0.0) + jnp.log(1.0 + jnp.exp(-jnp.abs(x)))


def _tile_rows(x, n):
    return x if n == 1 else jnp.concatenate([x] * n, axis=0)


def _tile_lanes(x, n):
    return x if n == 1 else jnp.concatenate([x] * n, axis=1)


def _iota(shape, dim):
    return lax.broadcasted_iota(I32, shape, dim)


def _half_mask(lane, upper):
    return lane >= HEAD_DIM if upper else lane < HEAD_DIM


def _ada_kernel(c_ref, w_ref, b_ref, o_ref):
    a = _silu(c_ref[...])
    o_ref[...] = _mm(a, w_ref[...]) + b_ref[...]


def _ada_all(c_all, w_ada, b_ada):
    depth, d, n6 = w_ada.shape
    nb = c_all.shape[0]
    tn = d
    return pl.pallas_call(
        _ada_kernel,
        grid=(depth, n6 // tn),
        in_specs=[pl.BlockSpec((nb, d), lambda l, j: (0, 0)),
                  pl.BlockSpec((None, d, tn), lambda l, j: (l, 0, j)),
                  pl.BlockSpec((None, 1, tn), lambda l, j: (l, 0, j))],
        out_specs=pl.BlockSpec((None, nb, tn), lambda l, j: (l, 0, j)),
        out_shape=jax.ShapeDtypeStruct((depth, nb, n6), F32),
        compiler_params=_cp("arbitrary", "arbitrary"),
        name="ada",
    )(c_all, w_ada, b_ada.reshape(depth, 1, n6))


def _inproj_kernel(x_ref, nw_ref, shp_ref, scp_ref, shs_ref, scs_ref, w_ref, pm_ref, pt_ref, h_ref, *,
                   n_p, reps, n_main):
    i = pl.program_id(0)
    j = pl.program_id(1)

    def make_h(sh, sc):
        y = _rms(x_ref[...], nw_ref[...])
        h_ref[...] = (y * (1.0 + sc) + sh).astype(h_ref.dtype)

    @pl.when((j == 0) & (i < n_p))
    def _():
        make_h(shp_ref[...], scp_ref[...])

    @pl.when((j == 0) & (i >= n_p))
    def _():
        make_h(_tile_rows(shs_ref[...], reps), _tile_rows(scs_ref[...], reps))

    res = jnp.dot(h_ref[...], w_ref[...], preferred_element_type=F32)

    @pl.when(j < n_main)
    def _():
        pm_ref[...] = res.astype(pm_ref.dtype)

    @pl.when(j == n_main)
    def _():
        pt_ref[...] = res


def _mod_specs(dims, col):
    d, tm, seq, b_p, b_s = dims

    def pidx(i, *_):
        return (jnp.minimum(i * tm // seq, b_p - 1), 0, col)

    def sidx(i, *_):
        return (0, col)

    return pl.BlockSpec((None, 1, d), pidx), pl.BlockSpec((b_s, d), sidx)


def _inproj(x_all, norm_w, ada_p, ada_s, w_pack, layer, dims):
    d, tm, seq, b_p, b_s = dims
    tn = N_TAIL
    t_all = x_all.shape[0]
    n_t = t_all // tm
    n_p = b_p * seq // tm
    n_main = N_MAIN // tn
    shp, shs = _mod_specs(dims, 0)
    scp, scs = _mod_specs(dims, 1)
    kern = functools.partial(_inproj_kernel, n_p=n_p, reps=tm // b_s, n_main=n_main)
    return pl.pallas_call(
        kern,
        grid=(n_t, n_main + 1),
        in_specs=[pl.BlockSpec((tm, d), lambda i, j: (i, 0)),
                  pl.BlockSpec((1, d), lambda i, j: (0, 0)),
                  shp, scp, shs, scs,
                  pl.BlockSpec((None, d, tn), lambda i, j: (layer, 0, j))],
        out_specs=[pl.BlockSpec((tm, tn), lambda i, j: (i, jnp.minimum(j, n_main - 1))),
                   pl.BlockSpec((tm, tn), lambda i, j: (i, 0)),
                   pl.BlockSpec((tm, d), lambda i, j: (i, 0))],
        out_shape=[jax.ShapeDtypeStruct((t_all, N_MAIN), MXU_DTYPE),
                   jax.ShapeDtypeStruct((t_all, N_TAIL), F32),
                   jax.ShapeDtypeStruct((t_all, d), MXU_DTYPE)],
        compiler_params=_cp("arbitrary", "arbitrary"),
        name="inproj",
    )(x_all, norm_w.reshape(1, d), ada_p, ada_p, ada_s, ada_s, w_pack)


def _tokens_per_step(tm, n_j):
    per = -(-tm // n_j)
    return per + per % 2


def _inproj_c_kernel(dcur_ref, dnxt_ref, eo_ref, x1_ref, tg_ref, g2p_ref, g2s_ref,
                     nw_ref, shp_ref, scp_ref, shs_ref, scs_ref, w_ref,
                     pm_ref, pt_ref, h_ref, xn_ref, gbuf, sem, *, n_p, reps, n_main):
    i = pl.program_id(0)
    j = pl.program_id(1)
    n_t = pl.num_programs(0)
    n_j = n_main + 1
    tm = x1_ref.shape[0]
    per = _tokens_per_step(tm, n_j)
    surplus = (per * n_j - tm) * TOP_K

    def issue(d_ref, buf, tok0):
        for u in range(per):
            t = tok0 + u
            ts = jnp.minimum(t, tm - 1)
            for k in range(TOP_K):
                pltpu.make_async_copy(eo_ref.at[pl.ds(d_ref[ts * TOP_K + k], 1)],
                                      gbuf.at[buf, k, pl.ds(t, 1)], sem.at[buf]).start()

    def wait(buf):
        for k in range(TOP_K):
            pltpu.make_async_copy(eo_ref.at[pl.ds(0, tm)], gbuf.at[buf, k, pl.ds(0, tm)], sem.at[buf]).wait()
        if surplus:
            pltpu.make_async_copy(eo_ref.at[pl.ds(0, surplus)], gbuf.at[buf, 0, pl.ds(0, surplus)], sem.at[buf]).wait()

    @pl.when((i == 0) & (j == 0))
    def _():
        def body(jj, carry):
            issue(dcur_ref, 0, jj * per)
            return carry
        lax.fori_loop(0, n_j, body, 0)

    nxt = (i + 1) % 2

    def make(g2, sh, sc):
        cur = i % 2
        wait(cur)
        tg = tg_ref[...]
        y = tg[:, 0:1] * gbuf[cur, 0, pl.ds(0, tm), :]
        for k in range(1, TOP_K):
            y = y + tg[:, k:k + 1] * gbuf[cur, k, pl.ds(0, tm), :]
        x = x1_ref[...] + g2 * y
        xn_ref[...] = x
        h_ref[...] = (_rms(x, nw_ref[...]) * (1.0 + sc) + sh).astype(h_ref.dtype)

    @pl.when((j == 0) & (i < n_p))
    def _():
        make(g2p_ref[...], shp_ref[...], scp_ref[...])

    @pl.when((j == 0) & (i >= n_p))
    def _():
        make(_tile_rows(g2s_ref[...], reps), _tile_rows(shs_ref[...], reps), _tile_rows(scs_ref[...], reps))

    issue(dnxt_ref, nxt, j * per)
    res = jnp.dot(h_ref[...], w_ref[...], preferred_element_type=F32)

    @pl.when(j < n_main)
    def _():
        pm_ref[...] = res.astype(pm_ref.dtype)

    @pl.when(j == n_main)
    def _():
        pt_ref[...] = res

    @pl.when((i == n_t - 1) & (j == n_j - 1))
    def _():
        wait(nxt)


def _inproj_combine(dest_flat, eo, x1, topg, ada_prev_p, ada_prev_s, norm_w, ada_p, ada_s, w_pack, layer, dims):
    d, tm, seq, b_p, b_s = dims
    tn = N_TAIL
    t_all = x1.shape[0]
    n_t = t_all // tm
    n_p = b_p * seq // tm
    n_main = N_MAIN // tn
    n_j = n_main + 1
    buf_rows = -(-(_tokens_per_step(tm, n_j) * n_j) // SUBLANES) * SUBLANES
    g2p, g2s = _mod_specs(dims, 5)
    shp, shs = _mod_specs(dims, 0)
    scp, scs = _mod_specs(dims, 1)
    kern = functools.partial(_inproj_c_kernel, n_p=n_p, reps=tm // b_s, n_main=n_main)
    row = lambda n: pl.BlockSpec((tm, n), lambda i, j: (i, 0))
    return pl.pallas_call(
        kern,
        grid=(n_t, n_j),
        in_specs=[pl.BlockSpec((tm * TOP_K,), lambda i, j: (i,), memory_space=pltpu.SMEM),
                  pl.BlockSpec((tm * TOP_K,), lambda i, j: (jnp.minimum(i + 1, n_t - 1),), memory_space=pltpu.SMEM),
                  pl.BlockSpec(memory_space=pl.ANY),
                  row(d), row(LANES), g2p, g2s,
                  pl.BlockSpec((1, d), lambda i, j: (0, 0)),
                  shp, scp, shs, scs,
                  pl.BlockSpec((None, d, tn), lambda i, j: (layer, 0, j))],
        out_specs=[pl.BlockSpec((tm, tn), lambda i, j: (i, jnp.minimum(j, n_main - 1))),
                   pl.BlockSpec((tm, tn), lambda i, j: (i, 0)),
                   row(d), row(d)],
        out_shape=[jax.ShapeDtypeStruct((t_all, N_MAIN), MXU_DTYPE),
                   jax.ShapeDtypeStruct((t_all, N_TAIL), F32),
                   jax.ShapeDtypeStruct((t_all, d), MXU_DTYPE),
                   jax.ShapeDtypeStruct((t_all, d), F32)],
        scratch_shapes=[pltpu.VMEM((2, TOP_K, buf_rows, d), F32), pltpu.SemaphoreType.DMA((2,))],
        compiler_params=_cp("arbitrary", "arbitrary"),
        name="inproj_combine",
    )(dest_flat, dest_flat, eo, x1, topg, ada_prev_p, ada_prev_s, norm_w.reshape(1, d),
      ada_p, ada_p, ada_s, ada_s, w_pack)


def _conva_p_kernel(b_ref, c_ref, x_ref, w_ref, y_ref, st_ref, buf):
    seq = b_ref.shape[0]
    u = c_ref[...].astype(F32) * x_ref[...].astype(F32)
    buf[0:SUBLANES, :] = jnp.zeros((SUBLANES, buf.shape[1]), F32)
    buf[SUBLANES:, :] = u
    w = w_ref[...]
    acc = w[2:3, :] * u
    acc = acc + w[1:2, :] * buf[pl.ds(SUBLANES - 1, seq), :]
    acc = acc + w[0:1, :] * buf[pl.ds(SUBLANES - 2, seq), :]
    y_ref[...] = (b_ref[...].astype(F32) * acc).astype(y_ref.dtype)
    st_ref[...] = buf[pl.ds(seq, SUBLANES), :]


def _conva_prompt(p, conv_w, dims, t_all, tc=256):
    d, tm, seq, b_p, b_s = dims
    nc = d // tc
    return pl.pallas_call(
        _conva_p_kernel,
        grid=(b_p, nc),
        in_specs=[pl.BlockSpec((seq, tc), lambda b, c: (b, C_B // tc + c)),
                  pl.BlockSpec((seq, tc), lambda b, c: (b, C_C // tc + c)),
                  pl.BlockSpec((seq, tc), lambda b, c: (b, C_X // tc + c)),
                  pl.BlockSpec((CONV_A_WIDTH, tc), lambda b, c: (0, c))],
        out_specs=[pl.BlockSpec((seq, tc), lambda b, c: (b, c)),
                   pl.BlockSpec((None, SUBLANES, tc), lambda b, c: (b, 0, c))],
        out_shape=[jax.ShapeDtypeStruct((t_all, d), MXU_DTYPE),
                   jax.ShapeDtypeStruct((b_p, SUBLANES, d), F32)],
        scratch_shapes=[pltpu.VMEM((seq + SUBLANES, tc), F32)],
        compiler_params=_cp("arbitrary", "arbitrary"),
        name="conva_prompt",
    )(p, p, p, conv_w)


def _conva_s_kernel(b_ref, c_ref, x_ref, s_ref, w_ref, y_ref, st_ref, *, b_s, steps):
    ext = jnp.concatenate([s_ref[...], c_ref[...].astype(F32) * x_ref[...].astype(F32)], axis=0)
    w = w_ref[...]
    acc = None
    for k in range(CONV_A_WIDTH):
        t = w[k:k + 1, :] * ext[k * b_s:(k + steps) * b_s]
        acc = t if acc is None else acc + t
    y_ref[...] = (b_ref[...].astype(F32) * acc).astype(y_ref.dtype)
    st_ref[...] = ext[steps * b_s:]


def _conva_sample(p, state_tm, conv_w, dims, steps, tc=256):
    d, tm, seq, b_p, b_s = dims
    t_s = b_s * steps
    rb = (b_p * seq) // t_s
    ns = (CONV_A_WIDTH - 1) * b_s
    kern = functools.partial(_conva_s_kernel, b_s=b_s, steps=steps)
    return pl.pallas_call(
        kern,
        grid=(d // tc,),
        in_specs=[pl.BlockSpec((t_s, tc), lambda c: (rb, C_B // tc + c)),
                  pl.BlockSpec((t_s, tc), lambda c: (rb, C_C // tc + c)),
                  pl.BlockSpec((t_s, tc), lambda c: (rb, C_X // tc + c)),
                  pl.BlockSpec((ns, tc), lambda c: (0, c)),
                  pl.BlockSpec((CONV_A_WIDTH, tc), lambda c: (0, c))],
        out_specs=[pl.BlockSpec((t_s, tc), lambda c: (0, c)),
                   pl.BlockSpec((ns, tc), lambda c: (0, c))],
        out_shape=[jax.ShapeDtypeStruct((t_s, d), MXU_DTYPE),
                   jax.ShapeDtypeStruct((ns, d), F32)],
        compiler_params=_cp("arbitrary"),
        name="conva_sample",
    )(p, p, p, state_tm, conv_w)


def _ssd_p_kernel(z_ref, xs_ref, bc_ref, dt_ref, wx_ref, wbc_ref, bx_ref, bbc_ref, dtb_ref, alog_ref,
                  de_ref, gn_ref, rexp_ref, shift_ref,
                  y_ref, csx_ref, csbc_ref, stout_ref,
                  xbuf, bcbuf, st, ybuf):
    c = pl.program_id(1)
    q = xs_ref.shape[0]
    d_ssm = xs_ref.shape[1]
    half = d_ssm // SSM_GROUPS

    @pl.when(c == 0)
    def _():
        xbuf[...] = jnp.zeros(xbuf.shape, xbuf.dtype)
        bcbuf[0:SUBLANES, :] = jnp.zeros((SUBLANES, bcbuf.shape[1]), F32)
        st[...] = jnp.zeros(st.shape, F32)

    bcbuf[SUBLANES:, :] = bc_ref[...]

    acc = bbc_ref[...]
    for k in range(SSM_CONV_WIDTH):
        acc = acc + wbc_ref[k:k + 1, :] * bcbuf[pl.ds(SUBLANES - (SSM_CONV_WIDTH - 1) + k, q), :]
    bcbuf[0:SUBLANES, :] = bcbuf[pl.ds(q, SUBLANES), :]
    bca = _silu(acc)
    csbc_ref[...] = bcbuf[0:SUBLANES, :]

    dtv = _softplus(dt_ref[:, 0:LANES] + dtb_ref[...])
    a = -jnp.exp(alog_ref[...])
    da = dtv * a
    row = _iota((q, q), 0)
    col = _iota((q, q), 1)
    causal = row >= col
    tri = jnp.where(causal, 1.0, 0.0).astype(MXU_DTYPE)
    acs = _sel_mm(tri, da)
    acs_t = acs.T
    bslab = bca[:, 0:LANES]
    cslab = bca[:, LANES:2 * LANES]
    bt = bslab.T
    lane = _iota((q, LANES), 1)
    srow = _iota((st.shape[0], half), 0)
    pairs_per_group = SSM_HEADS // SSM_GROUPS // 2
    for g in range(SSM_GROUPS):
        cols = slice(g * half, (g + 1) * half)
        xcur = xs_ref[:, cols]
        xext = jnp.concatenate([xbuf[:, cols], xcur], axis=0)
        acc = bx_ref[:, cols] + wx_ref[SSM_CONV_WIDTH - 1:SSM_CONV_WIDTH, cols] * xcur.astype(F32)
        for s in range(1, SSM_CONV_WIDTH):
            k = SSM_CONV_WIDTH - 1 - s
            acc = acc + wx_ref[k:k + 1, cols] * jnp.dot(shift_ref[s - 1], xext, preferred_element_type=F32)
        xbuf[:, cols] = xcur
        xa = _silu(acc)
        csx_ref[:, cols] = xs_ref[q - SUBLANES:q, cols].astype(F32)
        rexp = rexp_ref[:, cols]
        acs_e = _mm_sel(acs, rexp)
        dt_e = _mm_sel(dtv, rexp, n=2)
        last_e = acs_e[q - 1:q, :]
        exp_e = jnp.exp(acs_e)
        w_e = jnp.exp(last_e - acs_e)
        dec_last = jnp.exp(last_e)
        xdt = xa * dt_e
        st_old = st[:, cols]
        y_off = _mm(cslab, st_old) * exp_e
        upd = _mm(bt, xdt * w_e)
        st[:, cols] = dec_last * st_old + jnp.where(_half_mask(srow, g == 1), upd, 0.0)

        cb = _mm(jnp.where(_half_mask(lane, g == 1), cslab, 0.0), bt)
        for jl in range(pairs_per_group):
            jp = g * pairs_per_group + jl
            ms = []
            for hh in (2 * jp, 2 * jp + 1):
                seg = acs[:, hh:hh + 1] - acs_t[hh:hh + 1, :]
                ms.append(cb * jnp.exp(jnp.where(causal, seg, NEG_BIG)))
            lhs = jnp.concatenate(ms, axis=1)
            slab = xdt[:, jl * LANES:(jl + 1) * LANES]
            rhs = jnp.concatenate([jnp.where(lane < SSM_HEAD_DIM, slab, 0.0),
                                   jnp.where(lane >= SSM_HEAD_DIM, slab, 0.0)], axis=0)
            ybuf[:, jp * LANES:(jp + 1) * LANES] = _mm(lhs, rhs)

        u = (ybuf[:, cols] + y_off + de_ref[:, cols] * xa) * _silu(z_ref[:, cols].astype(F32))
        un = u * lax.rsqrt(jnp.mean(u * u, axis=-1, keepdims=True) + RMS_EPS)
        y_ref[:, cols] = (un * gn_ref[:, cols]).astype(y_ref.dtype)

    @pl.when(c == pl.num_programs(1) - 1)
    def _():
        stout_ref[...] = st[...]


def _ssd_prompt(pm, pt, prm, dims, t_all):
    d, tm, seq, b_p, b_s = dims
    q = min(SSD_CHUNK, seq)
    nc = seq // q
    n_bc = 2 * SSM_GROUPS * SSM_STATE
    full = lambda shape: pl.BlockSpec(shape, lambda b, c: tuple(0 for _ in shape))
    return pl.pallas_call(
        _ssd_p_kernel,
        grid=(b_p, nc),
        in_specs=[pl.BlockSpec((q, d), lambda b, c: (b * nc + c, C_Z // d)),
                  pl.BlockSpec((q, d), lambda b, c: (b * nc + c, C_XS // d)),
                  pl.BlockSpec((q, n_bc), lambda b, c: (b * nc + c, T_BC // n_bc)),
                  pl.BlockSpec((q, n_bc), lambda b, c: (b * nc + c, T_DT // n_bc)),
                  full((SSM_CONV_WIDTH, d)), full((SSM_CONV_WIDTH, n_bc)),
                  full((1, d)), full((1, n_bc)), full((1, LANES)), full((1, LANES)),
                  full((1, d)), full((1, d)), full((LANES, d)), full((SSM_CONV_WIDTH - 1, q, 2 * q))],
        out_specs=[pl.BlockSpec((q, d), lambda b, c: (b * nc + c, 0)),
                   pl.BlockSpec((None, SUBLANES, d), lambda b, c: (b, 0, 0)),
                   pl.BlockSpec((None, SUBLANES, n_bc), lambda b, c: (b, 0, 0)),
                   pl.BlockSpec((None, 2 * SSM_STATE, d), lambda b, c: (b, 0, 0))],
        out_shape=[jax.ShapeDtypeStruct((t_all, d), MXU_DTYPE),
                   jax.ShapeDtypeStruct((b_p, SUBLANES, d), F32),
                   jax.ShapeDtypeStruct((b_p, SUBLANES, n_bc), F32),
                   jax.ShapeDtypeStruct((b_p, 2 * SSM_STATE, d), F32)],
        scratch_shapes=[pltpu.VMEM((q, d), MXU_DTYPE),
                        pltpu.VMEM((q + SUBLANES, n_bc), F32),
                        pltpu.VMEM((2 * SSM_STATE, d), F32),
                        pltpu.VMEM((q, d), F32)],
        compiler_params=_cp("arbitrary", "arbitrary"),
        name="ssd_prompt",
    )(pm, pm, pt, pt, prm["wx"], prm["wbc"], prm["bx"], prm["bbc"], prm["dtb"], prm["alog"],
      prm["de"], prm["gn"], prm["rexp"], prm["shift"])


def _ssd_s_kernel(xs_ref, bc_ref, dt_ref, cx_ref, cbc_ref, wx_ref, wbc_ref, bx_ref, bbc_ref, dtb_ref,
                  alog_ref, de_ref, e2_ref, g2_ref, s_ref,
                  y_ref, snew_ref, csx_ref, csbc_ref, *, b_s, steps):
    j = pl.program_id(0)
    pn = SSM_HEAD_DIM * SSM_STATE
    extx = jnp.concatenate([cx_ref[...], xs_ref[...].astype(F32)], axis=0)
    extbc = jnp.concatenate([cbc_ref[...], bc_ref[...]], axis=0)

    def conv(ext, w, b):
        acc = b
        for k in range(SSM_CONV_WIDTH):
            acc = acc + w[k:k + 1, :] * ext[k * b_s:(k + steps) * b_s]
        return _silu(acc)

    xa = conv(extx, wx_ref[...], bx_ref[...])
    bca = conv(extbc, wbc_ref[...], bbc_ref[...])
    csx_ref[...] = extx[steps * b_s:]
    csbc_ref[...] = extbc[steps * b_s:]
    dtv = _softplus(dt_ref[:, 0:LANES] + dtb_ref[...])
    dec = jnp.exp(dtv * (-jnp.exp(alog_ref[...])))
    lane = _iota(dtv.shape, 1)
    grp = (2 * j) // (SSM_HEADS // SSM_GROUPS)
    own = (lane // SSM_STATE) == grp
    bslab = bca[:, 0:LANES]
    cslab = bca[:, LANES:2 * LANES]
    b2 = jnp.where(own, bslab, pltpu.roll(bslab, SSM_STATE, 1))
    c2 = jnp.where(own, cslab, pltpu.roll(cslab, SSM_STATE, 1))
    ys = [None] * steps
    for lh in range(2):
        hh = 2 * j + lh
        dt_col = jnp.sum(jnp.where(lane == hh, dtv, 0.0), axis=1, keepdims=True)
        dec_col = jnp.sum(jnp.where(lane == hh, dec, 0.0), axis=1, keepdims=True)
        s = s_ref[:, lh * pn:(lh + 1) * pn]
        for t in range(steps):
            rows = slice(t * b_s, (t + 1) * b_s)
            xe = _mm_sel(xa[rows] * dt_col[rows], e2_ref[lh], n=2)
            s = dec_col[rows] * s + xe * _tile_lanes(b2[rows], pn // LANES)
            yt = _mm(s * _tile_lanes(c2[rows], pn // LANES), g2_ref[lh])
            ys[t] = yt if ys[t] is None else ys[t] + yt
        snew_ref[:, lh * pn:(lh + 1) * pn] = s
    y_ref[...] = jnp.concatenate(ys, axis=0) + de_ref[...] * xa


def _ssd_sample(pm, pt, cs_x, cs_bc, state3d, layer, prm, dims, steps):
    d, tm, seq, b_p, b_s = dims
    t_s = b_s * steps
    rb = (b_p * seq) // t_s
    n_bc = 2 * SSM_GROUPS * SSM_STATE
    pn = SSM_HEAD_DIM * SSM_STATE
    ns = (SSM_CONV_WIDTH - 1) * b_s
    kern = functools.partial(_ssd_s_kernel, b_s=b_s, steps=steps)
    full = lambda shape: pl.BlockSpec(shape, lambda j: tuple(0 for _ in shape))
    return pl.pallas_call(
        kern,
        grid=(SSM_HEADS // 2,),
        in_specs=[pl.BlockSpec((t_s, LANES), lambda j: (rb, C_XS // LANES + j)),
                  pl.BlockSpec((t_s, n_bc), lambda j: (rb, T_BC // n_bc)),
                  pl.BlockSpec((t_s, n_bc), lambda j: (rb, T_DT // n_bc)),
                  pl.BlockSpec((ns, LANES), lambda j: (0, j)),
                  full((ns, n_bc)),
                  pl.BlockSpec((SSM_CONV_WIDTH, LANES), lambda j: (0, j)),
                  full((SSM_CONV_WIDTH, n_bc)),
                  pl.BlockSpec((1, LANES), lambda j: (0, j)),
                  full((1, n_bc)), full((1, LANES)), full((1, LANES)),
                  pl.BlockSpec((1, LANES), lambda j: (0, j)),
                  full((2, LANES, pn)), full((2, pn, LANES)),
                  pl.BlockSpec((None, b_s, 2 * pn), lambda j: (layer, 0, j))],
        out_specs=[pl.BlockSpec((t_s, LANES), lambda j: (0, j)),
                   pl.BlockSpec((b_s, 2 * pn), lambda j: (0, j)),
                   pl.BlockSpec((ns, LANES), lambda j: (0, j)),
                   full((ns, n_bc))],
        out_shape=[jax.ShapeDtypeStruct((t_s, d), F32),
                   jax.ShapeDtypeStruct(state3d.shape[1:], F32),
                   jax.ShapeDtypeStruct((ns, d), F32),
                   jax.ShapeDtypeStruct((ns, n_bc), F32)],
        compiler_params=_cp("arbitrary"),
        name="ssd_sample",
    )(pm, pt, pt, cs_x, cs_bc, prm["wx"], prm["wbc"], prm["bx"], prm["bbc"], prm["dtb"], prm["alog"],
      prm["de"], prm["e2"], prm["g2"], state3d)


def _gnorm_s_kernel(y_ref, z_ref, gn_ref, o_ref):
    u = y_ref[...] * _silu(z_ref[...].astype(F32))
    half = u.shape[1] // SSM_GROUPS
    outs = []
    for g in range(SSM_GROUPS):
        ug = u[:, g * half:(g + 1) * half]
        outs.append(ug * lax.rsqrt(jnp.mean(ug * ug, axis=-1, keepdims=True) + RMS_EPS))
    o_ref[...] = (jnp.concatenate(outs, axis=1) * gn_ref[...]).astype(o_ref.dtype)


def _gnorm_sample(y_raw, p, gn, dims, steps):
    d, tm, seq, b_p, b_s = dims
    t_s = b_s * steps
    rb = (b_p * seq) // t_s
    return pl.pallas_call(
        _gnorm_s_kernel,
        grid=(1,),
        in_specs=[pl.BlockSpec((t_s, d), lambda i: (0, 0)),
                  pl.BlockSpec((t_s, d), lambda i: (rb, C_Z // d)),
                  pl.BlockSpec((1, d), lambda i: (0, 0))],
        out_specs=pl.BlockSpec((t_s, d), lambda i: (0, 0)),
        out_shape=jax.ShapeDtypeStruct((t_s, d), MXU_DTYPE),
        compiler_params=_cp("arbitrary"),
        name="gnorm_sample",
    )(y_raw, p, gn)


def _rope(x, cos_f, sin_a, sin_b):
    n = x.shape[1] // cos_f.shape[1]
    half = ROPE_DIMS // 2
    return (x * _tile_lanes(cos_f, n) + pltpu.roll(x, half, 1) * _tile_lanes(sin_a, n)
            + pltpu.roll(x, x.shape[1] - half, 1) * _tile_lanes(sin_b, n))


def _attn_p_kernel(sink_ref, q_ref, k_ref, v_ref, cos_ref, sa_ref, sb_ref,
                   o_ref, kn_ref, vn_ref, kprev, vprev):
    c = pl.program_id(1)
    w = q_ref.shape[0]
    cos_f, sin_a, sin_b = cos_ref[...], sa_ref[...], sb_ref[...]
    scale = HEAD_DIM ** -0.5
    k = _rope(k_ref[...], cos_f, sin_a, sin_b)
    v = v_ref[...]

    @pl.when(c == 0)
    def _():
        kprev[...] = jnp.zeros(kprev.shape, F32)
        vprev[...] = jnp.zeros(vprev.shape, F32)

    kk = jnp.concatenate([kprev[...], k], axis=0)
    vv = jnp.concatenate([vprev[...], v], axis=0)
    key = _iota((2 * w, w), 0)
    qry = _iota((2 * w, w), 1)
    valid = (key > qry) & (key <= qry + w) & ((c > 0) | (key >= w))
    valid = _tile_lanes(valid.astype(F32), Q_PER_KV) > 0.5
    lane_k = _iota((2 * w, LANES), 1)
    sub_v = _iota((LANES, 2 * w), 0)
    for a in range(KV_HEADS // 2):
        slabs = range(a * Q_PER_KV, (a + 1) * Q_PER_KV)
        qst = jnp.concatenate(
            [(_rope(q_ref[:, LANES * j:LANES * (j + 1)].astype(F32), cos_f, sin_a, sin_b) * scale).astype(MXU_DTYPE)
             for j in slabs], axis=0)
        ksl = kk[:, LANES * a:LANES * (a + 1)]
        vt = vv[:, LANES * a:LANES * (a + 1)].T
        o_t = None
        for hk in range(2):
            ks = jnp.where(_half_mask(lane_k, hk == 1), ksl, 0.0).astype(MXU_DTYPE)
            vs = jnp.where(_half_mask(sub_v, hk == 1), vt, 0.0).astype(MXU_DTYPE)
            sink = jnp.concatenate([jnp.full((1, w), sink_ref[HEAD_ORDER[2 * j + hk]], F32) for j in slabs], axis=1)
            s = lax.dot_general(ks, qst, (((1,), (1,)), ((), ())), preferred_element_type=F32)
            s = jnp.where(valid, s, NEG_BIG)
            m = jnp.maximum(jnp.max(s, axis=0, keepdims=True), sink)
            pr = jnp.exp(s - m)
            den = jnp.sum(pr, axis=0, keepdims=True) + jnp.exp(sink - m)
            pn = (pr * (1.0 / den)).astype(MXU_DTYPE)
            o = jnp.dot(vs, pn, preferred_element_type=F32)
            o_t = o if o_t is None else o_t + o
        o_all = o_t.T
        for g, j in enumerate(slabs):
            o_ref[:, LANES * j:LANES * (j + 1)] = o_all[g * w:(g + 1) * w].astype(o_ref.dtype)
    kprev[...] = k
    vprev[...] = v
    kn_ref[...] = k
    vn_ref[...] = v


def _attn_prompt(pm, pt, sinks, tabs, dims, t_all):
    d, tm, seq, b_p, b_s = dims
    w = WINDOW
    nb = seq // w
    d_kv = KV_HEADS * HEAD_DIM
    tab_spec = pl.BlockSpec((w, LANES), lambda b, c: (c, 0))
    return pl.pallas_call(
        _attn_p_kernel,
        grid=(b_p, nb),
        in_specs=[pl.BlockSpec(memory_space=pltpu.SMEM),
                  pl.BlockSpec((w, d), lambda b, c: (b * nb + c, C_Q // d)),
                  pl.BlockSpec((w, d_kv), lambda b, c: (b * nb + c, T_K // d_kv)),
                  pl.BlockSpec((w, d_kv), lambda b, c: (b * nb + c, T_V // d_kv)),
                  tab_spec, tab_spec, tab_spec],
        out_specs=[pl.BlockSpec((w, d), lambda b, c: (b * nb + c, 0)),
                   pl.BlockSpec((None, w, d_kv), lambda b, c: (b, 0, 0)),
                   pl.BlockSpec((None, w, d_kv), lambda b, c: (b, 0, 0))],
        out_shape=[jax.ShapeDtypeStruct((t_all, d), MXU_DTYPE),
                   jax.ShapeDtypeStruct((b_p, w, d_kv), F32),
                   jax.ShapeDtypeStruct((b_p, w, d_kv), F32)],
        scratch_shapes=[pltpu.VMEM((w, d_kv), F32), pltpu.VMEM((w, d_kv), F32)],
        compiler_params=_cp("arbitrary", "arbitrary"),
        name="attn_prompt",
    )(sinks, pm, pt, pt, *tabs)


def _rope_s_kernel(q_ref, k_ref, cos_ref, sa_ref, sb_ref, qo_ref, ko_ref):
    cos_f, sin_a, sin_b = cos_ref[...], sa_ref[...], sb_ref[...]
    qo_ref[...] = _rope(q_ref[...].astype(F32), cos_f, sin_a, sin_b)
    ko_ref[...] = _rope(k_ref[...], cos_f, sin_a, sin_b)


def _rope_sample(pm, pt, tabs, dims, steps):
    d, tm, seq, b_p, b_s = dims
    t_s = b_s * steps
    rb = (b_p * seq) // t_s
    d_kv = KV_HEADS * HEAD_DIM
    tab_spec = pl.BlockSpec((t_s, LANES), lambda i: (0, 0))
    return pl.pallas_call(
        _rope_s_kernel,
        grid=(1,),
        in_specs=[pl.BlockSpec((t_s, d), lambda i: (rb, C_Q // d)),
                  pl.BlockSpec((t_s, d_kv), lambda i: (rb, T_K // d_kv)),
                  tab_spec, tab_spec, tab_spec],
        out_specs=[pl.BlockSpec((t_s, d), lambda i: (0, 0)),
                   pl.BlockSpec((t_s, d_kv), lambda i: (0, 0))],
        out_shape=[jax.ShapeDtypeStruct((t_s, d), F32),
                   jax.ShapeDtypeStruct((t_s, d_kv), F32)],
        compiler_params=_cp("arbitrary"),
        name="rope_sample",
    )(pm, pt, *tabs)


def _attn_s_kernel(q_ref, kn_ref, vn_ref, ko_ref, vo_ref, sink_ref, valid_ref, o_ref):
    tb, nq, d_kv = q_ref.shape
    w = kn_ref.shape[1]
    n_old = ko_ref.shape[1]
    pad = jnp.zeros((w - n_old, d_kv), F32)
    valid = valid_ref[...] > 0.5
    sink = sink_ref[:, 0:1]
    scale = HEAD_DIM ** -0.5
    orow = _iota((nq, HEAD_DIM), 0)
    rows_per_kv = nq // KV_HEADS
    for b in range(tb):
        kk = jnp.concatenate([kn_ref[b], ko_ref[b], pad], axis=0)
        vv = jnp.concatenate([vn_ref[b], vo_ref[b], pad], axis=0)
        s = _mm_nt(q_ref[b], kk) * scale
        s = jnp.where(valid, s, NEG_BIG)
        m = jnp.maximum(jnp.max(s, axis=-1, keepdims=True), sink)
        pr = jnp.exp(s - m)
        den = jnp.sum(pr, axis=-1, keepdims=True) + jnp.exp(sink - m)
        o = _mm(pr, vv) / den
        acc = jnp.zeros((nq, HEAD_DIM), F32)
        for kh in range(KV_HEADS):
            mine = (orow >= kh * rows_per_kv) & (orow < (kh + 1) * rows_per_kv)
            acc = acc + jnp.where(mine, o[:, kh * HEAD_DIM:(kh + 1) * HEAD_DIM], 0.0)
        o_ref[b] = acc


def _attn_sample(qbd, k_cache, v_cache, k_new, v_new, layer, sink_col, steps, tb=8):
    b_s, nq, d_kv = qbd.shape
    w = k_cache.shape[2]
    t = (jnp.arange(nq) % steps)[:, None]
    col = jnp.arange(2 * w)[None, :]
    valid = jnp.where(col < w, col > t, col - w <= t).astype(F32)
    return pl.pallas_call(
        _attn_s_kernel,
        grid=(b_s // tb,),
        in_specs=[pl.BlockSpec((tb, nq, d_kv), lambda i: (i, 0, 0)),
                  pl.BlockSpec((None, tb, w, d_kv), lambda i: (layer, i, 0, 0)),
                  pl.BlockSpec((None, tb, w, d_kv), lambda i: (layer, i, 0, 0)),
                  pl.BlockSpec((tb, SUBLANES, d_kv), lambda i: (i, 0, 0)),
                  pl.BlockSpec((tb, SUBLANES, d_kv), lambda i: (i, 0, 0)),
                  pl.BlockSpec((nq, LANES), lambda i: (0, 0)),
                  pl.BlockSpec((nq, 2 * w), lambda i: (0, 0))],
        out_specs=pl.BlockSpec((tb, nq, HEAD_DIM), lambda i: (i, 0, 0)),
        out_shape=jax.ShapeDtypeStruct((b_s, nq, HEAD_DIM), F32),
        compiler_params=_cp("arbitrary"),
        name="attn_sample",
    )(qbd, k_cache, v_cache, k_new, v_new, sink_col, valid)


def _merge_kernel(h_ref, yap_ref, ybp_ref, ycp_ref, yas_ref, ybs_ref, ycs_ref, x_ref,
                  wg_ref, bg_ref, pa_ref, pb_ref, pc_ref, wo_ref,
                  g1p_ref, g1s_ref, nw_ref, shp_ref, scp_ref, shs_ref, scs_ref, rw_ref, rb_ref,
                  x1_ref, h2_ref, ti_ref, tg_ref, *, n_p, reps):
    i = pl.program_id(0)
    d = x_ref.shape[1]
    is_p = i < n_p
    ya = jnp.where(is_p, yap_ref[...], yas_ref[...])
    yb = jnp.where(is_p, ybp_ref[...], ybs_ref[...])
    yc = jnp.where(is_p, ycp_ref[...], ycs_ref[...])
    g = _sigmoid(jnp.dot(h_ref[...], wg_ref[...], preferred_element_type=F32) + bg_ref[...])
    merged = (g[:, 0:d] * _mm(ya, pa_ref[...]) + g[:, d:2 * d] * _mm(yb, pb_ref[...])
              + g[:, 2 * d:3 * d] * _mm(yc, pc_ref[...]))
    mix = _mm(merged, wo_ref[...])

    def finish(g1, sh, sc):
        x1 = x_ref[...] + g1 * mix
        x1_ref[...] = x1
        h2_ref[...] = _rms(x1, nw_ref[...]) * (1.0 + sc) + sh

    @pl.when(i < n_p)
    def _():
        finish(g1p_ref[...], shp_ref[...], scp_ref[...])

    @pl.when(i >= n_p)
    def _():
        finish(_tile_rows(g1s_ref[...], reps), _tile_rows(shs_ref[...], reps), _tile_rows(scs_ref[...], reps))

    h2 = h2_ref[...]
    rw = rw_ref[...]
    hs = _split(h2, 2)
    ws = _split(rw, 2)
    if len(hs) == 1:
        logits = jnp.dot(h2, rw, preferred_element_type=F32)
    else:
        logits = (jnp.dot(hs[0], ws[0], preferred_element_type=F32)
                  + jnp.dot(hs[0], ws[1], preferred_element_type=F32)
                  + jnp.dot(hs[1], ws[0], preferred_element_type=F32))
    logits = logits + rb_ref[...]
    lane = _iota(logits.shape, 1).astype(F32)
    vals, idxs = [], []
    cur = logits
    for _ in range(TOP_K):
        mx = jnp.max(cur, axis=-1, keepdims=True)
        ix = jnp.min(jnp.where(cur == mx, lane, float(LANES)), axis=-1, keepdims=True)
        vals.append(mx)
        idxs.append(ix)
        cur = jnp.where(lane == ix, -jnp.inf, cur)
    es = [jnp.exp(v - vals[0]) for v in vals]
    den = es[0]
    for e in es[1:]:
        den = den + e
    ti = jnp.zeros(logits.shape, F32)
    tg = jnp.zeros(logits.shape, F32)
    for k in range(TOP_K):
        ti = jnp.where(lane == float(k), idxs[k], ti)
        tg = jnp.where(lane == float(k), es[k] / den, tg)
    ti_ref[...] = ti.astype(I32)
    tg_ref[...] = tg


def _merge(h, ys_p, ys_s, x_all, wts, ada_p, ada_s, dims, tm_m=512):
    d, tm, seq, b_p, b_s = dims
    t_all = x_all.shape[0]
    tm_m = min(tm_m, tm)
    mdims = (d, tm_m, seq, b_p, b_s)
    n_p = b_p * seq // tm_m
    g1p, g1s = _mod_specs(mdims, 2)
    shp, shs = _mod_specs(mdims, 3)
    scp, scs = _mod_specs(mdims, 4)
    row = lambda n: pl.BlockSpec((tm_m, n), lambda i: (i, 0))
    row_p = pl.BlockSpec((tm_m, d), lambda i: (jnp.minimum(i, n_p - 1), 0))
    row_s = pl.BlockSpec((tm_m, d), lambda i: (jnp.maximum(i - n_p, 0), 0))
    const = lambda shape: pl.BlockSpec(shape, lambda i: tuple(0 for _ in shape), pipeline_mode=pl.Buffered(1))
    kern = functools.partial(_merge_kernel, n_p=n_p, reps=max(tm_m // b_s, 1))
    return pl.pallas_call(
        kern,
        grid=(t_all // tm_m,),
        in_specs=[row(d), row_p, row_p, row_p, row_s, row_s, row_s, row(d),
                  const((d, 3 * d)), const((1, 3 * d)), const((d, d)), const((d, d)), const((d, d)),
                  const((d, d)),
                  g1p, g1s, const((1, d)), shp, scp, shs, scs,
                  const((d, LANES)), const((1, LANES))],
        out_specs=[row(d), row(d), row(LANES), row(LANES)],
        out_shape=[jax.ShapeDtypeStruct((t_all, d), F32),
                   jax.ShapeDtypeStruct((t_all, d), F32),
                   jax.ShapeDtypeStruct((t_all, LANES), I32),
                   jax.ShapeDtypeStruct((t_all, LANES), F32)],
        compiler_params=_cp("arbitrary"),
        name="merge",
    )(h, *ys_p, *ys_s, x_all, wts["wg"], wts["bg"], wts["pa"], wts["pb"], wts["pc"], wts["wo"],
      ada_p, ada_s, wts["nffn"], ada_p, ada_p, ada_s, ada_s, wts["rw"], wts["rb"])


def _rank_kernel(ti_ref, dest_ref, cnt_ref, run, tot):
    ph = pl.program_id(0)
    i = pl.program_id(1)
    tm = ti_ref.shape[0]
    ti = ti_ref[...]
    lane = _iota((tm, LANES), 1)
    oh = jnp.zeros((tm, LANES), F32)
    for k in range(TOP_K):
        oh = oh + jnp.where(lane == ti[:, k:k + 1], 1.0, 0.0)
    ones = jnp.ones((SUBLANES, tm), MXU_DTYPE)
    colsum = jnp.dot(ones, oh.astype(MXU_DTYPE), preferred_element_type=F32)[0:1, :]

    @pl.when((ph == 0) & (i == 0))
    def _():
        tot[...] = jnp.zeros(tot.shape, F32)

    @pl.when(ph == 0)
    def _():
        tot[...] = tot[...] + colsum

    @pl.when((ph == 1) & (i == 0))
    def _():
        run[...] = jnp.zeros(run.shape, F32)

    @pl.when(ph == 1)
    def _():
        r = _iota((LANES, LANES), 0)
        c = _iota((LANES, LANES), 1)
        upper = jnp.where(r < c, 1.0, 0.0).astype(MXU_DTYPE)
        starts = _mm_sel(jnp.broadcast_to(tot[...], (SUBLANES, LANES)), upper)[0:1, :]
        rr = _iota((tm, tm), 0)
        cc = _iota((tm, tm), 1)
        lower = jnp.where(rr > cc, 1.0, 0.0).astype(MXU_DTYPE)
        pre = jnp.dot(lower, oh.astype(MXU_DTYPE), preferred_element_type=F32)
        pos = pre + run[...] + starts
        dest = jnp.zeros((tm, LANES), F32)
        for k in range(TOP_K):
            dk = jnp.sum(jnp.where(lane == ti[:, k:k + 1], pos, 0.0), axis=1, keepdims=True)
            dest = jnp.where(lane == k, dk, dest)
        dest_ref[...] = dest.astype(I32)
        run[...] = run[...] + colsum

    cnt_ref[...] = jnp.broadcast_to(tot[...], cnt_ref.shape)


def _rank(topi, tm):
    t_all = topi.shape[0]
    return pl.pallas_call(
        _rank_kernel,
        grid=(2, t_all // tm),
        in_specs=[pl.BlockSpec((tm, LANES), lambda ph, i: (i, 0))],
        out_specs=[pl.BlockSpec((tm, LANES), lambda ph, i: (i * ph, 0)),
                   pl.BlockSpec((SUBLANES, LANES), lambda ph, i: (0, 0))],
        out_shape=[jax.ShapeDtypeStruct((t_all, LANES), I32),
                   jax.ShapeDtypeStruct((SUBLANES, LANES), F32)],
        scratch_shapes=[pltpu.VMEM((1, LANES), F32), pltpu.VMEM((1, LANES), F32)],
        compiler_params=_cp("arbitrary", "arbitrary"),
        name="rank",
    )(topi)


def _dispatch_kernel(dest_ref, h2_ref, xs_ref, sem):
    tm = h2_ref.shape[0]

    def issue(r, carry):
        for k in range(TOP_K):
            pltpu.make_async_copy(h2_ref.at[pl.ds(r, 1)], xs_ref.at[pl.ds(dest_ref[r * TOP_K + k], 1)],
                                  sem).start()
        return carry

    lax.fori_loop(0, tm, issue, 0)
    for _ in range(TOP_K):
        pltpu.make_async_copy(h2_ref, xs_ref.at[pl.ds(0, tm)], sem).wait()


def _dispatch(dest_flat, h2, tm=256):
    t_all, d = h2.shape
    return pl.pallas_call(
        _dispatch_kernel,
        grid=(t_all // tm,),
        in_specs=[pl.BlockSpec((tm * TOP_K,), lambda i: (i,), memory_space=pltpu.SMEM),
                  pl.BlockSpec((tm, d), lambda i: (i, 0))],
        out_specs=pl.BlockSpec(memory_space=pl.ANY),
        out_shape=jax.ShapeDtypeStruct((t_all * TOP_K, d), F32),
        scratch_shapes=[pltpu.SemaphoreType.DMA(())],
        compiler_params=_cp("arbitrary"),
        name="dispatch",
    )(dest_flat, h2)


def _expert_kernel(tile_ref, exp_ref, lo_ref, hi_ref, firste_ref, firstt_ref, valid_ref, slot_ref, nexte_ref,
                   x_ref, wgu_hbm, bgu_ref, wd_hbm, bd_ref, o_ref, wgu_f, wd_f, wgu_s, wd_s, sem, *, layer):
    w = pl.program_id(0)
    tmx = x_ref.shape[0]
    f = wd_s.shape[0]

    def fetch(e, slot):
        return (pltpu.make_async_copy(wgu_hbm.at[layer, e], wgu_f.at[slot], sem.at[slot]),
                pltpu.make_async_copy(wd_hbm.at[layer, e], wd_f.at[slot], sem.at[slot]))

    @pl.when(w == 0)
    def _():
        for cp in fetch(exp_ref[0], slot_ref[0]):
            cp.start()

    @pl.when(firste_ref[w] == 1)
    def _():
        slot = slot_ref[w]
        for cp in fetch(exp_ref[w], slot):
            cp.wait()
        wgu_s[...] = wgu_f[slot].astype(wgu_s.dtype)
        wd_s[...] = wd_f[slot].astype(wd_s.dtype)

        @pl.when(nexte_ref[w] >= 0)
        def _():
            for cp in fetch(nexte_ref[w], 1 - slot):
                cp.start()

    @pl.when(valid_ref[w] == 1)
    def _():
        gu = jnp.dot(x_ref[...].astype(MXU_DTYPE), wgu_s[...], preferred_element_type=F32) + bgu_ref[...]
        g = jnp.minimum(gu[:, 0:f], SWIGLU_LIMIT)
        u = jnp.clip(gu[:, f:2 * f], -SWIGLU_LIMIT, SWIGLU_LIMIT)
        act = g * _sigmoid(SWIGLU_ALPHA * g) * (u + 1.0)
        res = jnp.dot(act.astype(MXU_DTYPE), wd_s[...], preferred_element_type=F32) + bd_ref[...]
        rows = tile_ref[w] * tmx + _iota((tmx, 1), 0)
        mine = (rows >= lo_ref[w]) & (rows < hi_ref[w])

        @pl.when(firstt_ref[w] == 1)
        def _():
            o_ref[...] = jnp.where(mine, res, 0.0)

        @pl.when(firstt_ref[w] == 0)
        def _():
            o_ref[...] = jnp.where(mine, res, o_ref[...])


def _experts(meta, xs, w_gu, b_gu, w_d, b_d, layer, tmx):
    tk, d = xs.shape
    depth, n_e, _, f2 = w_gu.shape
    f = f2 // 2
    n_w = meta[0].shape[0]
    grid_spec = pltpu.PrefetchScalarGridSpec(
        num_scalar_prefetch=9,
        grid=(n_w,),
        in_specs=[pl.BlockSpec((tmx, d), lambda w, tl, ex, *_: (tl[w], 0)),
                  pl.BlockSpec(memory_space=pl.ANY),
                  pl.BlockSpec((None, None, 1, f2), lambda w, tl, ex, *_: (layer, ex[w], 0, 0)),
                  pl.BlockSpec(memory_space=pl.ANY),
                  pl.BlockSpec((None, None, 1, d), lambda w, tl, ex, *_: (layer, ex[w], 0, 0))],
        out_specs=pl.BlockSpec((tmx, d), lambda w, tl, ex, *_: (tl[w], 0)),
        scratch_shapes=[pltpu.VMEM((2, d, f2), F32), pltpu.VMEM((2, f, d), F32),
                        pltpu.VMEM((d, f2), MXU_DTYPE), pltpu.VMEM((f, d), MXU_DTYPE),
                        pltpu.SemaphoreType.DMA((2,))],
    )
    return pl.pallas_call(
        functools.partial(_expert_kernel, layer=layer),
        grid_spec=grid_spec,
        out_shape=jax.ShapeDtypeStruct((tk, d), F32),
        compiler_params=_cp("arbitrary"),
        name="experts",
    )(*meta, xs, w_gu, b_gu.reshape(depth, n_e, 1, f2), w_d, b_d.reshape(depth, n_e, 1, d))


def _expert_schedule(counts, tk, tmx):
    n_e = counts.shape[0]
    n_w = tk // tmx + n_e - 1
    e_ids = jnp.arange(n_e, dtype=I32)
    tri = (e_ids[:, None] <= e_ids[None, :]).astype(I32)
    ends = jnp.sum(counts[:, None] * tri, axis=0)
    starts = ends - counts
    first_tile = starts // tmx
    last_tile = jnp.maximum(ends - 1, 0) // tmx
    n_t = jnp.where(counts > 0, last_tile - first_tile + 1, 0)
    cum = jnp.sum(n_t[:, None] * tri, axis=0)
    total = jnp.sum(n_t)

    def item(wv):
        ex = jnp.sum((cum[None, :] <= wv[:, None]).astype(I32), axis=1)
        oh = (ex[:, None] == e_ids[None, :]).astype(I32)
        pick = lambda v: jnp.sum(oh * v[None, :], axis=1)
        tile = pick(first_tile) + wv - (pick(cum) - pick(n_t))
        return ex, tile, pick(starts), pick(ends)

    w = jnp.arange(n_w, dtype=I32)
    wc = jnp.minimum(w, total - 1)
    ex, tile, lo, hi = item(wc)
    ex_prev, tile_prev, _, _ = item(jnp.maximum(wc - 1, 0))
    valid = w < total
    first_e = valid & ((w == 0) | (ex != ex_prev))
    first_t = valid & ((w == 0) | (tile != tile_prev))
    nonempty = (n_t > 0).astype(I32)
    ordinal = jnp.sum(nonempty[:, None] * tri, axis=0) - nonempty
    later = (e_ids[None, :] > e_ids[:, None]) & (nonempty[None, :] == 1)
    nxt = jnp.min(jnp.where(later, e_ids[None, :], n_e), axis=1)
    nxt = jnp.where(nxt == n_e, -1, nxt)
    oh = (ex[:, None] == e_ids[None, :]).astype(I32)
    slot = jnp.sum(oh * (ordinal % 2)[None, :], axis=1)
    next_e = jnp.sum(oh * nxt[None, :], axis=1)
    return (tile, ex, lo, hi, first_e.astype(I32), first_t.astype(I32), valid.astype(I32),
            slot.astype(I32), next_e.astype(I32))


def _combine_kernel(dest_ref, eo_ref, x1_ref, tg_ref, g2p_ref, g2s_ref, nf_ref, op_ref, os_ref, buf, sem, *,
                    n_p, reps):
    i = pl.program_id(0)
    tm = x1_ref.shape[0]

    def issue(r, carry):
        for k in range(TOP_K):
            pltpu.make_async_copy(eo_ref.at[pl.ds(dest_ref[r * TOP_K + k], 1)], buf.at[k, pl.ds(r, 1)],
                                  sem).start()
        return carry

    lax.fori_loop(0, tm, issue, 0)
    for k in range(TOP_K):
        pltpu.make_async_copy(eo_ref.at[pl.ds(0, tm)], buf.at[k], sem).wait()
    tg = tg_ref[...]
    y = tg[:, 0:1] * buf[0]
    for k in range(1, TOP_K):
        y = y + tg[:, k:k + 1] * buf[k]

    def finish(g2, o_ref):
        o_ref[...] = _rms(x1_ref[...] + g2 * y, nf_ref[...])

    @pl.when(i < n_p)
    def _():
        finish(g2p_ref[...], op_ref)

    @pl.when(i >= n_p)
    def _():
        finish(_tile_rows(g2s_ref[...], reps), os_ref)


def _combine(dest_flat, eo, x1, topg, ada_p, ada_s, norm_final, dims, tm_c=256):
    d, tm, seq, b_p, b_s = dims
    t_all = x1.shape[0]
    tm_c = min(tm_c, tm)
    cdims = (d, tm_c, seq, b_p, b_s)
    n_p = b_p * seq // tm_c
    g2p, g2s = _mod_specs(cdims, 5)
    kern = functools.partial(_combine_kernel, n_p=n_p, reps=max(tm_c // b_s, 1))
    return pl.pallas_call(
        kern,
        grid=(t_all // tm_c,),
        in_specs=[pl.BlockSpec((tm_c * TOP_K,), lambda i: (i,), memory_space=pltpu.SMEM),
                  pl.BlockSpec(memory_space=pl.ANY),
                  pl.BlockSpec((tm_c, d), lambda i: (i, 0)),
                  pl.BlockSpec((tm_c, LANES), lambda i: (i, 0)),
                  g2p, g2s,
                  pl.BlockSpec((1, d), lambda i: (0, 0))],
        out_specs=[pl.BlockSpec((tm_c, d), lambda i: (jnp.minimum(i, n_p - 1), 0)),
                   pl.BlockSpec((tm_c, d), lambda i: (jnp.maximum(i - n_p, 0), 0))],
        out_shape=[jax.ShapeDtypeStruct((n_p * tm_c, d), F32),
                   jax.ShapeDtypeStruct((t_all - n_p * tm_c, d), F32)],
        scratch_shapes=[pltpu.VMEM((TOP_K, tm_c, d), F32), pltpu.SemaphoreType.DMA(())],
        compiler_params=_cp("arbitrary"),
        name="combine",
    )(dest_flat, eo, x1, topg, ada_p, ada_s, norm_final.reshape(1, d))


def _rope_tables(pos):
    half = ROPE_DIMS // 2
    inv_freq = jnp.exp(-math.log(ROPE_THETA) * jnp.arange(half, dtype=F32) / half)
    ang = pos.astype(F32)[:, None] * inv_freq[None, :]
    cos, sin = jnp.cos(ang), jnp.sin(ang)
    n = pos.shape[0]
    rest = HEAD_DIM - ROPE_DIMS
    cos_h = jnp.concatenate([cos, cos, jnp.ones((n, rest), F32)], axis=1)
    sa_h = jnp.concatenate([jnp.zeros((n, half), F32), sin, jnp.zeros((n, rest), F32)], axis=1)
    sb_h = jnp.concatenate([-sin, jnp.zeros((n, half + rest), F32)], axis=1)
    rep = LANES // HEAD_DIM
    return tuple(jnp.tile(t, (1, rep)) for t in (cos_h, sa_h, sb_h))


def _pad_lanes(v, n, value=0.0):
    return jnp.pad(v, ((0, 0), (0, n - v.shape[1])), constant_values=value)


def kernel(x_prompt, x_sample, state_conv_a, state_conv_ssm, state_ssm, cache_k, cache_v, c_prompt, c_sample, w_ada, b_ada, norm_mix, norm_ffn, w_in, conv_a_w, ssm_conv_w, ssm_conv_b, ssm_dt_bias, ssm_a_log, ssm_d, ssm_norm, attn_sinks, w_branch_gate, b_branch_gate, w_proj_a, w_proj_b, w_proj_c, w_out, router_w, router_b, w_gate_up, b_gate_up, w_down, b_down, norm_final):
    b_p, seq, d = x_prompt.shape
    b_s, steps, _ = x_sample.shape
    depth = w_ada.shape[0]
    t_p, t_s = b_p * seq, b_s * steps
    t_all = t_p + t_s
    tm = t_s
    dims = (d, tm, seq, b_p, b_s)
    n_e = router_w.shape[2]
    d_ssm = SSM_HEADS * SSM_HEAD_DIM
    n_bc = 2 * SSM_GROUPS * SSM_STATE
    d_kv = KV_HEADS * HEAD_DIM
    pn = SSM_HEAD_DIM * SSM_STATE
    assert d == d_ssm == ATTN_HEADS * HEAD_DIM and t_p % t_s == 0 and seq % t_s == 0
    assert t_s % 256 == 0 or t_s <= 256
    assert steps <= SUBLANES and PAST_LEN >= WINDOW and seq >= WINDOW and b_s % SUBLANES == 0

    x_all = jnp.concatenate([x_prompt.reshape(t_p, d), x_sample.transpose(1, 0, 2).reshape(t_s, d)], axis=0)
    ada = _ada_all(jnp.concatenate([c_prompt, c_sample], axis=0), w_ada, b_ada)

    o_z = 3 * d
    o_bc = o_z + 2 * d_ssm
    o_dt = o_bc + n_bc
    o_q = o_dt + SSM_HEADS
    o_k = o_q + d
    inv_order = [HEAD_ORDER.index(h) for h in range(ATTN_HEADS)]

    def to_head_order(a, axis):
        return jnp.concatenate([lax.slice_in_dim(a, h * HEAD_DIM, (h + 1) * HEAD_DIM, axis=axis)
                                for h in HEAD_ORDER], axis=axis)

    def from_head_order(a, axis):
        return jnp.concatenate([lax.slice_in_dim(a, i * HEAD_DIM, (i + 1) * HEAD_DIM, axis=axis)
                                for i in inv_order], axis=axis)

    w_pack = jnp.concatenate(
        [w_in[:, :, :o_bc], to_head_order(w_in[:, :, o_q:o_k], 2), w_in[:, :, o_bc:o_dt], w_in[:, :, o_k:],
         w_in[:, :, o_dt:o_q], jnp.zeros((depth, d, N_TAIL - T_DT - SSM_HEADS), F32)], axis=2).astype(MXU_DTYPE)

    hp = jnp.arange(d_ssm) // SSM_HEAD_DIM
    rexp = (jnp.arange(LANES)[:, None] == hp[None, :]).astype(MXU_DTYPE)
    lane_pn = jnp.arange(pn)
    e2 = jnp.stack([(jnp.arange(LANES)[:, None] == (lane_pn // SSM_STATE + lh * SSM_HEAD_DIM)[None, :])
                    for lh in range(2)]).astype(MXU_DTYPE)
    g2 = jnp.transpose(e2, (0, 2, 1))
    qc = min(SSD_CHUNK, seq)
    shift = jnp.stack([jnp.arange(2 * qc)[None, :] == (qc - s + jnp.arange(qc))[:, None]
                       for s in range(1, SSM_CONV_WIDTH)]).astype(MXU_DTYPE)
    state_ssm3 = state_ssm.reshape(depth, b_s, SSM_HEADS * pn)
    cache_k4 = cache_k.reshape(depth, b_s, WINDOW, d_kv)
    cache_v4 = cache_v.reshape(depth, b_s, WINDOW, d_kv)

    tabs_p = _rope_tables(jnp.arange(seq, dtype=I32))
    tabs_s = _rope_tables(jnp.repeat(PAST_LEN + jnp.arange(steps, dtype=I32), b_s))
    eye_kv = jnp.eye(KV_HEADS, dtype=F32)

    outs = [[] for _ in range(10)]
    pending = None
    for l in range(depth):
        ada_p = ada[l, :b_p].reshape(b_p, 1, 6 * d)
        ada_s = ada[l, b_p:]
        ssm_prm = dict(
            wx=ssm_conv_w[l, :, :d_ssm], wbc=ssm_conv_w[l, :, d_ssm:],
            bx=ssm_conv_b[l, :d_ssm].reshape(1, d_ssm), bbc=ssm_conv_b[l, d_ssm:].reshape(1, n_bc),
            dtb=_pad_lanes(ssm_dt_bias[l].reshape(1, SSM_HEADS), LANES),
            alog=_pad_lanes(ssm_a_log[l].reshape(1, SSM_HEADS), LANES),
            de=jnp.repeat(ssm_d[l], SSM_HEAD_DIM).reshape(1, d_ssm),
            gn=ssm_norm[l].reshape(1, d_ssm), rexp=rexp, e2=e2, g2=g2, shift=shift)
        wts = dict(
            wg=w_branch_gate[l].astype(MXU_DTYPE), bg=b_branch_gate[l].reshape(1, 3 * d),
            pa=w_proj_a[l].astype(MXU_DTYPE), pb=w_proj_b[l].astype(MXU_DTYPE),
            pc=to_head_order(w_proj_c[l], 0).astype(MXU_DTYPE), wo=w_out[l].astype(MXU_DTYPE),
            nffn=norm_ffn[l].reshape(1, d),
            rw=_pad_lanes(router_w[l], LANES), rb=_pad_lanes(router_b[l].reshape(1, n_e), LANES, NEG_BIG))

        if pending is None:
            p, p_tail, h = _inproj(x_all, norm_mix[l], ada_p, ada_s, w_pack, l, dims)
        else:
            p, p_tail, h, x_all = _inproj_combine(*pending, norm_mix[l], ada_p, ada_s, w_pack, l, dims)

        ya_p, ca_p = _conva_prompt(p, conv_a_w[l], dims, t_p)
        ca_tm = state_conv_a[l].transpose(1, 0, 2).reshape((CONV_A_WIDTH - 1) * b_s, d)
        ya_s, ca_s = _conva_sample(p, ca_tm, conv_a_w[l], dims, steps)
        outs[0].append(ca_p[:, SUBLANES - (CONV_A_WIDTH - 1):])
        outs[1].append(ca_s.reshape(CONV_A_WIDTH - 1, b_s, d).transpose(1, 0, 2))

        yb_p, csx_p, csbc_p, st_p = _ssd_prompt(p, p_tail, ssm_prm, dims, t_p)
        nsc = SSM_CONV_WIDTH - 1
        outs[2].append(jnp.concatenate([csx_p[:, SUBLANES - nsc:], csbc_p[:, SUBLANES - nsc:]], axis=2))
        hg = SSM_HEADS // SSM_GROUPS
        st_g = jnp.stack([st_p[:, g * SSM_STATE:(g + 1) * SSM_STATE, g * (d_ssm // 2):(g + 1) * (d_ssm // 2)]
                          for g in range(SSM_GROUPS)], axis=1)
        outs[4].append(st_g.reshape(b_p, SSM_GROUPS, SSM_STATE, hg, SSM_HEAD_DIM)
                       .transpose(0, 1, 3, 4, 2).reshape(b_p, SSM_HEADS, SSM_HEAD_DIM, SSM_STATE))
        cs_tm = state_conv_ssm[l].transpose(1, 0, 2).reshape(nsc * b_s, d_ssm + n_bc)
        y_raw, st_s, csx_s, csbc_s = _ssd_sample(p, p_tail, cs_tm[:, :d_ssm], cs_tm[:, d_ssm:],
                                                 state_ssm3, l, ssm_prm, dims, steps)
        yb_s = _gnorm_sample(y_raw, p, ssm_prm["gn"], dims, steps)
        outs[3].append(jnp.concatenate([csx_s, csbc_s], axis=1).reshape(nsc, b_s, d_ssm + n_bc).transpose(1, 0, 2))
        outs[5].append(st_s.reshape(b_s, SSM_HEADS, SSM_HEAD_DIM, SSM_STATE))

        yc_p, k_p, v_p = _attn_prompt(p, p_tail, attn_sinks[l], tabs_p, dims, t_p)
        outs[6].append(k_p.reshape(b_p, WINDOW, KV_HEADS, HEAD_DIM))
        outs[8].append(v_p.reshape(b_p, WINDOW, KV_HEADS, HEAD_DIM))
        q_s, k_s = _rope_sample(p, p_tail, tabs_s, dims, steps)
        v_s = p_tail[t_p:, T_V:T_V + d_kv]
        k_rows = k_s.reshape(steps, b_s, d_kv).transpose(1, 0, 2)
        v_rows = v_s.reshape(steps, b_s, d_kv).transpose(1, 0, 2)
        outs[7].append(k_rows)
        outs[9].append(v_rows)
        pad8 = ((0, 0), (0, SUBLANES - steps), (0, 0))
        q4 = from_head_order(q_s, 1).reshape(steps, b_s, KV_HEADS, Q_PER_KV, HEAD_DIM).transpose(1, 2, 3, 0, 4)
        qbd = (q4.reshape(b_s, KV_HEADS, Q_PER_KV * steps, 1, HEAD_DIM)
               * eye_kv[None, :, None, :, None]).reshape(b_s, KV_HEADS * Q_PER_KV * steps, d_kv)
        sink_col = jnp.broadcast_to(jnp.repeat(attn_sinks[l], steps)[:, None], (ATTN_HEADS * steps, LANES))
        o_s = _attn_sample(qbd, cache_k4, cache_v4, jnp.pad(k_rows, pad8), jnp.pad(v_rows, pad8), l,
                           sink_col, steps)
        yc_s = to_head_order(
            o_s.reshape(b_s, KV_HEADS, Q_PER_KV, steps, HEAD_DIM).transpose(3, 0, 1, 2, 4).reshape(t_s, d), 1
        ).astype(MXU_DTYPE)

        x1, h2, topi, topg = _merge(h, (ya_p, yb_p, yc_p), (ya_s, yb_s, yc_s), x_all, wts, ada_p, ada_s, dims)
        dest, cnt = _rank(topi, tm)
        dest_flat = dest[:, :TOP_K].reshape(t_all * TOP_K)
        xs = _dispatch(dest_flat, h2, min(256, tm))
        tmx = min(256, tm)
        meta = _expert_schedule(cnt[0, :n_e].astype(I32), t_all * TOP_K, tmx)
        eo = _experts(meta, xs, w_gate_up, b_gate_up, w_down, b_down, l, tmx)
        pending = (dest_flat, eo, x1, topg, ada_p, ada_s)

    y_p, y_s = _combine(*pending, norm_final, dims)
    y_prompt = y_p.reshape(b_p, seq, d)
    y_sample = y_s.reshape(steps, b_s, d).transpose(1, 0, 2)
    st = [jnp.stack(o) for o in outs]
    k_sample = jnp.concatenate([cache_k[:, :, steps:], st[7].reshape(depth, b_s, steps, KV_HEADS, HEAD_DIM)], axis=2)
    v_sample = jnp.concatenate([cache_v[:, :, steps:], st[9].reshape(depth, b_s, steps, KV_HEADS, HEAD_DIM)], axis=2)
    return (y_prompt, y_sample, st[0], st[1], st[2], st[3], st[4], st[5], st[6], k_sample, st[8], v_sample)
```
